```python
import jax, jax.numpy as jnp
from jax import lax
import numpy as np

D_MODEL = 2048
BATCH = 4
SEQ = 2048
DEPTH = 4

D_MIX = D_MODEL
HEAD_DIM = 128
MLSTM_HEADS = 4
MLSTM_WIDTH = MLSTM_HEADS * HEAD_DIM
MLSTM_CHUNK = 64
CONV_WIDTH = D_MIX // 4
CONV_LEN = 31
NSA_HEADS = 8
NSA_KV_HEADS = 2
NSA_HPG = NSA_HEADS // NSA_KV_HEADS
NSA_WIDTH = NSA_HEADS * HEAD_DIM
NSA_KV_WIDTH = NSA_KV_HEADS * HEAD_DIM
CMP_LEN = 32
CMP_STRIDE = 16
SEL_BLOCK = 64
SEL_COUNT = 16
WINDOW = 512
WIN_Q_BLOCK = 128
SEL_Q_BLOCK = 32
ROPE_THETA = 10000.0
D_FF = 5632
FFN_CONV_LEN = 3
EPS = 1e-6
NEG = -1e30
FORCE = 1e4
IN_SPLITS = (MLSTM_WIDTH, MLSTM_WIDTH, MLSTM_WIDTH, MLSTM_WIDTH, MLSTM_HEADS, MLSTM_HEADS,
             CONV_WIDTH, CONV_WIDTH,
             NSA_WIDTH, NSA_KV_WIDTH, NSA_KV_WIDTH, NSA_KV_WIDTH, NSA_KV_WIDTH, NSA_KV_WIDTH, NSA_KV_WIDTH,
             3 * NSA_HEADS)
N_IN = 4 * MLSTM_WIDTH + 2 * MLSTM_HEADS + 2 * CONV_WIDTH + NSA_WIDTH + 6 * NSA_KV_WIDTH + 3 * NSA_HEADS

kernel_name = "hybrid_mlstm_conformer_nsa_trunk"


def rms_norm(x, w):
    xf = x.astype(jnp.float32)
    y = xf * lax.rsqrt(jnp.mean(xf * xf, axis=-1, keepdims=True) + EPS)
    return (y * w.astype(jnp.float32)).astype(x.dtype)


def layer_norm(x, w, b):
    xf = x.astype(jnp.float32)
    mu = jnp.mean(xf, axis=-1, keepdims=True)
    xc = xf - mu
    y = xc * lax.rsqrt(jnp.mean(xc * xc, axis=-1, keepdims=True) + EPS)
    return (y * w.astype(jnp.float32) + b.astype(jnp.float32)).astype(x.dtype)


def rope(x, pos):
    d = x.shape[-1]
    half = d // 2
    inv = ROPE_THETA ** (-jnp.arange(half, dtype=jnp.float32) / half)
    ang = jnp.asarray(pos).astype(jnp.float32)[:, None] * inv[None, :]
    cos, sin = jnp.cos(ang), jnp.sin(ang)
    xf = x.astype(jnp.float32)
    x1, x2 = xf[..., :half], xf[..., half:]
    return jnp.concatenate([x1 * cos - x2 * sin, x2 * cos + x1 * sin], axis=-1).astype(x.dtype)


def causal_dwconv(x, w):
    k, c = w.shape
    return lax.conv_general_dilated(x, w[:, None, :].astype(x.dtype), window_strides=(1,),
                                    padding=[(k - 1, 0)], dimension_numbers=('NWC', 'WIO', 'NWC'),
                                    feature_group_count=c)


def masked_softmax(s, valid):
    return jax.nn.softmax(jnp.where(valid, s, NEG), axis=-1)


def mlstm_chunkwise(q, k, v, i_pre, f_pre):
    B, H, T, D = q.shape
    L = MLSTM_CHUNK
    NC = T // L
    f32 = jnp.float32

    def chunks(a):
        a = a.astype(f32).reshape((B, H, NC, L) + a.shape[3:])
        return jnp.moveaxis(a, 2, 0)

    logf = jax.nn.log_sigmoid(f_pre.astype(f32))
    causal = jnp.tril(jnp.ones((L, L), dtype=bool))

    def step(carry, inp):
        C, n, m = carry
        qc, kc, vc, ic, lfc = inp
        b = jnp.cumsum(lfc, axis=-1)
        dmat = jnp.where(causal, b[..., :, None] - b[..., None, :] + ic[..., None, :], -jnp.inf)
        inter = b + m[..., None]
        m_t = jnp.maximum(inter, dmat.max(-1))
        s = jnp.einsum('bhtd,bhsd->bhts', qc, kc) * jnp.exp(dmat - m_t[..., None])
        w_inter = jnp.exp(inter - m_t)
        num = jnp.einsum('bhts,bhse->bhte', s, vc) + w_inter[..., None] * jnp.einsum('bhtd,bhde->bhte', qc, C)
        den = s.sum(-1) + w_inter * jnp.einsum('bhtd,bhd->bht', qc, n)
        h = num / jnp.maximum(jnp.abs(den), jnp.exp(-m_t))[..., None]
        b_last = b[..., -1]
        g = b_last[..., None] - b + ic
        m_new = jnp.maximum(b_last + m, g.max(-1))
        w_k = jnp.exp(g - m_new[..., None])
        decay = jnp.exp(b_last + m - m_new)
        C_new = decay[..., None, None] * C + jnp.einsum('bhs,bhsd,bhse->bhde', w_k, kc, vc)
        n_new = decay[..., None] * n + jnp.einsum('bhs,bhsd->bhd', w_k, kc)
        return (C_new, n_new, m_new), h

    init = (jnp.zeros((B, H, D, D), f32), jnp.zeros((B, H, D), f32), jnp.full((B, H), -jnp.inf, f32))
    xs = (chunks(q), chunks(k) * (D ** -0.5), chunks(v), chunks(i_pre), chunks(logf))
    _, h = lax.scan(step, init, xs)
    return jnp.moveaxis(h, 0, 2).reshape(B, H, T, D)


def mlstm_mixer(q, k, v, o, i_pre, f_pre, norm_w):
    B, T, _ = q.shape
    heads = lambda a: a.reshape(B, T, MLSTM_HEADS, HEAD_DIM).transpose(0, 2, 1, 3)
    h = mlstm_chunkwise(heads(q), heads(k), heads(v), i_pre.transpose(0, 2, 1), f_pre.transpose(0, 2, 1))
    h = h.transpose(0, 2, 1, 3)
    mu = jnp.mean(h, axis=-1, keepdims=True)
    hc = h - mu
    h = hc * lax.rsqrt(jnp.mean(hc * hc, axis=-1, keepdims=True) + EPS)
    h = h * norm_w.astype(jnp.float32).reshape(MLSTM_HEADS, HEAD_DIM)
    h = h.reshape(B, T, MLSTM_WIDTH).astype(q.dtype)
    return jax.nn.sigmoid(o) * h


def conformer_conv(a, g, dw_w, dw_b, ln_w, ln_b):
    u = a * jax.nn.sigmoid(g)
    u = causal_dwconv(u, dw_w) + dw_b.astype(u.dtype)
    u = layer_norm(u, ln_w, ln_b)
    return jax.nn.silu(u)


def selected_attention(q, kb, vb, sel):
    B, G, HPG, T, D = q.shape
    K = sel.shape[-1]
    C = SEL_Q_BLOCK
    NQ = T // C
    qs = q.reshape(B, G, HPG, NQ, C, D).transpose(3, 0, 1, 2, 4, 5)
    ss = sel.reshape(B, G, NQ, C, K).transpose(2, 0, 1, 3, 4)
    ps = jnp.arange(T).reshape(NQ, C)
    bi = jnp.arange(B)[:, None, None, None]
    gi = jnp.arange(G)[None, :, None, None]
    offs = jnp.arange(SEL_BLOCK)
    scale = D ** -0.5

    def body(args):
        qc, sc, pc = args
        kg = kb[bi, gi, sc].reshape(B, G, C, K * SEL_BLOCK, D)
        vg = vb[bi, gi, sc].reshape(B, G, C, K * SEL_BLOCK, D)
        kpos = (sc[..., None] * SEL_BLOCK + offs).reshape(B, G, C, K * SEL_BLOCK)
        valid = (kpos <= pc[:, None])[:, :, None]
        s = jnp.einsum('bghcd,bgcnd->bghcn', qc, kg).astype(jnp.float32) * scale
        p = masked_softmax(s, valid)
        return jnp.einsum('bghcn,bgcnd->bghcd', p.astype(vg.dtype), vg)

    out = lax.map(body, (qs, ss, ps))
    return out.transpose(1, 2, 3, 0, 4, 5).reshape(B, G, HPG, T, D)


def window_attention(q, k, v):
    B, G, HPG, T, D = q.shape
    nqb = T // WIN_Q_BLOCK
    kp = jnp.pad(k, ((0, 0), (0, 0), (WINDOW, 0), (0, 0)))
    vp = jnp.pad(v, ((0, 0), (0, 0), (WINDOW, 0), (0, 0)))
    idx = (np.arange(nqb) * WIN_Q_BLOCK)[:, None] + np.arange(WIN_Q_BLOCK + WINDOW)[None, :]
    kb, vb = kp[:, :, idx], vp[:, :, idx]
    qb = q.reshape(B, G, HPG, nqb, WIN_Q_BLOCK, D)
    s = jnp.einsum('bghnqd,bgnkd->bghnqk', qb, kb).astype(jnp.float32) * (D ** -0.5)
    qpos = np.arange(T).reshape(nqb, WIN_Q_BLOCK)[:, :, None]
    kpos = (idx - WINDOW)[:, None, :]
    valid = (kpos >= 0) & (kpos <= qpos) & (qpos - kpos < WINDOW)
    p = masked_softmax(s, jnp.asarray(valid))
    o = jnp.einsum('bghnqk,bgnkd->bghnqd', p.astype(vb.dtype), vb)
    return o.reshape(B, G, HPG, T, D)


def nsa_mixer(q, k_cmp, v_cmp, k_slc, v_slc, k_win, v_win, gates, pe_k, w_k, pe_v, w_v):
    B, T, _ = q.shape
    G, HPG, D = NSA_KV_HEADS, NSA_HPG, HEAD_DIM
    pos = jnp.arange(T)
    q = rope(q.reshape(B, T, G, HPG, D).transpose(0, 2, 3, 1, 4), pos)
    heads_kv = lambda a: a.reshape(B, T, G, D).transpose(0, 2, 1, 3)
    scale = D ** -0.5

    n_cmp = (T - CMP_LEN) // CMP_STRIDE + 1
    cidx = np.arange(n_cmp)[:, None] * CMP_STRIDE + np.arange(CMP_LEN)[None, :]
    cend = np.arange(n_cmp) * CMP_STRIDE + CMP_LEN - 1

    def compress(a, pe, w):
        blk = heads_kv(a)[:, :, cidx] + pe.astype(a.dtype)
        return jnp.einsum('bgnf,fd->bgnd', blk.reshape(B, G, n_cmp, CMP_LEN * D), w.astype(a.dtype))

    kc = rope(compress(k_cmp, pe_k, w_k), cend)
    vc = compress(v_cmp, pe_v, w_v)
    valid_c = jnp.asarray(cend[None, :] <= np.arange(T)[:, None])
    s = jnp.einsum('bghtd,bgnd->bghtn', q, kc).astype(jnp.float32) * scale
    p_cmp = masked_softmax(s, valid_c) * valid_c
    o_cmp = jnp.einsum('bghtn,bgnd->bghtd', p_cmp.astype(vc.dtype), vc)

    n_sel = T // SEL_BLOCK
    k_sel = min(SEL_COUNT, n_sel)
    bstart = np.arange(n_sel) * SEL_BLOCK
    overlap = ((cidx[:, 0][:, None] <= bstart[None, :] + SEL_BLOCK - 1) &
               (cend[:, None] >= bstart[None, :])).astype(np.float32)
    imp = jnp.einsum('bghtn,nj->bgtj', p_cmp, jnp.asarray(overlap))
    j = np.arange(n_sel)[None, :]
    qblk = np.arange(T)[:, None] // SEL_BLOCK
    forced = jnp.asarray((j == 0) | (j == qblk) | (j == qblk - 1))
    valid_s = jnp.asarray(j * SEL_BLOCK <= np.arange(T)[:, None])
    score = jnp.where(forced, FORCE, jnp.where(valid_s, imp, -FORCE))
    _, sel = lax.top_k(score, k_sel)
    ks = rope(heads_kv(k_slc), pos).reshape(B, G, n_sel, SEL_BLOCK, D)
    vs = heads_kv(v_slc).reshape(B, G, n_sel, SEL_BLOCK, D)
    o_slc = selected_attention(q, ks, vs, sel)

    o_win = window_attention(q, rope(heads_kv(k_win), pos), heads_kv(v_win))

    g = jax.nn.sigmoid(gates).reshape(B, T, G, HPG, 3).transpose(0, 2, 3, 1, 4)
    o = g[..., 0:1] * o_cmp + g[..., 1:2] * o_slc + g[..., 2:3] * o_win
    return o.transpose(0, 3, 1, 2, 4).reshape(B, T, NSA_WIDTH)


def setup_inputs(seed: int = 0) -> dict:
    key = jax.random.key(seed)
    ks = jax.random.split(key, 24)
    nrm = lambda k, shape, s: jax.random.normal(k, shape, jnp.float32) * s
    L = DEPTH
    return {
        "x": nrm(ks[0], (BATCH, SEQ, D_MODEL), 1.0),
        "attn_norm_w": 1.0 + nrm(ks[1], (L, D_MODEL), 0.02),
        "w_in": nrm(ks[2], (L, D_MODEL, N_IN), D_MODEL ** -0.5),
        "mlstm_i_bias": nrm(ks[3], (L, MLSTM_HEADS), 0.1),
        "mlstm_f_bias": jnp.linspace(3.0, 6.0, MLSTM_HEADS, dtype=jnp.float32)[None, :] + nrm(ks[4], (L, MLSTM_HEADS), 0.1),
        "mlstm_norm_w": 1.0 + nrm(ks[5], (L, MLSTM_WIDTH), 0.02),
        "conv_dw_w": nrm(ks[6], (L, CONV_LEN, CONV_WIDTH), CONV_LEN ** -0.5),
        "conv_dw_b": nrm(ks[7], (L, CONV_WIDTH), 0.02),
        "conv_ln_w": 1.0 + nrm(ks[8], (L, CONV_WIDTH), 0.02),
        "conv_ln_b": nrm(ks[9], (L, CONV_WIDTH), 0.02),
        "nsa_cmp_pe_k": nrm(ks[10], (L, CMP_LEN, HEAD_DIM), 0.02),
        "nsa_cmp_w_k": nrm(ks[11], (L, CMP_LEN * HEAD_DIM, HEAD_DIM), (CMP_LEN * HEAD_DIM) ** -0.5),
        "nsa_cmp_pe_v": nrm(ks[12], (L, CMP_LEN, HEAD_DIM), 0.02),
        "nsa_cmp_w_v": nrm(ks[13], (L, CMP_LEN * HEAD_DIM, HEAD_DIM), (CMP_LEN * HEAD_DIM) ** -0.5),
        "w_out": nrm(ks[14], (L, D_MIX, D_MODEL), (D_MIX * 2 * DEPTH) ** -0.5),
        "ffn_norm_w": 1.0 + nrm(ks[15], (L, D_MODEL), 0.02),
        "w_up": nrm(ks[16], (L, D_MODEL, 2 * D_FF), D_MODEL ** -0.5),
        "ffn_dw_w": nrm(ks[17], (L, FFN_CONV_LEN, 2 * D_FF), FFN_CONV_LEN ** -0.5),
        "w_down": nrm(ks[18], (L, D_FF, D_MODEL), (D_FF * 2 * DEPTH) ** -0.5),
        "final_norm_w": 1.0 + nrm(ks[19], (D_MODEL,), 0.02),
    }


def reference(x, attn_norm_w, w_in, mlstm_i_bias, mlstm_f_bias, mlstm_norm_w, conv_dw_w, conv_dw_b,
              conv_ln_w, conv_ln_b, nsa_cmp_pe_k, nsa_cmp_w_k, nsa_cmp_pe_v, nsa_cmp_w_v, w_out,
              ffn_norm_w, w_up, ffn_dw_w, w_down, final_norm_w):
    split_at = list(np.cumsum(IN_SPLITS)[:-1])
    for l in range(DEPTH):
        h = rms_norm(x, attn_norm_w[l])
        proj = jnp.einsum('btd,dn->btn', h, w_in[l])
        (mq, mk, mv, mo, mi, mf, ca, cg, nq, nkc, nvc, nks, nvs, nkw, nvw, ngt) = jnp.split(proj, split_at, axis=-1)
        y_a = mlstm_mixer(mq, mk, mv, mo, mi + mlstm_i_bias[l], mf + mlstm_f_bias[l], mlstm_norm_w[l])
        y_b = conformer_conv(ca, cg, conv_dw_w[l], conv_dw_b[l], conv_ln_w[l], conv_ln_b[l])
        y_c = nsa_mixer(nq, nkc, nvc, nks, nvs, nkw, nvw, ngt,
                        nsa_cmp_pe_k[l], nsa_cmp_w_k[l], nsa_cmp_pe_v[l], nsa_cmp_w_v[l])
        y = jnp.concatenate([y_a, y_b, y_c], axis=-1)
        x = x + jnp.einsum('btm,md->btd', y, w_out[l])
        h = rms_norm(x, ffn_norm_w[l])
        u = causal_dwconv(jnp.einsum('btd,df->btf', h, w_up[l]), ffn_dw_w[l])
        gate, val = jnp.split(u, 2, axis=-1)
        x = x + jnp.einsum('btf,fd->btd', jax.nn.silu(gate) * val, w_down[l])
    return rms_norm(x, final_norm_w)
```

```python
import functools

import numpy as np
import jax
import jax.numpy as jnp
from jax import lax
from jax.experimental import pallas as pl
from jax.experimental.pallas import tpu as pltpu

F32 = jnp.float32
BF16 = jnp.bfloat16

D_MODEL = 2048
SEQ = 2048
HEAD_DIM = 128
MLSTM_HEADS = 4
MLSTM_WIDTH = MLSTM_HEADS * HEAD_DIM
MLSTM_CHUNK = 64
CONV_WIDTH = 512
CONV_LEN = 31
NSA_HEADS = 8
NSA_KV_HEADS = 2
NSA_HPG = NSA_HEADS // NSA_KV_HEADS
NSA_WIDTH = NSA_HEADS * HEAD_DIM
CMP_LEN = 32
CMP_STRIDE = 16
N_CMP = (SEQ - CMP_LEN) // CMP_STRIDE + 1
SEL_BLOCK = 64
SEL_COUNT = 16
N_SEL = SEQ // SEL_BLOCK
WINDOW = 512
ROPE_THETA = 10000.0
D_FF = 5632
FFN_CONV_LEN = 3
EPS = 1e-6
NEG = -1e30
FORCE = 1e4

OFF_MQ, OFF_MK, OFF_MV, OFF_MO = 0, 512, 1024, 1536
OFF_CA, OFF_CG = 2048, 2560
OFF_NQ = 3072
OFF_NKC, OFF_NVC, OFF_NKS, OFF_NVS, OFF_NKW, OFF_NVW = 4096, 4352, 4608, 4864, 5120, 5376
OFF_SMALL = 5632
N_PROJ = 5760
LANE = 128
GATE_LANE0 = 8

VMEM_LIMIT = 52 * 1024 * 1024

TM_IN, TN_IN = 1024, 1152
TM_OUT, TN_OUT = 1024, 1024
TM_UP, TN_UP = 512, 512
TM_DN, TN_DN = 512, 512
TM_NORM = 512
MLSTM_ROWS = 2 * MLSTM_CHUNK
CONV_ROWS = 256
CONV_HALO = 32
FFN_HALO = 16
TQ = 256
TK = 256


def _cparams(*sem):
    return pltpu.CompilerParams(dimension_semantics=sem, vmem_limit_bytes=VMEM_LIMIT)


def _rms(x, w):
    return x * lax.rsqrt(jnp.mean(x * x, axis=-1, keepdims=True) + EPS) * w


def _norm_matmul_kernel(x_ref, nw_ref, w_ref, o_ref, h_scr):
    @pl.when(pl.program_id(1) == 0)
    def _():
        h_scr[...] = _rms(x_ref[...], nw_ref[...]).astype(BF16)

    o_ref[...] = jnp.dot(h_scr[...], w_ref[...], preferred_element_type=F32)


def _norm_matmul(x, nw, w):
    m, d = x.shape
    n = w.shape[1]
    return pl.pallas_call(
        _norm_matmul_kernel,
        grid=(m // TM_IN, n // TN_IN),
        in_specs=[
            pl.BlockSpec((TM_IN, d), lambda i, j: (i, 0)),
            pl.BlockSpec((1, d), lambda i, j: (0, 0)),
            pl.BlockSpec((d, TN_IN), lambda i, j: (0, j)),
        ],
        out_specs=pl.BlockSpec((TM_IN, TN_IN), lambda i, j: (i, j)),
        out_shape=jax.ShapeDtypeStruct((m, n), F32),
        scratch_shapes=[pltpu.VMEM((TM_IN, d), BF16)],
        compiler_params=_cparams("parallel", "arbitrary"),
        name="in_proj",
    )(x, nw, w)


def _log_sigmoid(x):
    return jnp.minimum(x, 0.0) - jnp.log1p(jnp.exp(-jnp.abs(x)))


def _mlstm_kernel(q_ref, k_ref, v_ref, o_ref, g_ref, gb_ref, nw_ref, y_ref, c_scr, n_scr, m_scr):
    L = MLSTM_CHUNK

    @pl.when(pl.program_id(1) == 0)
    def _():
        c_scr[...] = jnp.zeros_like(c_scr)
        n_scr[...] = jnp.zeros_like(n_scr)
        m_scr[...] = jnp.full_like(m_scr, -jnp.inf)

    gates = g_ref[...] + gb_ref[...]
    logf = _log_sigmoid(gates)
    row = lax.broadcasted_iota(jnp.int32, (L, L), 0)
    col = lax.broadcasted_iota(jnp.int32, (L, L), 1)
    causal = col <= row
    tri = causal.astype(F32)
    b_all = jnp.concatenate(
        [jnp.dot(tri, logf[c * L:(c + 1) * L], precision=lax.Precision.HIGHEST,
                 preferred_element_type=F32) for c in range(2)], axis=0)
    b_t = b_all.T
    g_t = gates.T
    scale = HEAD_DIM ** -0.5

    for h in range(MLSTM_HEADS):
        hs = slice(h * HEAD_DIM, (h + 1) * HEAD_DIM)
        k_all = k_ref[:, hs] * scale
        k_allt = k_all.T
        c_st = c_scr[h]
        n_st = n_scr[h]
        m_st = m_scr[h][:, 0:1]
        for c in range(2):
            rs = slice(c * L, (c + 1) * L)
            qc = q_ref[rs, hs]
            kc = k_all[rs]
            vc = v_ref[rs, hs]
            b_col = b_all[rs, 4 + h:5 + h]
            i_col = gates[rs, h:h + 1]
            b_row = b_t[4 + h:5 + h, rs]
            i_row = g_t[h:h + 1, rs]
            dmat = jnp.where(causal, b_col - b_row + i_row, -jnp.inf)
            inter = b_col + m_st
            m_t = jnp.maximum(inter, jnp.max(dmat, axis=-1, keepdims=True))
            qb = qc.astype(BF16)
            s = lax.dot_general(qb, kc.astype(BF16), (((1,), (1,)), ((), ())),
                                preferred_element_type=F32)
            s = s * jnp.exp(dmat - m_t)
            w_inter = jnp.exp(inter - m_t)
            num = (jnp.dot(s.astype(BF16), vc.astype(BF16), preferred_element_type=F32)
                   + w_inter * jnp.dot(qb, c_st.astype(BF16), preferred_element_type=F32))
            den = (jnp.sum(s, axis=-1, keepdims=True)
                   + w_inter * jnp.sum(qc * n_st, axis=-1, keepdims=True))
            hh = num / jnp.maximum(jnp.abs(den), jnp.exp(-m_t))
            b_last = b_col[L - 1:L, :]
            g = b_last - b_col + i_col
            m_new = jnp.maximum(b_last + m_st, jnp.max(g, axis=0, keepdims=True))
            w_k = jnp.exp(g - m_new)
            decay = jnp.exp(b_last + m_st - m_new)
            c_st = decay * c_st + jnp.dot(k_allt[:, rs].astype(BF16), (w_k * vc).astype(BF16),
                                          preferred_element_type=F32)
            n_st = decay * n_st + jnp.sum(w_k * kc, axis=0, keepdims=True)
            m_st = m_new
            mu = jnp.mean(hh, axis=-1, keepdims=True)
            hc = hh - mu
            hn = hc * lax.rsqrt(jnp.mean(hc * hc, axis=-1, keepdims=True) + EPS) * nw_ref[:, hs]
            y_ref[rs, hs] = (jax.nn.sigmoid(o_ref[rs, hs]) * hn).astype(y_ref.dtype)
        c_scr[h] = c_st
        n_scr[h] = n_st
        m_scr[h] = jnp.broadcast_to(m_st, (1, LANE))


def _mlstm(proj, gate_bias, norm_w, batch):
    m = proj.shape[0]
    nsteps = SEQ // MLSTM_ROWS
    blk = lambda cb: pl.BlockSpec((MLSTM_ROWS, MLSTM_WIDTH), lambda b, c, cb=cb: (b * nsteps + c, cb))
    return pl.pallas_call(
        _mlstm_kernel,
        grid=(batch, nsteps),
        in_specs=[
            blk(OFF_MQ // MLSTM_WIDTH), blk(OFF_MK // MLSTM_WIDTH),
            blk(OFF_MV // MLSTM_WIDTH), blk(OFF_MO // MLSTM_WIDTH),
            pl.BlockSpec((MLSTM_ROWS, LANE), lambda b, c: (b * nsteps + c, OFF_SMALL // LANE)),
            pl.BlockSpec((1, LANE), lambda b, c: (0, 0)),
            pl.BlockSpec((1, MLSTM_WIDTH), lambda b, c: (0, 0)),
        ],
        out_specs=pl.BlockSpec((MLSTM_ROWS, MLSTM_WIDTH), lambda b, c: (b * nsteps + c, 0)),
        out_shape=jax.ShapeDtypeStruct((m, MLSTM_WIDTH), BF16),
        scratch_shapes=[
            pltpu.VMEM((MLSTM_HEADS, HEAD_DIM, HEAD_DIM), F32),
            pltpu.VMEM((MLSTM_HEADS, 1, HEAD_DIM), F32),
            pltpu.VMEM((MLSTM_HEADS, 1, LANE), F32),
        ],
        compiler_params=_cparams("parallel", "arbitrary"),
        name="mlstm",
    )(proj, proj, proj, proj, proj, gate_bias, norm_w)


def _conv_kernel(a_ref, g_ref, ah_ref, gh_ref, w_ref, b_ref, lw_ref, lb_ref, y_ref, u_scr):
    first = pl.program_id(1) == 0
    uh = ah_ref[...] * jax.nn.sigmoid(gh_ref[...])
    u_scr[0:CONV_HALO, :] = jnp.where(first, 0.0, uh)
    u_scr[CONV_HALO:, :] = a_ref[...] * jax.nn.sigmoid(g_ref[...])
    base = CONV_HALO - (CONV_LEN - 1)
    acc = jnp.zeros((CONV_ROWS, CONV_WIDTH), F32)
    for j in range(CONV_LEN):
        acc = acc + w_ref[j:j + 1, :] * u_scr[base + j:base + j + CONV_ROWS, :]
    u = acc + b_ref[...]
    mu = jnp.mean(u, axis=-1, keepdims=True)
    uc = u - mu
    y = uc * lax.rsqrt(jnp.mean(uc * uc, axis=-1, keepdims=True) + EPS) * lw_ref[...] + lb_ref[...]
    y_ref[...] = (y * jax.nn.sigmoid(y)).astype(y_ref.dtype)


def _conformer(proj, dw_w, dw_b, ln_w, ln_b, batch):
    m = proj.shape[0]
    nt = SEQ // CONV_ROWS
    hb = CONV_ROWS // CONV_HALO
    main = lambda cb: pl.BlockSpec((CONV_ROWS, CONV_WIDTH), lambda b, i, cb=cb: (b * nt + i, cb))
    halo = lambda cb: pl.BlockSpec(
        (CONV_HALO, CONV_WIDTH), lambda b, i, cb=cb: (jnp.maximum((b * nt + i) * hb - 1, 0), cb))
    vec = pl.BlockSpec((1, CONV_WIDTH), lambda b, i: (0, 0))
    return pl.pallas_call(
        _conv_kernel,
        grid=(batch, nt),
        in_specs=[
            main(OFF_CA // CONV_WIDTH), main(OFF_CG // CONV_WIDTH),
            halo(OFF_CA // CONV_WIDTH), halo(OFF_CG // CONV_WIDTH),
            pl.BlockSpec((CONV_LEN, CONV_WIDTH), lambda b, i: (0, 0)),
            vec, vec, vec,
        ],
        out_specs=pl.BlockSpec((CONV_ROWS, CONV_WIDTH), lambda b, i: (b * nt + i, 0)),
        out_shape=jax.ShapeDtypeStruct((m, CONV_WIDTH), BF16),
        scratch_shapes=[pltpu.VMEM((CONV_HALO + CONV_ROWS, CONV_WIDTH), F32)],
        compiler_params=_cparams("parallel", "arbitrary"),
        name="conformer_conv",
    )(proj, proj, proj, proj, dw_w, dw_b, ln_w, ln_b)


def _rope(x, cs, sn):
    return x * cs + pltpu.roll(x, HEAD_DIM // 2, axis=1) * sn


def _compress(x_ref, pe_ref, w_ref):
    nrow = SEQ // CMP_STRIDE
    lo = jnp.zeros((nrow, HEAD_DIM), F32)
    hi = jnp.zeros((nrow, HEAD_DIM), F32)
    for l in range(CMP_STRIDE):
        xl = x_ref[pl.ds(l, nrow, stride=CMP_STRIDE), :]
        lo = lo + jnp.dot((xl + pe_ref[l:l + 1, :]).astype(BF16), w_ref[l],
                          preferred_element_type=F32)
        hi = hi + jnp.dot((xl + pe_ref[CMP_STRIDE + l:CMP_STRIDE + l + 1, :]).astype(BF16),
                          w_ref[CMP_STRIDE + l], preferred_element_type=F32)
    return lo + pltpu.roll(hi, nrow - 1, axis=0)


def _nsa_prep_kernel(kc_ref, vc_ref, ks_ref, vs_ref, kw_ref, vw_ref, pek_ref, wk_ref, pev_ref, wv_ref,
                     cs_ref, sn_ref, ccs_ref, csn_ref,
                     kco_ref, vco_ref, kso_ref, vso_ref, kwo_ref, vwo_ref):
    kco_ref[...] = _rope(_compress(kc_ref, pek_ref, wk_ref), ccs_ref[...], csn_ref[...]).astype(BF16)
    vco_ref[...] = _compress(vc_ref, pev_ref, wv_ref).astype(BF16)
    cs = cs_ref[...]
    sn = sn_ref[...]
    kso_ref[...] = _rope(ks_ref[...], cs, sn).astype(BF16)
    kwo_ref[...] = _rope(kw_ref[...], cs, sn).astype(BF16)
    vso_ref[...] = vs_ref[...].astype(BF16)
    vwo_ref[...] = vw_ref[...].astype(BF16)


def _nsa_prep(proj, pe_k, w_k, pe_v, w_v, tabs, batch):
    cs, sn, ccs, csn = tabs
    G = NSA_KV_HEADS
    ncr = SEQ // CMP_STRIDE
    col = lambda off: pl.BlockSpec((SEQ, HEAD_DIM), lambda b, g, off=off: (b, off // HEAD_DIM + g))
    full2 = lambda r: pl.BlockSpec((r, HEAD_DIM), lambda b, g: (0, 0))
    wspec = pl.BlockSpec((CMP_LEN, HEAD_DIM, HEAD_DIM), lambda b, g: (0, 0, 0))
    ocmp = pl.BlockSpec((None, None, ncr, HEAD_DIM), lambda b, g: (b, g, 0, 0))
    oseq = pl.BlockSpec((None, None, SEQ, HEAD_DIM), lambda b, g: (b, g, 0, 0))
    cmp_shape = jax.ShapeDtypeStruct((batch, G, ncr, HEAD_DIM), BF16)
    seq_shape = jax.ShapeDtypeStruct((batch, G, SEQ, HEAD_DIM), BF16)
    return pl.pallas_call(
        _nsa_prep_kernel,
        grid=(batch, G),
        in_specs=[
            col(OFF_NKC), col(OFF_NVC), col(OFF_NKS), col(OFF_NVS), col(OFF_NKW), col(OFF_NVW),
            full2(CMP_LEN), wspec, full2(CMP_LEN), wspec,
            full2(SEQ), full2(SEQ), full2(ncr), full2(ncr),
        ],
        out_specs=[ocmp, ocmp, oseq, oseq, oseq, oseq],
        out_shape=[cmp_shape, cmp_shape, seq_shape, seq_shape, seq_shape, seq_shape],
        compiler_params=_cparams("parallel", "arbitrary"),
        name="nsa_prep",
    )(proj, proj, proj, proj, proj, proj, pe_k, w_k, pe_v, w_v, cs, sn, ccs, csn)


def _attend(q, k, v, bias, m, l, acc):
    s = lax.dot_general(q, k, (((1,), (1,)), ((), ())), preferred_element_type=F32) + bias
    m_new = jnp.maximum(m, jnp.max(s, axis=-1, keepdims=True))
    alpha = jnp.exp(m - m_new)
    p = jnp.exp(s - m_new)
    l = alpha * l + jnp.sum(p, axis=-1, keepdims=True)
    acc = alpha * acc + jnp.dot(p.astype(BF16), v, preferred_element_type=F32)
    return m_new, l, acc


def _nsa_kernel(q_ref, sm_ref, cs_ref, sn_ref, kc_ref, vc_ref, ks_ref, vs_ref, kw_ref, vw_ref, y_ref):
    g = pl.program_id(1)
    qi = pl.program_id(2)
    q0 = qi * TQ
    R = NSA_HPG * TQ
    scale = HEAD_DIM ** -0.5
    cs = cs_ref[...]
    sn = sn_ref[...]
    q = jnp.concatenate(
        [(_rope(q_ref[:, h * HEAD_DIM:(h + 1) * HEAD_DIM], cs, sn) * scale).astype(BF16)
         for h in range(NSA_HPG)], axis=0)

    ncr = SEQ // CMP_STRIDE
    t_c = q0 + lax.broadcasted_iota(jnp.int32, (TQ, ncr), 0)
    n_c = lax.broadcasted_iota(jnp.int32, (TQ, ncr), 1)
    valid_c = (n_c * CMP_STRIDE + (CMP_LEN - 1)) <= t_c
    bias_c = jnp.where(valid_c, 0.0, NEG)
    s_c = lax.dot_general(q, kc_ref[...], (((1,), (1,)), ((), ())), preferred_element_type=F32)
    s_c = s_c + jnp.concatenate([bias_c] * NSA_HPG, axis=0)
    m_c = jnp.max(s_c, axis=-1, keepdims=True)
    keep_c = jnp.concatenate([valid_c.astype(F32)] * NSA_HPG, axis=0)
    e_c = jnp.exp(s_c - m_c) * keep_c
    p_c = e_c / jnp.maximum(jnp.sum(e_c, axis=-1, keepdims=True), 1e-30)
    o_cmp = jnp.dot(p_c.astype(BF16), vc_ref[...], preferred_element_type=F32)
    p_sum = p_c[0:TQ] + p_c[TQ:2 * TQ] + p_c[2 * TQ:3 * TQ] + p_c[3 * TQ:4 * TQ]

    jj = lax.broadcasted_iota(jnp.int32, (LANE, ncr), 0)
    nn = lax.broadcasted_iota(jnp.int32, (LANE, ncr), 1)
    ov_t = ((nn * CMP_STRIDE <= jj * SEL_BLOCK + (SEL_BLOCK - 1))
            & (nn * CMP_STRIDE + (CMP_LEN - 1) >= jj * SEL_BLOCK)
            & (jj < N_SEL) & (nn < N_CMP)).astype(F32)
    imp_t = lax.dot_general(ov_t, p_sum, (((1,), (1,)), ((), ())), precision=lax.Precision.HIGHEST,
                            preferred_element_type=F32)[0:N_SEL]
    j_i = lax.broadcasted_iota(jnp.int32, (N_SEL, TQ), 0)
    t_i = q0 + lax.broadcasted_iota(jnp.int32, (N_SEL, TQ), 1)
    qblk = t_i // SEL_BLOCK
    forced = (j_i == 0) | (j_i == qblk) | (j_i == qblk - 1)
    score = jnp.where(forced, FORCE, jnp.where(j_i * SEL_BLOCK <= t_i, imp_t, -FORCE))
    rank = jnp.zeros((N_SEL, TQ), jnp.int32)
    for jp in range(N_SEL):
        row = score[jp:jp + 1, :]
        beats = (row > score) | ((row == score) & (j_i > jp))
        rank = rank + beats.astype(jnp.int32)
    sel_t = (rank < SEL_COUNT).astype(F32)
    sel = jnp.concatenate([sel_t, jnp.zeros((LANE - N_SEL, TQ), F32)], axis=0).T.astype(BF16)

    t_q = q0 + lax.broadcasted_iota(jnp.int32, (TQ, TK), 0)
    s_k = lax.broadcasted_iota(jnp.int32, (TQ, TK), 1)
    e_j = lax.broadcasted_iota(jnp.int32, (LANE, TK), 0)
    e_s = lax.broadcasted_iota(jnp.int32, (LANE, TK), 1)

    m0 = jnp.full((R, 1), NEG, F32)
    l0 = jnp.zeros((R, 1), F32)
    a0 = jnp.zeros((R, HEAD_DIM), F32)

    def sel_step(kt, carry):
        k0 = pl.multiple_of(kt * TK, TK)
        expand = jnp.where((k0 + e_s) // SEL_BLOCK == e_j, 1.0, 0.0).astype(BF16)
        allow = jnp.dot(sel, expand, preferred_element_type=F32) > 0.5
        ok = allow & ((k0 + s_k) <= t_q)
        bias = jnp.where(ok, 0.0, NEG)
        bias = jnp.concatenate([bias] * NSA_HPG, axis=0)
        return _attend(q, ks_ref[pl.ds(k0, TK), :], vs_ref[pl.ds(k0, TK), :], bias, *carry)

    m_s, l_s, a_s = lax.fori_loop(0, qi + 1, sel_step, (m0, l0, a0))
    o_slc = a_s / l_s

    carry = (m0, l0, a0)
    for back in range(WINDOW // TK + 1):
        kt = qi - back
        k0 = pl.multiple_of(jnp.maximum(kt, 0) * TK, TK)
        s_abs = k0 + s_k
        ok = (s_abs <= t_q) & (t_q - s_abs < WINDOW) & (kt >= 0)
        bias = jnp.where(ok, 0.0, NEG)
        bias = jnp.concatenate([bias] * NSA_HPG, axis=0)
        carry = _attend(q, kw_ref[pl.ds(k0, TK), :], vw_ref[pl.ds(k0, TK), :], bias, *carry)
    o_win = carry[2] / carry[1]

    gl = jax.nn.sigmoid(sm_ref[...])
    for h in range(NSA_HPG):
        rs = slice(h * TQ, (h + 1) * TQ)
        gate = []
        for c in range(3):
            l0_ = GATE_LANE0 + h * 3 + c
            l1_ = l0_ + NSA_HPG * 3
            gate.append(jnp.where(g == 0, gl[:, l0_:l0_ + 1], gl[:, l1_:l1_ + 1]))
        y = gate[0] * o_cmp[rs] + gate[1] * o_slc[rs] + gate[2] * o_win[rs]
        y_ref[:, h * HEAD_DIM:(h + 1) * HEAD_DIM] = y.astype(y_ref.dtype)


def _nsa(proj, prep, tabs, batch):
    kc, vc, ks, vs, kw, vw = prep
    cs, sn = tabs[0], tabs[1]
    m = proj.shape[0]
    G = NSA_KV_HEADS
    nq = SEQ // TQ
    ncr = SEQ // CMP_STRIDE
    gw = NSA_HPG * HEAD_DIM
    cmp_spec = pl.BlockSpec((None, None, ncr, HEAD_DIM), lambda b, g, i: (b, g, 0, 0))
    seq_spec = pl.BlockSpec((None, None, SEQ, HEAD_DIM), lambda b, g, i: (b, g, 0, 0))
    tab_spec = pl.BlockSpec((TQ, HEAD_DIM), lambda b, g, i: (i, 0))
    return pl.pallas_call(
        _nsa_kernel,
        grid=(batch, G, nq),
        in_specs=[
            pl.BlockSpec((TQ, gw), lambda b, g, i: (b * nq + i, OFF_NQ // gw + g)),
            pl.BlockSpec((TQ, LANE), lambda b, g, i: (b * nq + i, OFF_SMALL // LANE)),
            tab_spec, tab_spec, cmp_spec, cmp_spec, seq_spec, seq_spec, seq_spec, seq_spec,
        ],
        out_specs=pl.BlockSpec((TQ, gw), lambda b, g, i: (b * nq + i, g)),
        out_shape=jax.ShapeDtypeStruct((m, NSA_WIDTH), BF16),
        compiler_params=_cparams("parallel", "parallel", "arbitrary"),
        name="nsa_attn",
    )(proj, proj, cs, sn, kc, vc, ks, vs, kw, vw)


def _out_proj_kernel(ya_ref, yb_ref, yc_ref, wa_ref, wb_ref, wc_ref, x_ref, o_ref):
    acc = jnp.dot(ya_ref[...], wa_ref[...], preferred_element_type=F32)
    acc = acc + jnp.dot(yb_ref[...], wb_ref[...], preferred_element_type=F32)
    acc = acc + jnp.dot(yc_ref[...], wc_ref[...], preferred_element_type=F32)
    o_ref[...] = x_ref[...] + acc


def _out_proj(ya, yb, yc, w, x):
    m, d = x.shape
    wa, wb, wc = ya.shape[1], yb.shape[1], yc.shape[1]
    return pl.pallas_call(
        _out_proj_kernel,
        grid=(m // TM_OUT, d // TN_OUT),
        in_specs=[
            pl.BlockSpec((TM_OUT, wa), lambda i, j: (i, 0)),
            pl.BlockSpec((TM_OUT, wb), lambda i, j: (i, 0)),
            pl.BlockSpec((TM_OUT, wc), lambda i, j: (i, 0)),
            pl.BlockSpec((wa, TN_OUT), lambda i, j: (0, j)),
            pl.BlockSpec((wb, TN_OUT), lambda i, j: (wa // wb, j)),
            pl.BlockSpec((wc, TN_OUT), lambda i, j: ((wa + wb) // wc, j)),
            pl.BlockSpec((TM_OUT, TN_OUT), lambda i, j: (i, j)),
        ],
        out_specs=pl.BlockSpec((TM_OUT, TN_OUT), lambda i, j: (i, j)),
        out_shape=jax.ShapeDtypeStruct((m, d), F32),
        compiler_params=_cparams("parallel", "arbitrary"),
        name="out_proj",
    )(ya, yb, yc, w, w, w, x)


def _ffn_up_kernel(x_ref, xh_ref, nw_ref, wg_ref, wv_ref, cg_ref, cv_ref, o_ref, h_scr, hh_scr, g_scr, v_scr):
    i = pl.program_id(0)

    @pl.when(pl.program_id(1) == 0)
    def _():
        h_scr[...] = _rms(x_ref[...], nw_ref[...]).astype(BF16)
        hh_scr[...] = _rms(xh_ref[...], nw_ref[...]).astype(BF16)

    keep = jnp.where((i * TM_UP) % SEQ == 0, 0.0, 1.0)

    def branch(w_ref, c_ref, scr):
        cur = jnp.dot(h_scr[...], w_ref[...], preferred_element_type=F32)
        scr[0:FFN_HALO, :] = jnp.dot(hh_scr[...], w_ref[...], preferred_element_type=F32) * keep
        scr[FFN_HALO:, :] = cur
        out = c_ref[FFN_CONV_LEN - 1:FFN_CONV_LEN, :] * cur
        for j in range(FFN_CONV_LEN - 1):
            sh = FFN_HALO - (FFN_CONV_LEN - 1) + j
            out = out + c_ref[j:j + 1, :] * scr[sh:sh + TM_UP, :]
        return out

    gate = branch(wg_ref, cg_ref, g_scr)
    val = branch(wv_ref, cv_ref, v_scr)
    o_ref[...] = (gate * jax.nn.sigmoid(gate) * val).astype(o_ref.dtype)


def _ffn_up(x, nw, w_up, dw):
    m, d = x.shape
    nj = D_FF // TN_UP
    hb = TM_UP // FFN_HALO
    return pl.pallas_call(
        _ffn_up_kernel,
        grid=(m // TM_UP, nj),
        in_specs=[
            pl.BlockSpec((TM_UP, d), lambda i, j: (i, 0)),
            pl.BlockSpec((FFN_HALO, d), lambda i, j: (jnp.maximum(i * hb - 1, 0), 0)),
            pl.BlockSpec((1, d), lambda i, j: (0, 0)),
            pl.BlockSpec((d, TN_UP), lambda i, j: (0, j)),
            pl.BlockSpec((d, TN_UP), lambda i, j: (0, j + nj)),
            pl.BlockSpec((FFN_CONV_LEN, TN_UP), lambda i, j: (0, j)),
            pl.BlockSpec((FFN_CONV_LEN, TN_UP), lambda i, j: (0, j + nj)),
        ],
        out_specs=pl.BlockSpec((TM_UP, TN_UP), lambda i, j: (i, j)),
        out_shape=jax.ShapeDtypeStruct((m, D_FF), BF16),
        scratch_shapes=[
            pltpu.VMEM((TM_UP, d), BF16),
            pltpu.VMEM((FFN_HALO, d), BF16),
            pltpu.VMEM((FFN_HALO + TM_UP, TN_UP), F32),
            pltpu.VMEM((FFN_HALO + TM_UP, TN_UP), F32),
        ],
        compiler_params=_cparams("parallel", "arbitrary"),
        name="ffn_up",
    )(x, x, nw, w_up, w_up, dw, dw)


def _ffn_down_kernel(a_ref, w_ref, x_ref, o_ref):
    o_ref[...] = x_ref[...] + jnp.dot(a_ref[...], w_ref[...], preferred_element_type=F32)


def _ffn_down(a, w, x):
    m, d = x.shape
    f = a.shape[1]
    return pl.pallas_call(
        _ffn_down_kernel,
        grid=(m // TM_DN, d // TN_DN),
        in_specs=[
            pl.BlockSpec((TM_DN, f), lambda i, j: (i, 0)),
            pl.BlockSpec((f, TN_DN), lambda i, j: (0, j)),
            pl.BlockSpec((TM_DN, TN_DN), lambda i, j: (i, j)),
        ],
        out_specs=pl.BlockSpec((TM_DN, TN_DN), lambda i, j: (i, j)),
        out_shape=jax.ShapeDtypeStruct((m, d), F32),
        compiler_params=_cparams("parallel", "arbitrary"),
        name="ffn_down",
    )(a, w, x)


def _final_norm_kernel(x_ref, w_ref, o_ref):
    o_ref[...] = _rms(x_ref[...], w_ref[...])


def _final_norm(x, w):
    m, d = x.shape
    return pl.pallas_call(
        _final_norm_kernel,
        grid=(m // TM_NORM,),
        in_specs=[pl.BlockSpec((TM_NORM, d), lambda i: (i, 0)), pl.BlockSpec((1, d), lambda i: (0, 0))],
        out_specs=pl.BlockSpec((TM_NORM, d), lambda i: (i, 0)),
        out_shape=jax.ShapeDtypeStruct((m, d), F32),
        compiler_params=_cparams("parallel"),
        name="final_norm",
    )(x, w)


def _rope_tables():
    half = HEAD_DIM // 2
    inv = ROPE_THETA ** (-np.arange(half, dtype=np.float64) / half)

    def tab(pos):
        ang = np.asarray(pos, np.float64)[:, None] * inv[None, :]
        cos, sin = np.cos(ang), np.sin(ang)
        return (jnp.asarray(np.concatenate([cos, cos], -1), F32),
                jnp.asarray(np.concatenate([-sin, sin], -1), F32))

    cs, sn = tab(np.arange(SEQ))
    ccs, csn = tab(np.arange(SEQ // CMP_STRIDE) * CMP_STRIDE + CMP_LEN - 1)
    return cs, sn, ccs, csn


def _reorder_w_in(w):
    d = w.shape[0]
    a0 = 4 * MLSTM_WIDTH
    a1 = a0 + 2 * MLSTM_HEADS
    a2 = a1 + (OFF_SMALL - OFF_CA)
    a3 = a2 + 3 * NSA_HEADS
    pad = jnp.zeros((d, N_PROJ - a3), w.dtype)
    return jnp.concatenate([w[:, :a0], w[:, a1:a2], w[:, a0:a1], w[:, a2:a3], pad], axis=1).astype(BF16)


def kernel(x, attn_norm_w, w_in, mlstm_i_bias, mlstm_f_bias, mlstm_norm_w, conv_dw_w, conv_dw_b, conv_ln_w, conv_ln_b, nsa_cmp_pe_k, nsa_cmp_w_k, nsa_cmp_pe_v, nsa_cmp_w_v, w_out, ffn_norm_w, w_up, ffn_dw_w, w_down, final_norm_w):
    batch, seq, d = x.shape
    assert seq == SEQ and d == D_MODEL
    depth = w_in.shape[0]
    tabs = _rope_tables()
    xf = x.reshape(batch * seq, d)
    for l in range(depth):
        proj = _norm_matmul(xf, attn_norm_w[l][None, :], _reorder_w_in(w_in[l]))
        gate_bias = jnp.concatenate(
            [mlstm_i_bias[l], mlstm_f_bias[l], jnp.zeros((LANE - 2 * MLSTM_HEADS,), F32)])[None, :]
        y_a = _mlstm(proj, gate_bias, mlstm_norm_w[l][None, :], batch)
        y_b = _conformer(proj, conv_dw_w[l], conv_dw_b[l][None, :], conv_ln_w[l][None, :],
                         conv_ln_b[l][None, :], batch)
        prep = _nsa_prep(proj, nsa_cmp_pe_k[l],
                         nsa_cmp_w_k[l].reshape(CMP_LEN, HEAD_DIM, HEAD_DIM).astype(BF16),
                         nsa_cmp_pe_v[l],
                         nsa_cmp_w_v[l].reshape(CMP_LEN, HEAD_DIM, HEAD_DIM).astype(BF16), tabs, batch)
        y_c = _nsa(proj, prep, tabs, batch)
        xf = _out_proj(y_a, y_b, y_c, w_out[l].astype(BF16), xf)
        act = _ffn_up(xf, ffn_norm_w[l][None, :], w_up[l].astype(BF16), ffn_dw_w[l])
        xf = _ffn_down(act, w_down[l].astype(BF16), xf)
    return _final_norm(xf, final_norm_w[None, :]).reshape(batch, seq, d)
```

```python
import functools

import numpy as np
import jax
import jax.numpy as jnp
from jax import lax
from jax.experimental import pallas as pl
from jax.experimental.pallas import tpu as pltpu

F32 = jnp.float32
BF16 = jnp.bfloat16

D_MODEL = 2048
SEQ = 2048
HEAD_DIM = 128
MLSTM_HEADS = 4
MLSTM_WIDTH = MLSTM_HEADS * HEAD_DIM
MLSTM_CHUNK = 64
CONV_WIDTH = 512
CONV_LEN = 31
NSA_HEADS = 8
NSA_KV_HEADS = 2
NSA_HPG = NSA_HEADS // NSA_KV_HEADS
NSA_WIDTH = NSA_HEADS * HEAD_DIM
CMP_LEN = 32
CMP_STRIDE = 16
N_CMP = (SEQ - CMP_LEN) // CMP_STRIDE + 1
SEL_BLOCK = 64
SEL_COUNT = 16
N_SEL = SEQ // SEL_BLOCK
WINDOW = 512
ROPE_THETA = 10000.0
D_FF = 5632
FFN_CONV_LEN = 3
EPS = 1e-6
NEG = -1e30
FORCE = 1e4

OFF_MQ, OFF_MK, OFF_MV, OFF_MO = 0, 512, 1024, 1536
OFF_CA, OFF_CG = 2048, 2560
OFF_NQ = 3072
OFF_NKC, OFF_NVC, OFF_NKS, OFF_NVS, OFF_NKW, OFF_NVW = 4096, 4352, 4608, 4864, 5120, 5376
OFF_SMALL = 5632
N_PROJ = 5760
LANE = 128
GATE_LANE0 = 8

VMEM_LIMIT = 52 * 1024 * 1024

TM_IN, TN_IN = 1024, 1152
TM_OUT, TN_OUT = 1024, 1024
TM_UP, TN_UP = 512, 512
TM_DN, TN_DN = 512, 512
TM_NORM = 512
MLSTM_ROWS = 2 * MLSTM_CHUNK
CONV_ROWS = 256
CONV_HALO = 32
FFN_HALO = 16
TQ = 256
TK = 256
assert TQ == TK and WINDOW == 2 * TK
LOG2E = 1.4426950408889634


def _cparams(*sem):
    return pltpu.CompilerParams(dimension_semantics=sem, vmem_limit_bytes=VMEM_LIMIT)


def _rms(x, w):
    return x * lax.rsqrt(jnp.mean(x * x, axis=-1, keepdims=True) + EPS) * w


def _norm_matmul_kernel(x_ref, nw_ref, w_ref, o_ref, h_scr):
    @pl.when(pl.program_id(1) == 0)
    def _():
        h_scr[...] = _rms(x_ref[...], nw_ref[...]).astype(BF16)

    o_ref[...] = jnp.dot(h_scr[...], w_ref[...], preferred_element_type=F32)


def _norm_matmul(x, nw, w):
    m, d = x.shape
    n = w.shape[1]
    return pl.pallas_call(
        _norm_matmul_kernel,
        grid=(m // TM_IN, n // TN_IN),
        in_specs=[
            pl.BlockSpec((TM_IN, d), lambda i, j: (i, 0)),
            pl.BlockSpec((1, d), lambda i, j: (0, 0)),
            pl.BlockSpec((d, TN_IN), lambda i, j: (0, j)),
        ],
        out_specs=pl.BlockSpec((TM_IN, TN_IN), lambda i, j: (i, j)),
        out_shape=jax.ShapeDtypeStruct((m, n), F32),
        scratch_shapes=[pltpu.VMEM((TM_IN, d), BF16)],
        compiler_params=_cparams("parallel", "arbitrary"),
        name="in_proj",
    )(x, nw, w)


def _log_sigmoid(x):
    return jnp.minimum(x, 0.0) - jnp.log1p(jnp.exp(-jnp.abs(x)))


def _mlstm_kernel(q_ref, k_ref, v_ref, o_ref, g_ref, gb_ref, nw_ref, y_ref, c_scr, n_scr, m_scr):
    L = MLSTM_CHUNK

    @pl.when(pl.program_id(1) == 0)
    def _():
        c_scr[...] = jnp.zeros_like(c_scr)
        n_scr[...] = jnp.zeros_like(n_scr)
        m_scr[...] = jnp.full_like(m_scr, -jnp.inf)

    gates = g_ref[...] + gb_ref[...]
    logf = _log_sigmoid(gates)
    row = lax.broadcasted_iota(jnp.int32, (L, L), 0)
    col = lax.broadcasted_iota(jnp.int32, (L, L), 1)
    causal = col <= row
    tri = causal.astype(F32)
    b_all = jnp.concatenate(
        [jnp.dot(tri, logf[c * L:(c + 1) * L], precision=lax.Precision.HIGHEST,
                 preferred_element_type=F32) for c in range(2)], axis=0)
    b_t = b_all.T
    g_t = gates.T
    scale = HEAD_DIM ** -0.5

    for h in range(MLSTM_HEADS):
        hs = slice(h * HEAD_DIM, (h + 1) * HEAD_DIM)
        k_all = k_ref[:, hs] * scale
        k_allt = k_all.T
        c_st = c_scr[h]
        n_st = n_scr[h]
        m_st = m_scr[h][:, 0:1]
        for c in range(2):
            rs = slice(c * L, (c + 1) * L)
            qc = q_ref[rs, hs]
            kc = k_all[rs]
            vc = v_ref[rs, hs]
            b_col = b_all[rs, 4 + h:5 + h]
            i_col = gates[rs, h:h + 1]
            b_row = b_t[4 + h:5 + h, rs]
            i_row = g_t[h:h + 1, rs]
            dmat = jnp.where(causal, b_col - b_row + i_row, -jnp.inf)
            inter = b_col + m_st
            m_t = jnp.maximum(inter, jnp.max(dmat, axis=-1, keepdims=True))
            qb = qc.astype(BF16)
            s = lax.dot_general(qb, kc.astype(BF16), (((1,), (1,)), ((), ())),
                                preferred_element_type=F32)
            s = s * jnp.exp(dmat - m_t)
            w_inter = jnp.exp(inter - m_t)
            num = (jnp.dot(s.astype(BF16), vc.astype(BF16), preferred_element_type=F32)
                   + w_inter * jnp.dot(qb, c_st.astype(BF16), preferred_element_type=F32))
            den = (jnp.sum(s, axis=-1, keepdims=True)
                   + w_inter * jnp.sum(qc * n_st, axis=-1, keepdims=True))
            hh = num / jnp.maximum(jnp.abs(den), jnp.exp(-m_t))
            b_last = b_col[L - 1:L, :]
            g = b_last - b_col + i_col
            m_new = jnp.maximum(b_last + m_st, jnp.max(g, axis=0, keepdims=True))
            w_k = jnp.exp(g - m_new)
            decay = jnp.exp(b_last + m_st - m_new)
            c_st = decay * c_st + jnp.dot(k_allt[:, rs].astype(BF16), (w_k * vc).astype(BF16),
                                          preferred_element_type=F32)
            n_st = decay * n_st + jnp.sum(w_k * kc, axis=0, keepdims=True)
            m_st = m_new
            mu = jnp.mean(hh, axis=-1, keepdims=True)
            hc = hh - mu
            hn = hc * lax.rsqrt(jnp.mean(hc * hc, axis=-1, keepdims=True) + EPS) * nw_ref[:, hs]
            y_ref[rs, hs] = (jax.nn.sigmoid(o_ref[rs, hs]) * hn).astype(y_ref.dtype)
        c_scr[h] = c_st
        n_scr[h] = n_st
        m_scr[h] = jnp.broadcast_to(m_st, (1, LANE))


def _mlstm(proj, gate_bias, norm_w, batch):
    m = proj.shape[0]
    nsteps = SEQ // MLSTM_ROWS
    blk = lambda cb: pl.BlockSpec((MLSTM_ROWS, MLSTM_WIDTH), lambda b, c, cb=cb: (b * nsteps + c, cb))
    return pl.pallas_call(
        _mlstm_kernel,
        grid=(batch, nsteps),
        in_specs=[
            blk(OFF_MQ // MLSTM_WIDTH), blk(OFF_MK // MLSTM_WIDTH),
            blk(OFF_MV // MLSTM_WIDTH), blk(OFF_MO // MLSTM_WIDTH),
            pl.BlockSpec((MLSTM_ROWS, LANE), lambda b, c: (b * nsteps + c, OFF_SMALL // LANE)),
            pl.BlockSpec((1, LANE), lambda b, c: (0, 0)),
            pl.BlockSpec((1, MLSTM_WIDTH), lambda b, c: (0, 0)),
        ],
        out_specs=pl.BlockSpec((MLSTM_ROWS, MLSTM_WIDTH), lambda b, c: (b * nsteps + c, 0)),
        out_shape=jax.ShapeDtypeStruct((m, MLSTM_WIDTH), BF16),
        scratch_shapes=[
            pltpu.VMEM((MLSTM_HEADS, HEAD_DIM, HEAD_DIM), F32),
            pltpu.VMEM((MLSTM_HEADS, 1, HEAD_DIM), F32),
            pltpu.VMEM((MLSTM_HEADS, 1, LANE), F32),
        ],
        compiler_params=_cparams("parallel", "arbitrary"),
        name="mlstm",
    )(proj, proj, proj, proj, proj, gate_bias, norm_w)


def _conv_kernel(a_ref, g_ref, ah_ref, gh_ref, w_ref, b_ref, lw_ref, lb_ref, y_ref, u_scr):
    first = pl.program_id(1) == 0
    uh = ah_ref[...] * jax.nn.sigmoid(gh_ref[...])
    u_scr[0:CONV_HALO, :] = jnp.where(first, 0.0, uh)
    u_scr[CONV_HALO:, :] = a_ref[...] * jax.nn.sigmoid(g_ref[...])
    base = CONV_HALO - (CONV_LEN - 1)
    acc = jnp.zeros((CONV_ROWS, CONV_WIDTH), F32)
    for j in range(CONV_LEN):
        acc = acc + w_ref[j:j + 1, :] * u_scr[base + j:base + j + CONV_ROWS, :]
    u = acc + b_ref[...]
    mu = jnp.mean(u, axis=-1, keepdims=True)
    uc = u - mu
    y = uc * lax.rsqrt(jnp.mean(uc * uc, axis=-1, keepdims=True) + EPS) * lw_ref[...] + lb_ref[...]
    y_ref[...] = (y * jax.nn.sigmoid(y)).astype(y_ref.dtype)


def _conformer(proj, dw_w, dw_b, ln_w, ln_b, batch):
    m = proj.shape[0]
    nt = SEQ // CONV_ROWS
    hb = CONV_ROWS // CONV_HALO
    main = lambda cb: pl.BlockSpec((CONV_ROWS, CONV_WIDTH), lambda b, i, cb=cb: (b * nt + i, cb))
    halo = lambda cb: pl.BlockSpec(
        (CONV_HALO, CONV_WIDTH), lambda b, i, cb=cb: (jnp.maximum((b * nt + i) * hb - 1, 0), cb))
    vec = pl.BlockSpec((1, CONV_WIDTH), lambda b, i: (0, 0))
    return pl.pallas_call(
        _conv_kernel,
        grid=(batch, nt),
        in_specs=[
            main(OFF_CA // CONV_WIDTH), main(OFF_CG // CONV_WIDTH),
            halo(OFF_CA // CONV_WIDTH), halo(OFF_CG // CONV_WIDTH),
            pl.BlockSpec((CONV_LEN, CONV_WIDTH), lambda b, i: (0, 0)),
            vec, vec, vec,
        ],
        out_specs=pl.BlockSpec((CONV_ROWS, CONV_WIDTH), lambda b, i: (b * nt + i, 0)),
        out_shape=jax.ShapeDtypeStruct((m, CONV_WIDTH), BF16),
        scratch_shapes=[pltpu.VMEM((CONV_HALO + CONV_ROWS, CONV_WIDTH), F32)],
        compiler_params=_cparams("parallel", "arbitrary"),
        name="conformer_conv",
    )(proj, proj, proj, proj, dw_w, dw_b, ln_w, ln_b)


def _rope(x, cs, sn):
    return x * cs + pltpu.roll(x, HEAD_DIM // 2, axis=1) * sn


def _compress(x_ref, pe_ref, w_ref):
    nrow = SEQ // CMP_STRIDE
    lo = jnp.zeros((nrow, HEAD_DIM), F32)
    hi = jnp.zeros((nrow, HEAD_DIM), F32)
    for l in range(CMP_STRIDE):
        xl = x_ref[pl.ds(l, nrow, stride=CMP_STRIDE), :]
        lo = lo + jnp.dot((xl + pe_ref[l:l + 1, :]).astype(BF16), w_ref[l],
                          preferred_element_type=F32)
        hi = hi + jnp.dot((xl + pe_ref[CMP_STRIDE + l:CMP_STRIDE + l + 1, :]).astype(BF16),
                          w_ref[CMP_STRIDE + l], preferred_element_type=F32)
    return lo + pltpu.roll(hi, nrow - 1, axis=0)


def _nsa_prep_kernel(kc_ref, vc_ref, ks_ref, vs_ref, kw_ref, vw_ref, pek_ref, wk_ref, pev_ref, wv_ref,
                     cs_ref, sn_ref, ccs_ref, csn_ref,
                     kco_ref, vco_ref, kso_ref, vso_ref, kwo_ref, vwo_ref):
    kco_ref[...] = _rope(_compress(kc_ref, pek_ref, wk_ref), ccs_ref[...], csn_ref[...]).astype(BF16)
    vco_ref[...] = _compress(vc_ref, pev_ref, wv_ref).T.astype(BF16)
    cs = cs_ref[...]
    sn = sn_ref[...]
    kso_ref[:, 0:HEAD_DIM] = _rope(ks_ref[...], cs, sn).astype(BF16)
    key = lax.broadcasted_iota(jnp.int32, (SEQ, LANE), 0)
    blk = lax.broadcasted_iota(jnp.int32, (SEQ, LANE), 1)
    kso_ref[:, HEAD_DIM:] = jnp.where(key // SEL_BLOCK == blk, 1.0, 0.0).astype(BF16)
    kwo_ref[...] = _rope(kw_ref[...], cs, sn).astype(BF16)
    for kt in range(SEQ // TK):
        vso_ref[kt] = vs_ref[kt * TK:(kt + 1) * TK, :].T.astype(BF16)
        vwo_ref[kt] = vw_ref[kt * TK:(kt + 1) * TK, :].T.astype(BF16)


def _nsa_prep(proj, pe_k, w_k, pe_v, w_v, tabs, batch):
    cs, sn, ccs, csn = tabs
    G = NSA_KV_HEADS
    ncr = SEQ // CMP_STRIDE
    col = lambda off: pl.BlockSpec((SEQ, HEAD_DIM), lambda b, g, off=off: (b, off // HEAD_DIM + g))
    full2 = lambda r: pl.BlockSpec((r, HEAD_DIM), lambda b, g: (0, 0))
    wspec = pl.BlockSpec((CMP_LEN, HEAD_DIM, HEAD_DIM), lambda b, g: (0, 0, 0))
    ocmp = pl.BlockSpec((None, None, ncr, HEAD_DIM), lambda b, g: (b, g, 0, 0))
    oseq = pl.BlockSpec((None, None, SEQ, HEAD_DIM), lambda b, g: (b, g, 0, 0))
    oext = pl.BlockSpec((None, None, SEQ, HEAD_DIM + LANE), lambda b, g: (b, g, 0, 0))
    cmp_shape = jax.ShapeDtypeStruct((batch, G, ncr, HEAD_DIM), BF16)
    seq_shape = jax.ShapeDtypeStruct((batch, G, SEQ, HEAD_DIM), BF16)
    ext_shape = jax.ShapeDtypeStruct((batch, G, SEQ, HEAD_DIM + LANE), BF16)
    oval = pl.BlockSpec((None, None, SEQ // TK, HEAD_DIM, TK), lambda b, g: (b, g, 0, 0, 0))
    val_shape = jax.ShapeDtypeStruct((batch, G, SEQ // TK, HEAD_DIM, TK), BF16)
    return pl.pallas_call(
        _nsa_prep_kernel,
        grid=(batch, G),
        in_specs=[
            col(OFF_NKC), col(OFF_NVC), col(OFF_NKS), col(OFF_NVS), col(OFF_NKW), col(OFF_NVW),
            full2(CMP_LEN), wspec, full2(CMP_LEN), wspec,
            full2(SEQ), full2(SEQ), full2(ncr), full2(ncr),
        ],
        out_specs=[ocmp, ocmp, oext, oval, oseq, oval],
        out_shape=[cmp_shape, cmp_shape, ext_shape, val_shape, seq_shape, val_shape],
        compiler_params=_cparams("parallel", "arbitrary"),
        name="nsa_prep",
    )(proj, proj, proj, proj, proj, proj, pe_k, w_k, pe_v, w_v, cs, sn, ccs, csn)


def _attend(k, v, q, bias, carry):
    m, l, acc = carry
    s = jnp.dot(k, q, preferred_element_type=F32)
    if bias is not None:
        s = s + bias
    m_new = jnp.maximum(m, jnp.max(s, axis=0, keepdims=True))
    alpha = jnp.exp2(m - m_new)
    p = jnp.exp2(s - m_new)
    l = alpha * l + jnp.sum(p, axis=0, keepdims=True)
    acc = alpha * acc + jnp.dot(v, p.astype(BF16), preferred_element_type=F32)
    return m_new, l, acc


def _nsa_kernel(q_ref, sm_ref, cs_ref, sn_ref, kc_ref, vc_ref, ks_ref, vs_ref, kw_ref, vw_ref, y_ref):
    g = pl.program_id(1)
    qi = pl.program_id(2)
    q0 = qi * TQ
    R = NSA_HPG * TQ
    scale = HEAD_DIM ** -0.5 * LOG2E
    cs = cs_ref[...]
    sn = sn_ref[...]
    q = jnp.concatenate(
        [(_rope(q_ref[:, h * HEAD_DIM:(h + 1) * HEAD_DIM], cs, sn) * scale).T.astype(BF16)
         for h in range(NSA_HPG)], axis=1)
    tok = lax.broadcasted_iota(jnp.int32, (LANE, R), 1) & (TQ - 1)
    rowi = lax.broadcasted_iota(jnp.int32, (LANE, R), 0)

    ncr = SEQ // CMP_STRIDE
    valid_c = (rowi * CMP_STRIDE + (CMP_LEN - 1)) <= q0 + tok
    s_c = jnp.dot(kc_ref[...], q, preferred_element_type=F32) + jnp.where(valid_c, 0.0, NEG)
    m_c = jnp.max(s_c, axis=0, keepdims=True)
    e_c = jnp.where(valid_c, jnp.exp2(s_c - m_c), 0.0)
    p_c = e_c * (1.0 / jnp.maximum(jnp.sum(e_c, axis=0, keepdims=True), 1e-30))
    o_cmp = jnp.dot(vc_ref[...], p_c.astype(BF16), preferred_element_type=F32)
    p_sum = p_c[:, 0:TQ] + p_c[:, TQ:2 * TQ] + p_c[:, 2 * TQ:3 * TQ] + p_c[:, 3 * TQ:4 * TQ]

    jj = lax.broadcasted_iota(jnp.int32, (LANE, ncr), 0)
    nn = lax.broadcasted_iota(jnp.int32, (LANE, ncr), 1)
    ov_t = ((nn * CMP_STRIDE <= jj * SEL_BLOCK + (SEL_BLOCK - 1))
            & (nn * CMP_STRIDE + (CMP_LEN - 1) >= jj * SEL_BLOCK)
            & (jj < N_SEL) & (nn < N_CMP)).astype(F32)
    imp_t = jnp.dot(ov_t, p_sum, precision=lax.Precision.HIGHEST,
                    preferred_element_type=F32)[0:N_SEL]
    j_i = lax.broadcasted_iota(jnp.int32, (N_SEL, TQ), 0)
    t_i = q0 + lax.broadcasted_iota(jnp.int32, (N_SEL, TQ), 1)
    qblk = t_i // SEL_BLOCK
    forced = (j_i == 0) | (j_i == qblk) | (j_i == qblk - 1)
    score = jnp.where(forced, FORCE, jnp.where(j_i * SEL_BLOCK <= t_i, imp_t, -FORCE))
    rank = jnp.zeros((N_SEL, TQ), jnp.int32)
    for jp in range(N_SEL):
        row = score[jp:jp + 1, :]
        beats = (row > score) | ((row == score) & (j_i > jp))
        rank = rank + beats.astype(jnp.int32)
    sel_bias_t = jnp.concatenate(
        [jnp.where(rank < SEL_COUNT, 0.0, NEG), jnp.zeros((LANE - N_SEL, TQ), F32)], axis=0)
    qx = jnp.concatenate([q, jnp.concatenate([sel_bias_t.astype(BF16)] * NSA_HPG, axis=1)], axis=0)

    keyi = lax.broadcasted_iota(jnp.int32, (TK, R), 0)
    toki = lax.broadcasted_iota(jnp.int32, (TK, R), 1) & (TQ - 1)
    causal = jnp.where(keyi <= toki, 0.0, NEG)

    init = (jnp.full((1, R), NEG, F32), jnp.zeros((1, R), F32), jnp.zeros((HEAD_DIM, R), F32))
    diag = pl.multiple_of(q0, TK)

    def sel_step(kt, carry):
        k0 = pl.multiple_of(kt * TK, TK)
        return _attend(ks_ref[pl.ds(k0, TK), :], vs_ref[kt], qx, None, carry)

    carry = lax.fori_loop(0, qi, sel_step, init)
    _, l_s, a_s = _attend(ks_ref[pl.ds(diag, TK), :], vs_ref[qi], qx, causal, carry)

    carry = _attend(kw_ref[pl.ds(diag, TK), :], vw_ref[qi], q, causal, init)
    t1 = jnp.maximum(qi - 1, 0)
    carry = _attend(kw_ref[pl.ds(pl.multiple_of(t1 * TK, TK), TK), :], vw_ref[t1], q,
                    jnp.where(qi >= 1, jnp.zeros((TK, R), F32), NEG), carry)
    t2 = jnp.maximum(qi - 2, 0)
    _, l_w, a_w = _attend(kw_ref[pl.ds(pl.multiple_of(t2 * TK, TK), TK), :], vw_ref[t2], q,
                          jnp.where(qi >= 2, jnp.where(keyi > toki, 0.0, NEG), NEG), carry)

    gl = jax.nn.sigmoid(sm_ref[...]).T
    o_slc = a_s * (1.0 / l_s)
    o_win = a_w * (1.0 / l_w)
    for h in range(NSA_HPG):
        cs_ = slice(h * TQ, (h + 1) * TQ)
        gate = []
        for c in range(3):
            r0 = GATE_LANE0 + h * 3 + c
            r1 = r0 + NSA_HPG * 3
            gate.append(jnp.where(g == 0, gl[r0:r0 + 1, :], gl[r1:r1 + 1, :]))
        y = gate[0] * o_cmp[:, cs_] + gate[1] * o_slc[:, cs_] + gate[2] * o_win[:, cs_]
        y_ref[:, h * HEAD_DIM:(h + 1) * HEAD_DIM] = y.T.astype(y_ref.dtype)


def _nsa(proj, prep, tabs, batch):
    kc, vc, ks, vs, kw, vw = prep
    cs, sn = tabs[0], tabs[1]
    m = proj.shape[0]
    G = NSA_KV_HEADS
    nq = SEQ // TQ
    ncr = SEQ // CMP_STRIDE
    gw = NSA_HPG * HEAD_DIM
    cmp_spec = pl.BlockSpec((None, None, ncr, HEAD_DIM), lambda b, g, i: (b, g, 0, 0))
    seq_spec = pl.BlockSpec((None, None, SEQ, HEAD_DIM), lambda b, g, i: (b, g, 0, 0))
    ext_spec = pl.BlockSpec((None, None, SEQ, HEAD_DIM + LANE), lambda b, g, i: (b, g, 0, 0))
    val_spec = pl.BlockSpec((None, None, SEQ // TK, HEAD_DIM, TK), lambda b, g, i: (b, g, 0, 0, 0))
    tab_spec = pl.BlockSpec((TQ, HEAD_DIM), lambda b, g, i: (i, 0))
    return pl.pallas_call(
        _nsa_kernel,
        grid=(batch, G, nq),
        in_specs=[
            pl.BlockSpec((TQ, gw), lambda b, g, i: (b * nq + i, OFF_NQ // gw + g)),
            pl.BlockSpec((TQ, LANE), lambda b, g, i: (b * nq + i, OFF_SMALL // LANE)),
            tab_spec, tab_spec, cmp_spec, cmp_spec, ext_spec, val_spec, seq_spec, val_spec,
        ],
        out_specs=pl.BlockSpec((TQ, gw), lambda b, g, i: (b * nq + i, g)),
        out_shape=jax.ShapeDtypeStruct((m, NSA_WIDTH), BF16),
        compiler_params=_cparams("parallel", "parallel", "arbitrary"),
        name="nsa_attn",
    )(proj, proj, cs, sn, kc, vc, ks, vs, kw, vw)


def _out_proj_kernel(ya_ref, yb_ref, yc_ref, wa_ref, wb_ref, wc_ref, x_ref, o_ref):
    acc = jnp.dot(ya_ref[...], wa_ref[...], preferred_element_type=F32)
    acc = acc + jnp.dot(yb_ref[...], wb_ref[...], preferred_element_type=F32)
    acc = acc + jnp.dot(yc_ref[...], wc_ref[...], preferred_element_type=F32)
    o_ref[...] = x_ref[...] + acc


def _out_proj(ya, yb, yc, w, x):
    m, d = x.shape
    wa, wb, wc = ya.shape[1], yb.shape[1], yc.shape[1]
    return pl.pallas_call(
        _out_proj_kernel,
        grid=(m // TM_OUT, d // TN_OUT),
        in_specs=[
            pl.BlockSpec((TM_OUT, wa), lambda i, j: (i, 0)),
            pl.BlockSpec((TM_OUT, wb), lambda i, j: (i, 0)),
            pl.BlockSpec((TM_OUT, wc), lambda i, j: (i, 0)),
            pl.BlockSpec((wa, TN_OUT), lambda i, j: (0, j)),
            pl.BlockSpec((wb, TN_OUT), lambda i, j: (wa // wb, j)),
            pl.BlockSpec((wc, TN_OUT), lambda i, j: ((wa + wb) // wc, j)),
            pl.BlockSpec((TM_OUT, TN_OUT), lambda i, j: (i, j)),
        ],
        out_specs=pl.BlockSpec((TM_OUT, TN_OUT), lambda i, j: (i, j)),
        out_shape=jax.ShapeDtypeStruct((m, d), F32),
        compiler_params=_cparams("parallel", "arbitrary"),
        name="out_proj",
    )(ya, yb, yc, w, w, w, x)


def _ffn_up_kernel(x_ref, xh_ref, nw_ref, wg_ref, wv_ref, cg_ref, cv_ref, o_ref, h_scr, hh_scr, g_scr, v_scr):
    i = pl.program_id(0)

    @pl.when(pl.program_id(1) == 0)
    def _():
        h_scr[...] = _rms(x_ref[...], nw_ref[...]).astype(BF16)
        hh_scr[...] = _rms(xh_ref[...], nw_ref[...]).astype(BF16)

    keep = jnp.where((i * TM_UP) % SEQ == 0, 0.0, 1.0)

    def branch(w_ref, c_ref, scr):
        cur = jnp.dot(h_scr[...], w_ref[...], preferred_element_type=F32)
        scr[0:FFN_HALO, :] = jnp.dot(hh_scr[...], w_ref[...], preferred_element_type=F32) * keep
        scr[FFN_HALO:, :] = cur
        out = c_ref[FFN_CONV_LEN - 1:FFN_CONV_LEN, :] * cur
        for j in range(FFN_CONV_LEN - 1):
            sh = FFN_HALO - (FFN_CONV_LEN - 1) + j
            out = out + c_ref[j:j + 1, :] * scr[sh:sh + TM_UP, :]
        return out

    gate = branch(wg_ref, cg_ref, g_scr)
    val = branch(wv_ref, cv_ref, v_scr)
    o_ref[...] = (gate * jax.nn.sigmoid(gate) * val).astype(o_ref.dtype)


def _ffn_up(x, nw, w_up, dw):
    m, d = x.shape
    nj = D_FF // TN_UP
    hb = TM_UP // FFN_HALO
    return pl.pallas_call(
        _ffn_up_kernel,
        grid=(m // TM_UP, nj),
        in_specs=[
            pl.BlockSpec((TM_UP, d), lambda i, j: (i, 0)),
            pl.BlockSpec((FFN_HALO, d), lambda i, j: (jnp.maximum(i * hb - 1, 0), 0)),
            pl.BlockSpec((1, d), lambda i, j: (0, 0)),
            pl.BlockSpec((d, TN_UP), lambda i, j: (0, j)),
            pl.BlockSpec((d, TN_UP), lambda i, j: (0, j + nj)),
            pl.BlockSpec((FFN_CONV_LEN, TN_UP), lambda i, j: (0, j)),
            pl.BlockSpec((FFN_CONV_LEN, TN_UP), lambda i, j: (0, j + nj)),
        ],
        out_specs=pl.BlockSpec((TM_UP, TN_UP), lambda i, j: (i, j)),
        out_shape=jax.ShapeDtypeStruct((m, D_FF), BF16),
        scratch_shapes=[
            pltpu.VMEM((TM_UP, d), BF16),
            pltpu.VMEM((FFN_HALO, d), BF16),
            pltpu.VMEM((FFN_HALO + TM_UP, TN_UP), F32),
            pltpu.VMEM((FFN_HALO + TM_UP, TN_UP), F32),
        ],
        compiler_params=_cparams("parallel", "arbitrary"),
        name="ffn_up",
    )(x, x, nw, w_up, w_up, dw, dw)


def _ffn_down_kernel(a_ref, w_ref, x_ref, o_ref):
    o_ref[...] = x_ref[...] + jnp.dot(a_ref[...], w_ref[...], preferred_element_type=F32)


def _ffn_down(a, w, x):
    m, d = x.shape
    f = a.shape[1]
    return pl.pallas_call(
        _ffn_down_kernel,
        grid=(m // TM_DN, d // TN_DN),
        in_specs=[
            pl.BlockSpec((TM_DN, f), lambda i, j: (i, 0)),
            pl.BlockSpec((f, TN_DN), lambda i, j: (0, j)),
            pl.BlockSpec((TM_DN, TN_DN), lambda i, j: (i, j)),
        ],
        out_specs=pl.BlockSpec((TM_DN, TN_DN), lambda i, j: (i, j)),
        out_shape=jax.ShapeDtypeStruct((m, d), F32),
        compiler_params=_cparams("parallel", "arbitrary"),
        name="ffn_down",
    )(a, w, x)


def _final_norm_kernel(x_ref, w_ref, o_ref):
    o_ref[...] = _rms(x_ref[...], w_ref[...])


def _final_norm(x, w):
    m, d = x.shape
    return pl.pallas_call(
        _final_norm_kernel,
        grid=(m // TM_NORM,),
        in_specs=[pl.BlockSpec((TM_NORM, d), lambda i: (i, 0)), pl.BlockSpec((1, d), lambda i: (0, 0))],
        out_specs=pl.BlockSpec((TM_NORM, d), lambda i: (i, 0)),
        out_shape=jax.ShapeDtypeStruct((m, d), F32),
        compiler_params=_cparams("parallel"),
        name="final_norm",
    )(x, w)


def _rope_tables():
    half = HEAD_DIM // 2
    inv = ROPE_THETA ** (-np.arange(half, dtype=np.float64) / half)

    def tab(pos):
        ang = np.asarray(pos, np.float64)[:, None] * inv[None, :]
        cos, sin = np.cos(ang), np.sin(ang)
        return (jnp.asarray(np.concatenate([cos, cos], -1), F32),
                jnp.asarray(np.concatenate([-sin, sin], -1), F32))

    cs, sn = tab(np.arange(SEQ))
    ccs, csn = tab(np.arange(SEQ // CMP_STRIDE) * CMP_STRIDE + CMP_LEN - 1)
    return cs, sn, ccs, csn


def _reorder_w_in(w):
    d = w.shape[0]
    a0 = 4 * MLSTM_WIDTH
    a1 = a0 + 2 * MLSTM_HEADS
    a2 = a1 + (OFF_SMALL - OFF_CA)
    a3 = a2 + 3 * NSA_HEADS
    pad = jnp.zeros((d, N_PROJ - a3), w.dtype)
    return jnp.concatenate([w[:, :a0], w[:, a1:a2], w[:, a0:a1], w[:, a2:a3], pad], axis=1).astype(BF16)


def kernel(x, attn_norm_w, w_in, mlstm_i_bias, mlstm_f_bias, mlstm_norm_w, conv_dw_w, conv_dw_b, conv_ln_w, conv_ln_b, nsa_cmp_pe_k, nsa_cmp_w_k, nsa_cmp_pe_v, nsa_cmp_w_v, w_out, ffn_norm_w, w_up, ffn_dw_w, w_down, final_norm_w):
    batch, seq, d = x.shape
    assert seq == SEQ and d == D_MODEL
    depth = w_in.shape[0]
    tabs = _rope_tables()
    xf = x.reshape(batch * seq, d)
    for l in range(depth):
        proj = _norm_matmul(xf, attn_norm_w[l][None, :], _reorder_w_in(w_in[l]))
        gate_bias = jnp.concatenate(
            [mlstm_i_bias[l], mlstm_f_bias[l], jnp.zeros((LANE - 2 * MLSTM_HEADS,), F32)])[None, :]
        y_a = _mlstm(proj, gate_bias, mlstm_norm_w[l][None, :], batch)
        y_b = _conformer(proj, conv_dw_w[l], conv_dw_b[l][None, :], conv_ln_w[l][None, :],
                         conv_ln_b[l][None, :], batch)
        prep = _nsa_prep(proj, nsa_cmp_pe_k[l],
                         nsa_cmp_w_k[l].reshape(CMP_LEN, HEAD_DIM, HEAD_DIM).astype(BF16),
                         nsa_cmp_pe_v[l],
                         nsa_cmp_w_v[l].reshape(CMP_LEN, HEAD_DIM, HEAD_DIM).astype(BF16), tabs, batch)
        y_c = _nsa(proj, prep, tabs, batch)
        xf = _out_proj(y_a, y_b, y_c, w_out[l].astype(BF16), xf)
        act = _ffn_up(xf, ffn_norm_w[l][None, :], w_up[l].astype(BF16), ffn_dw_w[l])
        xf = _ffn_down(act, w_down[l].astype(BF16), xf)
    return _final_norm(xf, final_norm_w[None, :]).reshape(batch, seq, d)
```

```python
import functools

import numpy as np
import jax
import jax.numpy as jnp
from jax import lax
from jax.experimental import pallas as pl
from jax.experimental.pallas import tpu as pltpu

F32 = jnp.float32
BF16 = jnp.bfloat16

D_MODEL = 2048
SEQ = 2048
HEAD_DIM = 128
MLSTM_HEADS = 4
MLSTM_WIDTH = MLSTM_HEADS * HEAD_DIM
MLSTM_CHUNK = 64
CONV_WIDTH = 512
CONV_LEN = 31
NSA_HEADS = 8
NSA_KV_HEADS = 2
NSA_HPG = NSA_HEADS // NSA_KV_HEADS
NSA_WIDTH = NSA_HEADS * HEAD_DIM
CMP_LEN = 32
CMP_STRIDE = 16
N_CMP = (SEQ - CMP_LEN) // CMP_STRIDE + 1
SEL_BLOCK = 64
SEL_COUNT = 16
N_SEL = SEQ // SEL_BLOCK
WINDOW = 512
ROPE_THETA = 10000.0
D_FF = 5632
FFN_CONV_LEN = 3
EPS = 1e-6
NEG = -1e30
FORCE = 1e4

OFF_MQ, OFF_MK, OFF_MV, OFF_MO = 0, 512, 1024, 1536
OFF_CA, OFF_CG = 2048, 2560
OFF_NQ = 3072
OFF_NKC, OFF_NVC, OFF_NKS, OFF_NVS, OFF_NKW, OFF_NVW = 4096, 4352, 4608, 4864, 5120, 5376
OFF_SMALL = 5632
N_PROJ = 5760
LANE = 128
GATE_LANE0 = 8

VMEM_LIMIT = 52 * 1024 * 1024

TM_IN, TN_IN = 1024, 1152
TM_OUT, TN_OUT = 1024, 1024
TM_UP, TN_UP = 1024, 512
TM_DN, TN_DN = 512, 512
TM_NORM = 512
MLSTM_ROWS = 2 * MLSTM_CHUNK
CONV_ROWS = 256
CONV_HALO = 32
SUBLANE = 8
CONV_PHASE_ROWS = CONV_HALO + CONV_ROWS - SUBLANE
CONV_CHUNK = 32
FFN_HALO = 16
FFN_EPI_ROWS = 32
FFN_SEG_N = 256
TQ = 256
TK = 256
assert TQ == TK and WINDOW == 2 * TK
LOG2E = 1.4426950408889634


def _cparams(*sem):
    return pltpu.CompilerParams(dimension_semantics=sem, vmem_limit_bytes=VMEM_LIMIT)


def _rms(x, w):
    return x * lax.rsqrt(jnp.mean(x * x, axis=-1, keepdims=True) + EPS) * w


def _norm_matmul_kernel(x_ref, nw_ref, w_ref, o_ref, h_scr):
    @pl.when(pl.program_id(1) == 0)
    def _():
        h_scr[...] = _rms(x_ref[...], nw_ref[...]).astype(BF16)

    o_ref[...] = jnp.dot(h_scr[...], w_ref[...], preferred_element_type=F32)


def _norm_matmul(x, nw, w, layer):
    m, d = x.shape
    n = w.shape[2]
    return pl.pallas_call(
        _norm_matmul_kernel,
        grid=(m // TM_IN, n // TN_IN),
        in_specs=[
            pl.BlockSpec((TM_IN, d), lambda i, j: (i, 0)),
            pl.BlockSpec((1, d), lambda i, j: (0, 0)),
            pl.BlockSpec((None, d, TN_IN), lambda i, j: (layer, 0, j)),
        ],
        out_specs=pl.BlockSpec((TM_IN, TN_IN), lambda i, j: (i, j)),
        out_shape=jax.ShapeDtypeStruct((m, n), F32),
        scratch_shapes=[pltpu.VMEM((TM_IN, d), BF16)],
        compiler_params=_cparams("parallel", "arbitrary"),
        name="in_proj",
    )(x, nw, w)


def _log_sigmoid(x):
    return jnp.minimum(x, 0.0) - jnp.log1p(jnp.exp(-jnp.abs(x)))


def _mlstm_kernel(q_ref, k_ref, v_ref, o_ref, g_ref, gb_ref, nw_ref, y_ref, c_scr, n_scr, m_scr):
    L = MLSTM_CHUNK

    @pl.when(pl.program_id(1) == 0)
    def _():
        c_scr[...] = jnp.zeros_like(c_scr)
        n_scr[...] = jnp.zeros_like(n_scr)
        m_scr[...] = jnp.full_like(m_scr, -jnp.inf)

    gates = g_ref[...] + gb_ref[...]
    logf = _log_sigmoid(gates)
    row = lax.broadcasted_iota(jnp.int32, (L, L), 0)
    col = lax.broadcasted_iota(jnp.int32, (L, L), 1)
    causal = col <= row
    tri = causal.astype(F32)
    b_all = jnp.concatenate(
        [jnp.dot(tri, logf[c * L:(c + 1) * L], precision=lax.Precision.HIGHEST,
                 preferred_element_type=F32) for c in range(2)], axis=0)
    b_t = b_all.T
    g_t = gates.T
    scale = HEAD_DIM ** -0.5

    for h in range(MLSTM_HEADS):
        hs = slice(h * HEAD_DIM, (h + 1) * HEAD_DIM)
        k_all = k_ref[:, hs] * scale
        k_allt = k_all.T
        c_st = c_scr[h]
        n_st = n_scr[h]
        m_st = m_scr[h][:, 0:1]
        for c in range(2):
            rs = slice(c * L, (c + 1) * L)
            qc = q_ref[rs, hs]
            kc = k_all[rs]
            vc = v_ref[rs, hs]
            b_col = b_all[rs, 4 + h:5 + h]
            i_col = gates[rs, h:h + 1]
            b_row = b_t[4 + h:5 + h, rs]
            i_row = g_t[h:h + 1, rs]
            dmat = jnp.where(causal, b_col - b_row + i_row, -jnp.inf)
            inter = b_col + m_st
            m_t = jnp.maximum(inter, jnp.max(dmat, axis=-1, keepdims=True))
            qb = qc.astype(BF16)
            s = lax.dot_general(qb, kc.astype(BF16), (((1,), (1,)), ((), ())),
                                preferred_element_type=F32)
            s = s * jnp.exp(dmat - m_t)
            w_inter = jnp.exp(inter - m_t)
            num = (jnp.dot(s.astype(BF16), vc.astype(BF16), preferred_element_type=F32)
                   + w_inter * jnp.dot(qb, c_st.astype(BF16), preferred_element_type=F32))
            den = (jnp.sum(s, axis=-1, keepdims=True)
                   + w_inter * jnp.sum(qc * n_st, axis=-1, keepdims=True))
            hh = num / jnp.maximum(jnp.abs(den), jnp.exp(-m_t))
            b_last = b_col[L - 1:L, :]
            g = b_last - b_col + i_col
            m_new = jnp.maximum(b_last + m_st, jnp.max(g, axis=0, keepdims=True))
            w_k = jnp.exp(g - m_new)
            decay = jnp.exp(b_last + m_st - m_new)
            c_st = decay * c_st + jnp.dot(k_allt[:, rs].astype(BF16), (w_k * vc).astype(BF16),
                                          preferred_element_type=F32)
            n_st = decay * n_st + jnp.sum(w_k * kc, axis=0, keepdims=True)
            m_st = m_new
            mu = jnp.mean(hh, axis=-1, keepdims=True)
            hc = hh - mu
            hn = hc * lax.rsqrt(jnp.mean(hc * hc, axis=-1, keepdims=True) + EPS) * nw_ref[:, hs]
            y_ref[rs, hs] = (jax.nn.sigmoid(o_ref[rs, hs]) * hn).astype(y_ref.dtype)
        c_scr[h] = c_st
        n_scr[h] = n_st
        m_scr[h] = jnp.broadcast_to(m_st, (1, LANE))


def _mlstm(proj, gate_bias, norm_w, batch):
    m = proj.shape[0]
    nsteps = SEQ // MLSTM_ROWS
    blk = lambda cb: pl.BlockSpec((MLSTM_ROWS, MLSTM_WIDTH), lambda b, c, cb=cb: (b * nsteps + c, cb))
    return pl.pallas_call(
        _mlstm_kernel,
        grid=(batch, nsteps),
        in_specs=[
            blk(OFF_MQ // MLSTM_WIDTH), blk(OFF_MK // MLSTM_WIDTH),
            blk(OFF_MV // MLSTM_WIDTH), blk(OFF_MO // MLSTM_WIDTH),
            pl.BlockSpec((MLSTM_ROWS, LANE), lambda b, c: (b * nsteps + c, OFF_SMALL // LANE)),
            pl.BlockSpec((1, LANE), lambda b, c: (0, 0)),
            pl.BlockSpec((1, MLSTM_WIDTH), lambda b, c: (0, 0)),
        ],
        out_specs=pl.BlockSpec((MLSTM_ROWS, MLSTM_WIDTH), lambda b, c: (b * nsteps + c, 0)),
        out_shape=jax.ShapeDtypeStruct((m, MLSTM_WIDTH), BF16),
        scratch_shapes=[
            pltpu.VMEM((MLSTM_HEADS, HEAD_DIM, HEAD_DIM), F32),
            pltpu.VMEM((MLSTM_HEADS, 1, HEAD_DIM), F32),
            pltpu.VMEM((MLSTM_HEADS, 1, LANE), F32),
        ],
        compiler_params=_cparams("parallel", "arbitrary"),
        name="mlstm",
    )(proj, proj, proj, proj, proj, gate_bias, norm_w)


def _conv_kernel(a_ref, g_ref, ah_ref, gh_ref, w_ref, b_ref, lw_ref, lb_ref, y_ref, u_scr, s_scr):
    first = pl.program_id(1) == 0
    uh = ah_ref[...] * jax.nn.sigmoid(gh_ref[...])
    u_scr[0:CONV_HALO, :] = jnp.where(first, 0.0, uh)
    u_scr[CONV_HALO:, :] = a_ref[...] * jax.nn.sigmoid(g_ref[...])
    for ph in range(1, SUBLANE):
        s_scr[ph] = u_scr[ph:ph + CONV_PHASE_ROWS, :]
    w = w_ref[...]
    bias = b_ref[...]
    lw = lw_ref[...]
    lb = lb_ref[...]
    base = CONV_HALO - (CONV_LEN - 1)
    for r in range(0, CONV_ROWS, CONV_CHUNK):
        acc = jnp.zeros((CONV_CHUNK, CONV_WIDTH), F32)
        for j in range(CONV_LEN):
            ph, al = (base + j) % SUBLANE, (base + j) // SUBLANE * SUBLANE
            src = u_scr[al + r:al + r + CONV_CHUNK, :] if ph == 0 else s_scr[ph, al + r:al + r + CONV_CHUNK, :]
            acc = acc + w[j:j + 1, :] * src
        u = acc + bias
        mu = jnp.mean(u, axis=-1, keepdims=True)
        uc = u - mu
        y = uc * lax.rsqrt(jnp.mean(uc * uc, axis=-1, keepdims=True) + EPS) * lw + lb
        y_ref[r:r + CONV_CHUNK, :] = (y * jax.nn.sigmoid(y)).astype(y_ref.dtype)


def _conformer(proj, dw_w, dw_b, ln_w, ln_b, batch):
    m = proj.shape[0]
    nt = SEQ // CONV_ROWS
    hb = CONV_ROWS // CONV_HALO
    main = lambda cb: pl.BlockSpec((CONV_ROWS, CONV_WIDTH), lambda b, i, cb=cb: (b * nt + i, cb))
    halo = lambda cb: pl.BlockSpec(
        (CONV_HALO, CONV_WIDTH), lambda b, i, cb=cb: (jnp.maximum((b * nt + i) * hb - 1, 0), cb))
    vec = pl.BlockSpec((1, CONV_WIDTH), lambda b, i: (0, 0))
    return pl.pallas_call(
        _conv_kernel,
        grid=(batch, nt),
        in_specs=[
            main(OFF_CA // CONV_WIDTH), main(OFF_CG // CONV_WIDTH),
            halo(OFF_CA // CONV_WIDTH), halo(OFF_CG // CONV_WIDTH),
            pl.BlockSpec((CONV_LEN, CONV_WIDTH), lambda b, i: (0, 0)),
            vec, vec, vec,
        ],
        out_specs=pl.BlockSpec((CONV_ROWS, CONV_WIDTH), lambda b, i: (b * nt + i, 0)),
        out_shape=jax.ShapeDtypeStruct((m, CONV_WIDTH), BF16),
        scratch_shapes=[pltpu.VMEM((CONV_HALO + CONV_ROWS, CONV_WIDTH), F32),
                        pltpu.VMEM((SUBLANE, CONV_PHASE_ROWS, CONV_WIDTH), F32)],
        compiler_params=_cparams("parallel", "arbitrary"),
        name="conformer_conv",
    )(proj, proj, proj, proj, dw_w, dw_b, ln_w, ln_b)


def _rope(x, cs, sn):
    return x * cs + pltpu.roll(x, HEAD_DIM // 2, axis=1) * sn


def _compress(x_ref, pe_ref, w_ref):
    nrow = SEQ // CMP_STRIDE
    lo = jnp.zeros((nrow, HEAD_DIM), F32)
    hi = jnp.zeros((nrow, HEAD_DIM), F32)
    for l in range(CMP_STRIDE):
        xl = x_ref[pl.ds(l, nrow, stride=CMP_STRIDE), :]
        lo = lo + jnp.dot((xl + pe_ref[l:l + 1, :]).astype(BF16), w_ref[l],
                          preferred_element_type=F32)
        hi = hi + jnp.dot((xl + pe_ref[CMP_STRIDE + l:CMP_STRIDE + l + 1, :]).astype(BF16),
                          w_ref[CMP_STRIDE + l], preferred_element_type=F32)
    return lo + pltpu.roll(hi, nrow - 1, axis=0)


def _nsa_prep_kernel(kc_ref, vc_ref, ks_ref, vs_ref, kw_ref, vw_ref, pek_ref, wk_ref, pev_ref, wv_ref,
                     cs_ref, sn_ref, ccs_ref, csn_ref,
                     kco_ref, vco_ref, kso_ref, vso_ref, kwo_ref, vwo_ref):
    kco_ref[...] = _rope(_compress(kc_ref, pek_ref, wk_ref), ccs_ref[...], csn_ref[...]).astype(BF16)
    vco_ref[...] = _compress(vc_ref, pev_ref, wv_ref).T.astype(BF16)
    cs = cs_ref[...]
    sn = sn_ref[...]
    kso_ref[:, 0:HEAD_DIM] = _rope(ks_ref[...], cs, sn).astype(BF16)
    key = lax.broadcasted_iota(jnp.int32, (SEQ, LANE), 0)
    blk = lax.broadcasted_iota(jnp.int32, (SEQ, LANE), 1)
    kso_ref[:, HEAD_DIM:] = jnp.where(key // SEL_BLOCK == blk, 1.0, 0.0).astype(BF16)
    kwo_ref[...] = _rope(kw_ref[...], cs, sn).astype(BF16)
    for kt in range(SEQ // TK):
        vso_ref[kt] = vs_ref[kt * TK:(kt + 1) * TK, :].T.astype(BF16)
        vwo_ref[kt] = vw_ref[kt * TK:(kt + 1) * TK, :].T.astype(BF16)


def _nsa_prep(proj, pe_k, w_k, pe_v, w_v, tabs, batch, layer):
    cs, sn, ccs, csn = tabs
    G = NSA_KV_HEADS
    ncr = SEQ // CMP_STRIDE
    col = lambda off: pl.BlockSpec((SEQ, HEAD_DIM), lambda b, g, off=off: (b, off // HEAD_DIM + g))
    full2 = lambda r: pl.BlockSpec((r, HEAD_DIM), lambda b, g: (0, 0))
    pespec = pl.BlockSpec((None, CMP_LEN, HEAD_DIM), lambda b, g: (layer, 0, 0))
    wspec = pl.BlockSpec((None, CMP_LEN, HEAD_DIM, HEAD_DIM), lambda b, g: (layer, 0, 0, 0))
    ocmp = pl.BlockSpec((None, None, ncr, HEAD_DIM), lambda b, g: (b, g, 0, 0))
    oseq = pl.BlockSpec((None, None, SEQ, HEAD_DIM), lambda b, g: (b, g, 0, 0))
    oext = pl.BlockSpec((None, None, SEQ, HEAD_DIM + LANE), lambda b, g: (b, g, 0, 0))
    cmp_shape = jax.ShapeDtypeStruct((batch, G, ncr, HEAD_DIM), BF16)
    seq_shape = jax.ShapeDtypeStruct((batch, G, SEQ, HEAD_DIM), BF16)
    ext_shape = jax.ShapeDtypeStruct((batch, G, SEQ, HEAD_DIM + LANE), BF16)
    oval = pl.BlockSpec((None, None, SEQ // TK, HEAD_DIM, TK), lambda b, g: (b, g, 0, 0, 0))
    val_shape = jax.ShapeDtypeStruct((batch, G, SEQ // TK, HEAD_DIM, TK), BF16)
    return pl.pallas_call(
        _nsa_prep_kernel,
        grid=(batch, G),
        in_specs=[
            col(OFF_NKC), col(OFF_NVC), col(OFF_NKS), col(OFF_NVS), col(OFF_NKW), col(OFF_NVW),
            pespec, wspec, pespec, wspec,
            full2(SEQ), full2(SEQ), full2(ncr), full2(ncr),
        ],
        out_specs=[ocmp, ocmp, oext, oval, oseq, oval],
        out_shape=[cmp_shape, cmp_shape, ext_shape, val_shape, seq_shape, val_shape],
        compiler_params=_cparams("parallel", "arbitrary"),
        name="nsa_prep",
    )(proj, proj, proj, proj, proj, proj, pe_k, w_k, pe_v, w_v, cs, sn, ccs, csn)


def _attend(k, v, q, bias, carry):
    m, l, acc = carry
    s = jnp.dot(k, q, preferred_element_type=F32)
    if bias is not None:
        s = s + bias
    m_new = jnp.maximum(m, jnp.max(s, axis=0, keepdims=True))
    alpha = jnp.exp2(m - m_new)
    p = jnp.exp2(s - m_new)
    l = alpha * l + jnp.sum(p, axis=0, keepdims=True)
    acc = alpha * acc + jnp.dot(v, p.astype(BF16), preferred_element_type=F32)
    return m_new, l, acc


def _nsa_kernel(q_ref, sm_ref, cs_ref, sn_ref, kc_ref, vc_ref, ks_ref, vs_ref, kw_ref, vw_ref, y_ref):
    g = pl.program_id(1)
    qi = pl.program_id(2)
    q0 = qi * TQ
    R = NSA_HPG * TQ
    scale = HEAD_DIM ** -0.5 * LOG2E
    cs = cs_ref[...]
    sn = sn_ref[...]
    q = jnp.concatenate(
        [(_rope(q_ref[:, h * HEAD_DIM:(h + 1) * HEAD_DIM], cs, sn) * scale).T.astype(BF16)
         for h in range(NSA_HPG)], axis=1)
    tok = lax.broadcasted_iota(jnp.int32, (LANE, R), 1) & (TQ - 1)
    rowi = lax.broadcasted_iota(jnp.int32, (LANE, R), 0)

    ncr = SEQ // CMP_STRIDE
    valid_c = (rowi * CMP_STRIDE + (CMP_LEN - 1)) <= q0 + tok
    s_c = jnp.dot(kc_ref[...], q, preferred_element_type=F32) + jnp.where(valid_c, 0.0, NEG)
    m_c = jnp.max(s_c, axis=0, keepdims=True)
    e_c = jnp.where(valid_c, jnp.exp2(s_c - m_c), 0.0)
    p_c = e_c * (1.0 / jnp.maximum(jnp.sum(e_c, axis=0, keepdims=True), 1e-30))
    o_cmp = jnp.dot(vc_ref[...], p_c.astype(BF16), preferred_element_type=F32)
    p_sum = p_c[:, 0:TQ] + p_c[:, TQ:2 * TQ] + p_c[:, 2 * TQ:3 * TQ] + p_c[:, 3 * TQ:4 * TQ]

    jj = lax.broadcasted_iota(jnp.int32, (LANE, ncr), 0)
    nn = lax.broadcasted_iota(jnp.int32, (LANE, ncr), 1)
    ov_t = ((nn * CMP_STRIDE <= jj * SEL_BLOCK + (SEL_BLOCK - 1))
            & (nn * CMP_STRIDE + (CMP_LEN - 1) >= jj * SEL_BLOCK)
            & (jj < N_SEL) & (nn < N_CMP)).astype(F32)
    imp_t = jnp.dot(ov_t, p_sum, precision=lax.Precision.HIGHEST,
                    preferred_element_type=F32)[0:N_SEL]
    j_i = lax.broadcasted_iota(jnp.int32, (N_SEL, TQ), 0)
    t_i = q0 + lax.broadcasted_iota(jnp.int32, (N_SEL, TQ), 1)
    qblk = t_i // SEL_BLOCK
    forced = (j_i == 0) | (j_i == qblk) | (j_i == qblk - 1)
    score = jnp.where(forced, FORCE, jnp.where(j_i * SEL_BLOCK <= t_i, imp_t, -FORCE))
    rank = jnp.zeros((N_SEL, TQ), jnp.int32)
    for jp in range(N_SEL):
        row = score[jp:jp + 1, :]
        beats = (row > score) | ((row == score) & (j_i > jp))
        rank = rank + beats.astype(jnp.int32)
    sel_bias_t = jnp.concatenate(
        [jnp.where(rank < SEL_COUNT, 0.0, NEG), jnp.zeros((LANE - N_SEL, TQ), F32)], axis=0)
    qx = jnp.concatenate([q, jnp.concatenate([sel_bias_t.astype(BF16)] * NSA_HPG, axis=1)], axis=0)

    keyi = lax.broadcasted_iota(jnp.int32, (TK, R), 0)
    toki = lax.broadcasted_iota(jnp.int32, (TK, R), 1) & (TQ - 1)
    causal = jnp.where(keyi <= toki, 0.0, NEG)

    init = (jnp.full((1, R), NEG, F32), jnp.zeros((1, R), F32), jnp.zeros((HEAD_DIM, R), F32))
    diag = pl.multiple_of(q0, TK)

    def sel_step(kt, carry):
        k0 = pl.multiple_of(kt * TK, TK)
        return _attend(ks_ref[pl.ds(k0, TK), :], vs_ref[kt], qx, None, carry)

    carry = lax.fori_loop(0, qi, sel_step, init)
    _, l_s, a_s = _attend(ks_ref[pl.ds(diag, TK), :], vs_ref[qi], qx, causal, carry)

    carry = _attend(kw_ref[pl.ds(diag, TK), :], vw_ref[qi], q, causal, init)
    t1 = jnp.maximum(qi - 1, 0)
    carry = _attend(kw_ref[pl.ds(pl.multiple_of(t1 * TK, TK), TK), :], vw_ref[t1], q,
                    jnp.where(qi >= 1, jnp.zeros((TK, R), F32), NEG), carry)
    t2 = jnp.maximum(qi - 2, 0)
    _, l_w, a_w = _attend(kw_ref[pl.ds(pl.multiple_of(t2 * TK, TK), TK), :], vw_ref[t2], q,
                          jnp.where(qi >= 2, jnp.where(keyi > toki, 0.0, NEG), NEG), carry)

    gl = jax.nn.sigmoid(sm_ref[...]).T
    o_slc = a_s * (1.0 / l_s)
    o_win = a_w * (1.0 / l_w)
    for h in range(NSA_HPG):
        cs_ = slice(h * TQ, (h + 1) * TQ)
        gate = []
        for c in range(3):
            r0 = GATE_LANE0 + h * 3 + c
            r1 = r0 + NSA_HPG * 3
            gate.append(jnp.where(g == 0, gl[r0:r0 + 1, :], gl[r1:r1 + 1, :]))
        y = gate[0] * o_cmp[:, cs_] + gate[1] * o_slc[:, cs_] + gate[2] * o_win[:, cs_]
        y_ref[:, h * HEAD_DIM:(h + 1) * HEAD_DIM] = y.T.astype(y_ref.dtype)


def _nsa(proj, prep, tabs, batch):
    kc, vc, ks, vs, kw, vw = prep
    cs, sn = tabs[0], tabs[1]
    m = proj.shape[0]
    G = NSA_KV_HEADS
    nq = SEQ // TQ
    ncr = SEQ // CMP_STRIDE
    gw = NSA_HPG * HEAD_DIM
    cmp_spec = pl.BlockSpec((None, None, ncr, HEAD_DIM), lambda b, g, i: (b, g, 0, 0))
    seq_spec = pl.BlockSpec((None, None, SEQ, HEAD_DIM), lambda b, g, i: (b, g, 0, 0))
    ext_spec = pl.BlockSpec((None, None, SEQ, HEAD_DIM + LANE), lambda b, g, i: (b, g, 0, 0))
    val_spec = pl.BlockSpec((None, None, SEQ // TK, HEAD_DIM, TK), lambda b, g, i: (b, g, 0, 0, 0))
    tab_spec = pl.BlockSpec((TQ, HEAD_DIM), lambda b, g, i: (i, 0))
    return pl.pallas_call(
        _nsa_kernel,
        grid=(batch, G, nq),
        in_specs=[
            pl.BlockSpec((TQ, gw), lambda b, g, i: (b * nq + i, OFF_NQ // gw + g)),
            pl.BlockSpec((TQ, LANE), lambda b, g, i: (b * nq + i, OFF_SMALL // LANE)),
            tab_spec, tab_spec, cmp_spec, cmp_spec, ext_spec, val_spec, seq_spec, val_spec,
        ],
        out_specs=pl.BlockSpec((TQ, gw), lambda b, g, i: (b * nq + i, g)),
        out_shape=jax.ShapeDtypeStruct((m, NSA_WIDTH), BF16),
        compiler_params=_cparams("parallel", "parallel", "arbitrary"),
        name="nsa_attn",
    )(proj, proj, cs, sn, kc, vc, ks, vs, kw, vw)


def _out_proj_kernel(ya_ref, yb_ref, yc_ref, wa_ref, wb_ref, wc_ref, x_ref, o_ref):
    acc = jnp.dot(ya_ref[...], wa_ref[...], preferred_element_type=F32)
    acc = acc + jnp.dot(yb_ref[...], wb_ref[...], preferred_element_type=F32)
    acc = acc + jnp.dot(yc_ref[...], wc_ref[...], preferred_element_type=F32)
    o_ref[...] = x_ref[...] + acc


def _out_proj(ya, yb, yc, w, x, layer):
    m, d = x.shape
    wa, wb, wc = ya.shape[1], yb.shape[1], yc.shape[1]
    return pl.pallas_call(
        _out_proj_kernel,
        grid=(m // TM_OUT, d // TN_OUT),
        in_specs=[
            pl.BlockSpec((TM_OUT, wa), lambda i, j: (i, 0)),
            pl.BlockSpec((TM_OUT, wb), lambda i, j: (i, 0)),
            pl.BlockSpec((TM_OUT, wc), lambda i, j: (i, 0)),
            pl.BlockSpec((None, wa, TN_OUT), lambda i, j: (layer, 0, j)),
            pl.BlockSpec((None, wb, TN_OUT), lambda i, j: (layer, wa // wb, j)),
            pl.BlockSpec((None, wc, TN_OUT), lambda i, j: (layer, (wa + wb) // wc, j)),
            pl.BlockSpec((TM_OUT, TN_OUT), lambda i, j: (i, j)),
        ],
        out_specs=pl.BlockSpec((TM_OUT, TN_OUT), lambda i, j: (i, j)),
        out_shape=jax.ShapeDtypeStruct((m, d), F32),
        compiler_params=_cparams("parallel", "arbitrary"),
        name="out_proj",
    )(ya, yb, yc, w, w, w, x)


def _ffn_up_kernel(x_ref, xh_ref, nw_ref, wg_ref, wv_ref, cg_ref, cv_ref, o_ref,
                   h_scr, hh_scr, g_scr, v_scr, out_scr):
    i = pl.program_id(0)

    @pl.when(pl.program_id(1) == 0)
    def _():
        h_scr[...] = _rms(x_ref[...], nw_ref[...]).astype(BF16)
        hh_scr[...] = _rms(xh_ref[...], nw_ref[...]).astype(BF16)

    keep = jnp.where((i * TM_UP) % SEQ == 0, 0.0, 1.0)
    cg = cg_ref[...]
    cv = cv_ref[...]

    def matmul(c0):
        cs = slice(c0, c0 + FFN_SEG_N)
        for w_ref, scr in ((wg_ref, g_scr), (wv_ref, v_scr)):
            w = w_ref[:, cs]
            scr[0:FFN_HALO, cs] = jnp.dot(hh_scr[...], w, preferred_element_type=F32) * keep
            scr[FFN_HALO:, cs] = jnp.dot(h_scr[...], w, preferred_element_type=F32)

    def conv(scr, c, r, cs):
        base = FFN_HALO - (FFN_CONV_LEN - 1) + r
        out = c[0:1, cs] * scr[base:base + FFN_EPI_ROWS, cs]
        for j in range(1, FFN_CONV_LEN):
            out = out + c[j:j + 1, cs] * scr[base + j:base + j + FFN_EPI_ROWS, cs]
        return out

    def gating(c0):
        cs = slice(c0, c0 + FFN_SEG_N)
        for r in range(0, TM_UP, FFN_EPI_ROWS):
            half = 0.5 * conv(g_scr, cg, r, cs)
            val = conv(v_scr, cv, r, cs)
            out_scr[r:r + FFN_EPI_ROWS, cs] = ((half + half * jnp.tanh(half)) * val).astype(BF16)

    for c0 in range(0, TN_UP, FFN_SEG_N):
        matmul(c0)
        gating(c0)
    o_ref[...] = out_scr[...]


def _ffn_up(x, nw, w_up, dw, layer):
    m, d = x.shape
    nj = D_FF // TN_UP
    hb = TM_UP // FFN_HALO
    raw = pltpu.VMEM((FFN_HALO + TM_UP, TN_UP), F32)
    return pl.pallas_call(
        _ffn_up_kernel,
        grid=(m // TM_UP, nj),
        in_specs=[
            pl.BlockSpec((TM_UP, d), lambda i, j: (i, 0)),
            pl.BlockSpec((FFN_HALO, d), lambda i, j: (jnp.maximum(i * hb - 1, 0), 0)),
            pl.BlockSpec((1, d), lambda i, j: (0, 0)),
            pl.BlockSpec((None, d, TN_UP), lambda i, j: (layer, 0, j)),
            pl.BlockSpec((None, d, TN_UP), lambda i, j: (layer, 0, j + nj)),
            pl.BlockSpec((None, FFN_CONV_LEN, TN_UP), lambda i, j: (layer, 0, j)),
            pl.BlockSpec((None, FFN_CONV_LEN, TN_UP), lambda i, j: (layer, 0, j + nj)),
        ],
        out_specs=pl.BlockSpec((TM_UP, TN_UP), lambda i, j: (i, j)),
        out_shape=jax.ShapeDtypeStruct((m, D_FF), BF16),
        scratch_shapes=[pltpu.VMEM((TM_UP, d), BF16), pltpu.VMEM((FFN_HALO, d), BF16), raw, raw,
                        pltpu.VMEM((TM_UP, TN_UP), BF16)],
        compiler_params=_cparams("parallel", "arbitrary"),
        name="ffn_up",
    )(x, x, nw, w_up, w_up, dw, dw)


def _norm_kernel(x_ref, w_ref, o_ref):
    o_ref[...] = _rms(x_ref[...], w_ref[...]).astype(o_ref.dtype)


def _norm(x, w, dtype):
    m, d = x.shape
    return pl.pallas_call(
        _norm_kernel,
        grid=(m // TM_NORM,),
        in_specs=[pl.BlockSpec((TM_NORM, d), lambda i: (i, 0)), pl.BlockSpec((1, d), lambda i: (0, 0))],
        out_specs=pl.BlockSpec((TM_NORM, d), lambda i: (i, 0)),
        out_shape=jax.ShapeDtypeStruct((m, d), dtype),
        compiler_params=_cparams("parallel"),
        name="rms_norm",
    )(x, w)


def _ffn_down_kernel(a_ref, w_ref, x_ref, o_ref):
    o_ref[...] = x_ref[...] + jnp.dot(a_ref[...], w_ref[...], preferred_element_type=F32)


def _ffn_down(a, w, x, layer):
    m, d = x.shape
    f = a.shape[1]
    return pl.pallas_call(
        _ffn_down_kernel,
        grid=(m // TM_DN, d // TN_DN),
        in_specs=[
            pl.BlockSpec((TM_DN, f), lambda i, j: (i, 0)),
            pl.BlockSpec((None, f, TN_DN), lambda i, j: (layer, 0, j)),
            pl.BlockSpec((TM_DN, TN_DN), lambda i, j: (i, j)),
        ],
        out_specs=pl.BlockSpec((TM_DN, TN_DN), lambda i, j: (i, j)),
        out_shape=jax.ShapeDtypeStruct((m, d), F32),
        compiler_params=_cparams("parallel", "arbitrary"),
        name="ffn_down",
    )(a, w, x)


def _rope_tables():
    half = HEAD_DIM // 2
    inv = ROPE_THETA ** (-np.arange(half, dtype=np.float64) / half)

    def tab(pos):
        ang = np.asarray(pos, np.float64)[:, None] * inv[None, :]
        cos, sin = np.cos(ang), np.sin(ang)
        return (jnp.asarray(np.concatenate([cos, cos], -1), F32),
                jnp.asarray(np.concatenate([-sin, sin], -1), F32))

    cs, sn = tab(np.arange(SEQ))
    ccs, csn = tab(np.arange(SEQ // CMP_STRIDE) * CMP_STRIDE + CMP_LEN - 1)
    return cs, sn, ccs, csn


def _reorder_w_in(w):
    a0 = 4 * MLSTM_WIDTH
    a1 = a0 + 2 * MLSTM_HEADS
    a2 = a1 + (OFF_SMALL - OFF_CA)
    a3 = a2 + 3 * NSA_HEADS
    pad = jnp.zeros(w.shape[:-1] + (N_PROJ - a3,), BF16)
    parts = [w[..., :a0], w[..., a1:a2], w[..., a0:a1], w[..., a2:a3]]
    return jnp.concatenate([p.astype(BF16) for p in parts] + [pad], axis=-1)


def kernel(x, attn_norm_w, w_in, mlstm_i_bias, mlstm_f_bias, mlstm_norm_w, conv_dw_w, conv_dw_b, conv_ln_w, conv_ln_b, nsa_cmp_pe_k, nsa_cmp_w_k, nsa_cmp_pe_v, nsa_cmp_w_v, w_out, ffn_norm_w, w_up, ffn_dw_w, w_down, final_norm_w):
    batch, seq, d = x.shape
    assert seq == SEQ and d == D_MODEL
    depth = w_in.shape[0]
    tabs = _rope_tables()
    w_in_b = _reorder_w_in(w_in)
    w_out_b = w_out.astype(BF16)
    w_up_b = w_up.astype(BF16)
    w_down_b = w_down.astype(BF16)
    w_ck = nsa_cmp_w_k.reshape(depth, CMP_LEN, HEAD_DIM, HEAD_DIM).astype(BF16)
    w_cv = nsa_cmp_w_v.reshape(depth, CMP_LEN, HEAD_DIM, HEAD_DIM).astype(BF16)
    xf = x.reshape(batch * seq, d)
    for l in range(depth):
        proj = _norm_matmul(xf, attn_norm_w[l][None, :], w_in_b, l)
        gate_bias = jnp.concatenate(
            [mlstm_i_bias[l], mlstm_f_bias[l], jnp.zeros((LANE - 2 * MLSTM_HEADS,), F32)])[None, :]
        y_a = _mlstm(proj, gate_bias, mlstm_norm_w[l][None, :], batch)
        y_b = _conformer(proj, conv_dw_w[l], conv_dw_b[l][None, :], conv_ln_w[l][None, :],
                         conv_ln_b[l][None, :], batch)
        prep = _nsa_prep(proj, nsa_cmp_pe_k, w_ck, nsa_cmp_pe_v, w_cv, tabs, batch, l)
        y_c = _nsa(proj, prep, tabs, batch)
        xf = _out_proj(y_a, y_b, y_c, w_out_b, xf, l)
        act = _ffn_up(xf, ffn_norm_w[l][None, :], w_up_b, ffn_dw_w, l)
        xf = _ffn_down(act, w_down_b, xf, l)
    return _norm(xf, final_norm_w[None, :], F32).reshape(batch, seq, d)
```

```python
import functools

import numpy as np
import jax
import jax.numpy as jnp
from jax import lax
from jax.experimental import pallas as pl
from jax.experimental.pallas import tpu as pltpu

F32 = jnp.float32
BF16 = jnp.bfloat16

D_MODEL = 2048
SEQ = 2048
HEAD_DIM = 128
MLSTM_HEADS = 4
MLSTM_WIDTH = MLSTM_HEADS * HEAD_DIM
MLSTM_CHUNK = 64
CONV_WIDTH = 512
CONV_LEN = 31
NSA_HEADS = 8
NSA_KV_HEADS = 2
NSA_HPG = NSA_HEADS // NSA_KV_HEADS
NSA_WIDTH = NSA_HEADS * HEAD_DIM
CMP_LEN = 32
CMP_STRIDE = 16
N_CMP = (SEQ - CMP_LEN) // CMP_STRIDE + 1
SEL_BLOCK = 64
SEL_COUNT = 16
N_SEL = SEQ // SEL_BLOCK
WINDOW = 512
ROPE_THETA = 10000.0
D_FF = 5632
FFN_CONV_LEN = 3
EPS = 1e-6
NEG = -1e30
FORCE = 1e4

OFF_MQ, OFF_MK, OFF_MV, OFF_MO = 0, 512, 1024, 1536
OFF_CA, OFF_CG = 2048, 2560
OFF_NQ = 3072
OFF_NKC, OFF_NVC, OFF_NKS, OFF_NVS, OFF_NKW, OFF_NVW = 4096, 4352, 4608, 4864, 5120, 5376
OFF_SMALL = 5632
N_PROJ = 5760
LANE = 128
GATE_LANE0 = 8

VMEM_LIMIT = 52 * 1024 * 1024

TM_IN, TN_IN = 1024, 1152
TM_OUT, TN_OUT = 1024, 1024
TM_UP, TN_UP = 1024, 512
TM_DN, TN_DN = 1024, 512
TM_NORM = 512
TK_REORDER = 256
MLSTM_ROWS = 2 * MLSTM_CHUNK
CONV_ROWS = 256
CONV_HALO = 32
SUBLANE = 8
CONV_PHASE_ROWS = CONV_HALO + CONV_ROWS - SUBLANE
CONV_CHUNK = 32
FFN_HALO = 16
FFN_EPI_ROWS = 32
FFN_SEG_N = 256
TQ = 256
TK = 256
assert TQ == TK and WINDOW == 2 * TK
LOG2E = 1.4426950408889634


def _cparams(*sem):
    return pltpu.CompilerParams(dimension_semantics=sem, vmem_limit_bytes=VMEM_LIMIT)


def _rms(x, w):
    return x * lax.rsqrt(jnp.mean(x * x, axis=-1, keepdims=True) + EPS) * w


def _norm_matmul_kernel(x_ref, nw_ref, w_ref, o_ref, h_scr):
    @pl.when(pl.program_id(1) == 0)
    def _():
        h_scr[...] = _rms(x_ref[...], nw_ref[...]).astype(BF16)

    o_ref[...] = jnp.dot(h_scr[...], w_ref[...], preferred_element_type=F32)


def _norm_matmul(x, nw, w, layer):
    m, d = x.shape
    n = w.shape[2]
    return pl.pallas_call(
        _norm_matmul_kernel,
        grid=(m // TM_IN, n // TN_IN),
        in_specs=[
            pl.BlockSpec((TM_IN, d), lambda i, j: (i, 0)),
            pl.BlockSpec((1, d), lambda i, j: (0, 0)),
            pl.BlockSpec((None, d, TN_IN), lambda i, j: (layer, 0, j)),
        ],
        out_specs=pl.BlockSpec((TM_IN, TN_IN), lambda i, j: (i, j)),
        out_shape=jax.ShapeDtypeStruct((m, n), F32),
        scratch_shapes=[pltpu.VMEM((TM_IN, d), BF16)],
        compiler_params=_cparams("parallel", "arbitrary"),
        name="in_proj",
    )(x, nw, w)


def _log_sigmoid(x):
    return jnp.minimum(x, 0.0) - jnp.log1p(jnp.exp(-jnp.abs(x)))


def _mlstm_kernel(q_ref, k_ref, v_ref, o_ref, g_ref, gb_ref, nw_ref, y_ref, c_scr, n_scr, m_scr):
    L = MLSTM_CHUNK

    @pl.when(pl.program_id(1) == 0)
    def _():
        c_scr[...] = jnp.zeros_like(c_scr)
        n_scr[...] = jnp.zeros_like(n_scr)
        m_scr[...] = jnp.full_like(m_scr, -jnp.inf)

    gates = g_ref[...] + gb_ref[...]
    logf = _log_sigmoid(gates)
    row = lax.broadcasted_iota(jnp.int32, (L, L), 0)
    col = lax.broadcasted_iota(jnp.int32, (L, L), 1)
    causal = col <= row
    tri = causal.astype(F32)
    b_all = jnp.concatenate(
        [jnp.dot(tri, logf[c * L:(c + 1) * L], precision=lax.Precision.HIGHEST,
                 preferred_element_type=F32) for c in range(2)], axis=0)
    b_t = b_all.T
    g_t = gates.T
    scale = HEAD_DIM ** -0.5

    heads = range(MLSTM_HEADS)
    hsl = [slice(h * HEAD_DIM, (h + 1) * HEAD_DIM) for h in heads]
    per_head = lambda f: jnp.stack([f(h) for h in heads], axis=0)
    k_all = [k_ref[:, hsl[h]] * scale for h in heads]
    k_allt = [k.T for k in k_all]
    nw = per_head(lambda h: nw_ref[:, hsl[h]])
    c_st = c_scr[...]
    n_st = n_scr[...]
    m_st = m_scr[...][:, :, 0:1]
    bdot = lambda a, b, ca, cb: lax.dot_general(
        a.astype(BF16), b.astype(BF16), (((ca,), (cb,)), ((0,), (0,))), preferred_element_type=F32)
    for c in range(2):
        rs = slice(c * L, (c + 1) * L)
        qc = per_head(lambda h: q_ref[rs, hsl[h]])
        kc = per_head(lambda h: k_all[h][rs])
        kct = per_head(lambda h: k_allt[h][:, rs])
        vc = per_head(lambda h: v_ref[rs, hsl[h]])
        b_col = per_head(lambda h: b_all[rs, 4 + h:5 + h])
        i_col = per_head(lambda h: gates[rs, h:h + 1])
        b_row = per_head(lambda h: b_t[4 + h:5 + h, rs])
        i_row = per_head(lambda h: g_t[h:h + 1, rs])
        dmat = jnp.where(causal[None], b_col - b_row + i_row, -jnp.inf)
        inter = b_col + m_st
        m_t = jnp.maximum(inter, jnp.max(dmat, axis=-1, keepdims=True))
        s = bdot(qc, kc, 2, 2) * jnp.exp(dmat - m_t)
        w_inter = jnp.exp(inter - m_t)
        num = bdot(s, vc, 2, 1) + w_inter * bdot(qc, c_st, 2, 1)
        den = (jnp.sum(s, axis=-1, keepdims=True)
               + w_inter * jnp.sum(qc * n_st, axis=-1, keepdims=True))
        hh = num / jnp.maximum(jnp.abs(den), jnp.exp(-m_t))
        b_last = b_col[:, L - 1:L, :]
        g = b_last - b_col + i_col
        m_new = jnp.maximum(b_last + m_st, jnp.max(g, axis=1, keepdims=True))
        w_k = jnp.exp(g - m_new)
        decay = jnp.exp(b_last + m_st - m_new)
        c_st = decay * c_st + bdot(kct, w_k * vc, 2, 1)
        n_st = decay * n_st + jnp.sum(w_k * kc, axis=1, keepdims=True)
        m_st = m_new
        mu = jnp.mean(hh, axis=-1, keepdims=True)
        hc = hh - mu
        hn = hc * lax.rsqrt(jnp.mean(hc * hc, axis=-1, keepdims=True) + EPS) * nw
        for h in heads:
            y_ref[rs, hsl[h]] = (jax.nn.sigmoid(o_ref[rs, hsl[h]]) * hn[h]).astype(y_ref.dtype)
    c_scr[...] = c_st
    n_scr[...] = n_st
    m_scr[...] = jnp.broadcast_to(m_st, (MLSTM_HEADS, 1, LANE))


def _mlstm(proj, gate_bias, norm_w, batch):
    m = proj.shape[0]
    nsteps = SEQ // MLSTM_ROWS
    blk = lambda cb: pl.BlockSpec((MLSTM_ROWS, MLSTM_WIDTH), lambda b, c, cb=cb: (b * nsteps + c, cb))
    return pl.pallas_call(
        _mlstm_kernel,
        grid=(batch, nsteps),
        in_specs=[
            blk(OFF_MQ // MLSTM_WIDTH), blk(OFF_MK // MLSTM_WIDTH),
            blk(OFF_MV // MLSTM_WIDTH), blk(OFF_MO // MLSTM_WIDTH),
            pl.BlockSpec((MLSTM_ROWS, LANE), lambda b, c: (b * nsteps + c, OFF_SMALL // LANE)),
            pl.BlockSpec((1, LANE), lambda b, c: (0, 0)),
            pl.BlockSpec((1, MLSTM_WIDTH), lambda b, c: (0, 0)),
        ],
        out_specs=pl.BlockSpec((MLSTM_ROWS, MLSTM_WIDTH), lambda b, c: (b * nsteps + c, 0)),
        out_shape=jax.ShapeDtypeStruct((m, MLSTM_WIDTH), BF16),
        scratch_shapes=[
            pltpu.VMEM((MLSTM_HEADS, HEAD_DIM, HEAD_DIM), F32),
            pltpu.VMEM((MLSTM_HEADS, 1, HEAD_DIM), F32),
            pltpu.VMEM((MLSTM_HEADS, 1, LANE), F32),
        ],
        compiler_params=_cparams("parallel", "arbitrary"),
        name="mlstm",
    )(proj, proj, proj, proj, proj, gate_bias, norm_w)


def _conv_kernel(a_ref, g_ref, ah_ref, gh_ref, w_ref, b_ref, lw_ref, lb_ref, y_ref, u_scr, s_scr):
    first = pl.program_id(1) == 0
    uh = ah_ref[...] * jax.nn.sigmoid(gh_ref[...])
    u_scr[0:CONV_HALO, :] = jnp.where(first, 0.0, uh)
    u_scr[CONV_HALO:, :] = a_ref[...] * jax.nn.sigmoid(g_ref[...])
    for ph in range(1, SUBLANE):
        s_scr[ph] = u_scr[ph:ph + CONV_PHASE_ROWS, :]
    w = w_ref[...]
    bias = b_ref[...]
    lw = lw_ref[...]
    lb = lb_ref[...]
    base = CONV_HALO - (CONV_LEN - 1)
    for r in range(0, CONV_ROWS, CONV_CHUNK):
        acc = jnp.zeros((CONV_CHUNK, CONV_WIDTH), F32)
        for j in range(CONV_LEN):
            ph, al = (base + j) % SUBLANE, (base + j) // SUBLANE * SUBLANE
            src = u_scr[al + r:al + r + CONV_CHUNK, :] if ph == 0 else s_scr[ph, al + r:al + r + CONV_CHUNK, :]
            acc = acc + w[j:j + 1, :] * src
        u = acc + bias
        mu = jnp.mean(u, axis=-1, keepdims=True)
        uc = u - mu
        y = uc * lax.rsqrt(jnp.mean(uc * uc, axis=-1, keepdims=True) + EPS) * lw + lb
        y_ref[r:r + CONV_CHUNK, :] = (y * jax.nn.sigmoid(y)).astype(y_ref.dtype)


def _conformer(proj, dw_w, dw_b, ln_w, ln_b, batch):
    m = proj.shape[0]
    nt = SEQ // CONV_ROWS
    hb = CONV_ROWS // CONV_HALO
    main = lambda cb: pl.BlockSpec((CONV_ROWS, CONV_WIDTH), lambda b, i, cb=cb: (b * nt + i, cb))
    halo = lambda cb: pl.BlockSpec(
        (CONV_HALO, CONV_WIDTH), lambda b, i, cb=cb: (jnp.maximum((b * nt + i) * hb - 1, 0), cb))
    vec = pl.BlockSpec((1, CONV_WIDTH), lambda b, i: (0, 0))
    return pl.pallas_call(
        _conv_kernel,
        grid=(batch, nt),
        in_specs=[
            main(OFF_CA // CONV_WIDTH), main(OFF_CG // CONV_WIDTH),
            halo(OFF_CA // CONV_WIDTH), halo(OFF_CG // CONV_WIDTH),
            pl.BlockSpec((CONV_LEN, CONV_WIDTH), lambda b, i: (0, 0)),
            vec, vec, vec,
        ],
        out_specs=pl.BlockSpec((CONV_ROWS, CONV_WIDTH), lambda b, i: (b * nt + i, 0)),
        out_shape=jax.ShapeDtypeStruct((m, CONV_WIDTH), BF16),
        scratch_shapes=[pltpu.VMEM((CONV_HALO + CONV_ROWS, CONV_WIDTH), F32),
                        pltpu.VMEM((SUBLANE, CONV_PHASE_ROWS, CONV_WIDTH), F32)],
        compiler_params=_cparams("parallel", "arbitrary"),
        name="conformer_conv",
    )(proj, proj, proj, proj, dw_w, dw_b, ln_w, ln_b)


def _rope(x, cs, sn):
    return x * cs + pltpu.roll(x, HEAD_DIM // 2, axis=1) * sn


def _compress(x_ref, pe_ref, w_ref):
    nrow = SEQ // CMP_STRIDE
    lo = jnp.zeros((nrow, HEAD_DIM), F32)
    hi = jnp.zeros((nrow, HEAD_DIM), F32)
    for l in range(CMP_STRIDE):
        xl = x_ref[pl.ds(l, nrow, stride=CMP_STRIDE), :]
        lo = lo + jnp.dot((xl + pe_ref[l:l + 1, :]).astype(BF16), w_ref[l],
                          preferred_element_type=F32)
        hi = hi + jnp.dot((xl + pe_ref[CMP_STRIDE + l:CMP_STRIDE + l + 1, :]).astype(BF16),
                          w_ref[CMP_STRIDE + l], preferred_element_type=F32)
    return lo + pltpu.roll(hi, nrow - 1, axis=0)


def _nsa_prep_kernel(kc_ref, vc_ref, ks_ref, vs_ref, kw_ref, vw_ref, pek_ref, wk_ref, pev_ref, wv_ref,
                     cs_ref, sn_ref, ccs_ref, csn_ref,
                     kco_ref, vco_ref, kso_ref, vso_ref, kwo_ref, vwo_ref):
    kco_ref[...] = _rope(_compress(kc_ref, pek_ref, wk_ref), ccs_ref[...], csn_ref[...]).astype(BF16)
    vco_ref[...] = _compress(vc_ref, pev_ref, wv_ref).T.astype(BF16)
    cs = cs_ref[...]
    sn = sn_ref[...]
    kso_ref[:, 0:HEAD_DIM] = _rope(ks_ref[...], cs, sn).astype(BF16)
    key = lax.broadcasted_iota(jnp.int32, (SEQ, LANE), 0)
    blk = lax.broadcasted_iota(jnp.int32, (SEQ, LANE), 1)
    kso_ref[:, HEAD_DIM:] = jnp.where(key // SEL_BLOCK == blk, 1.0, 0.0).astype(BF16)
    kwo_ref[...] = _rope(kw_ref[...], cs, sn).astype(BF16)
    for kt in range(SEQ // TK):
        vso_ref[kt] = vs_ref[kt * TK:(kt + 1) * TK, :].T.astype(BF16)
        vwo_ref[kt] = vw_ref[kt * TK:(kt + 1) * TK, :].T.astype(BF16)


def _nsa_prep(proj, pe_k, w_k, pe_v, w_v, tabs, batch, layer):
    cs, sn, ccs, csn = tabs
    G = NSA_KV_HEADS
    ncr = SEQ // CMP_STRIDE
    col = lambda off: pl.BlockSpec((SEQ, HEAD_DIM), lambda b, g, off=off: (b, off // HEAD_DIM + g))
    full2 = lambda r: pl.BlockSpec((r, HEAD_DIM), lambda b, g: (0, 0))
    pespec = pl.BlockSpec((None, CMP_LEN, HEAD_DIM), lambda b, g: (layer, 0, 0))
    wspec = pl.BlockSpec((None, CMP_LEN, HEAD_DIM, HEAD_DIM), lambda b, g: (layer, 0, 0, 0))
    ocmp = pl.BlockSpec((None, None, ncr, HEAD_DIM), lambda b, g: (b, g, 0, 0))
    oseq = pl.BlockSpec((None, None, SEQ, HEAD_DIM), lambda b, g: (b, g, 0, 0))
    oext = pl.BlockSpec((None, None, SEQ, HEAD_DIM + LANE), lambda b, g: (b, g, 0, 0))
    cmp_shape = jax.ShapeDtypeStruct((batch, G, ncr, HEAD_DIM), BF16)
    seq_shape = jax.ShapeDtypeStruct((batch, G, SEQ, HEAD_DIM), BF16)
    ext_shape = jax.ShapeDtypeStruct((batch, G, SEQ, HEAD_DIM + LANE), BF16)
    oval = pl.BlockSpec((None, None, SEQ // TK, HEAD_DIM, TK), lambda b, g: (b, g, 0, 0, 0))
    val_shape = jax.ShapeDtypeStruct((batch, G, SEQ // TK, HEAD_DIM, TK), BF16)
    return pl.pallas_call(
        _nsa_prep_kernel,
        grid=(batch, G),
        in_specs=[
            col(OFF_NKC), col(OFF_NVC), col(OFF_NKS), col(OFF_NVS), col(OFF_NKW), col(OFF_NVW),
            pespec, wspec, pespec, wspec,
            full2(SEQ), full2(SEQ), full2(ncr), full2(ncr),
        ],
        out_specs=[ocmp, ocmp, oext, oval, oseq, oval],
        out_shape=[cmp_shape, cmp_shape, ext_shape, val_shape, seq_shape, val_shape],
        compiler_params=_cparams("parallel", "arbitrary"),
        name="nsa_prep",
    )(proj, proj, proj, proj, proj, proj, pe_k, w_k, pe_v, w_v, cs, sn, ccs, csn)


def _attend(k, v, q, bias, carry):
    m, l, acc = carry
    s = jnp.dot(k, q, preferred_element_type=F32)
    if bias is not None:
        s = s + bias
    m_new = jnp.maximum(m, jnp.max(s, axis=0, keepdims=True))
    alpha = jnp.exp2(m - m_new)
    p = jnp.exp2(s - m_new)
    l = alpha * l + jnp.sum(p, axis=0, keepdims=True)
    acc = alpha * acc + jnp.dot(v, p.astype(BF16), preferred_element_type=F32)
    return m_new, l, acc


def _nsa_kernel(q_ref, sm_ref, cs_ref, sn_ref, kc_ref, vc_ref, ks_ref, vs_ref, kw_ref, vw_ref, y_ref):
    g = pl.program_id(1)
    qi = pl.program_id(2)
    q0 = qi * TQ
    R = NSA_HPG * TQ
    scale = HEAD_DIM ** -0.5 * LOG2E
    cs = cs_ref[...]
    sn = sn_ref[...]
    q = jnp.concatenate(
        [(_rope(q_ref[:, h * HEAD_DIM:(h + 1) * HEAD_DIM], cs, sn) * scale).T.astype(BF16)
         for h in range(NSA_HPG)], axis=1)
    tok = lax.broadcasted_iota(jnp.int32, (LANE, R), 1) & (TQ - 1)
    rowi = lax.broadcasted_iota(jnp.int32, (LANE, R), 0)

    ncr = SEQ // CMP_STRIDE
    valid_c = (rowi * CMP_STRIDE + (CMP_LEN - 1)) <= q0 + tok
    s_c = jnp.dot(kc_ref[...], q, preferred_element_type=F32) + jnp.where(valid_c, 0.0, NEG)
    m_c = jnp.max(s_c, axis=0, keepdims=True)
    e_c = jnp.where(valid_c, jnp.exp2(s_c - m_c), 0.0)
    p_c = e_c * (1.0 / jnp.maximum(jnp.sum(e_c, axis=0, keepdims=True), 1e-30))
    o_cmp = jnp.dot(vc_ref[...], p_c.astype(BF16), preferred_element_type=F32)
    p_sum = p_c[:, 0:TQ] + p_c[:, TQ:2 * TQ] + p_c[:, 2 * TQ:3 * TQ] + p_c[:, 3 * TQ:4 * TQ]

    jj = lax.broadcasted_iota(jnp.int32, (LANE, ncr), 0)
    nn = lax.broadcasted_iota(jnp.int32, (LANE, ncr), 1)
    ov_t = ((nn * CMP_STRIDE <= jj * SEL_BLOCK + (SEL_BLOCK - 1))
            & (nn * CMP_STRIDE + (CMP_LEN - 1) >= jj * SEL_BLOCK)
            & (jj < N_SEL) & (nn < N_CMP)).astype(F32)
    imp_t = jnp.dot(ov_t, p_sum, precision=lax.Precision.HIGHEST,
                    preferred_element_type=F32)[0:N_SEL]
    j_i = lax.broadcasted_iota(jnp.int32, (N_SEL, TQ), 0)
    t_i = q0 + lax.broadcasted_iota(jnp.int32, (N_SEL, TQ), 1)
    qblk = t_i // SEL_BLOCK
    forced = (j_i == 0) | (j_i == qblk) | (j_i == qblk - 1)
    score = jnp.where(forced, FORCE, jnp.where(j_i * SEL_BLOCK <= t_i, imp_t, -FORCE))
    rank = jnp.zeros((N_SEL, TQ), jnp.int32)
    for jp in range(N_SEL):
        row = score[jp:jp + 1, :]
        beats = (row > score) | ((row == score) & (j_i > jp))
        rank = rank + beats.astype(jnp.int32)
    sel_bias_t = jnp.concatenate(
        [jnp.where(rank < SEL_COUNT, 0.0, NEG), jnp.zeros((LANE - N_SEL, TQ), F32)], axis=0)
    qx = jnp.concatenate([q, jnp.concatenate([sel_bias_t.astype(BF16)] * NSA_HPG, axis=1)], axis=0)

    keyi = lax.broadcasted_iota(jnp.int32, (TK, R), 0)
    toki = lax.broadcasted_iota(jnp.int32, (TK, R), 1) & (TQ - 1)
    causal = jnp.where(keyi <= toki, 0.0, NEG)

    init = (jnp.full((1, R), NEG, F32), jnp.zeros((1, R), F32), jnp.zeros((HEAD_DIM, R), F32))
    diag = pl.multiple_of(q0, TK)

    def sel_step(kt, carry):
        k0 = pl.multiple_of(kt * TK, TK)
        return _attend(ks_ref[pl.ds(k0, TK), :], vs_ref[kt], qx, None, carry)

    carry = lax.fori_loop(0, qi, sel_step, init)
    _, l_s, a_s = _attend(ks_ref[pl.ds(diag, TK), :], vs_ref[qi], qx, causal, carry)

    carry = _attend(kw_ref[pl.ds(diag, TK), :], vw_ref[qi], q, causal, init)
    t1 = jnp.maximum(qi - 1, 0)
    carry = _attend(kw_ref[pl.ds(pl.multiple_of(t1 * TK, TK), TK), :], vw_ref[t1], q,
                    jnp.where(qi >= 1, jnp.zeros((TK, R), F32), NEG), carry)
    t2 = jnp.maximum(qi - 2, 0)
    _, l_w, a_w = _attend(kw_ref[pl.ds(pl.multiple_of(t2 * TK, TK), TK), :], vw_ref[t2], q,
                          jnp.where(qi >= 2, jnp.where(keyi > toki, 0.0, NEG), NEG), carry)

    gl = jax.nn.sigmoid(sm_ref[...]).T
    o_slc = a_s * (1.0 / l_s)
    o_win = a_w * (1.0 / l_w)
    for h in range(NSA_HPG):
        cs_ = slice(h * TQ, (h + 1) * TQ)
        gate = []
        for c in range(3):
            r0 = GATE_LANE0 + h * 3 + c
            r1 = r0 + NSA_HPG * 3
            gate.append(jnp.where(g == 0, gl[r0:r0 + 1, :], gl[r1:r1 + 1, :]))
        y = gate[0] * o_cmp[:, cs_] + gate[1] * o_slc[:, cs_] + gate[2] * o_win[:, cs_]
        y_ref[:, h * HEAD_DIM:(h + 1) * HEAD_DIM] = y.T.astype(y_ref.dtype)


def _nsa(proj, prep, tabs, batch):
    kc, vc, ks, vs, kw, vw = prep
    cs, sn = tabs[0], tabs[1]
    m = proj.shape[0]
    G = NSA_KV_HEADS
    nq = SEQ // TQ
    ncr = SEQ // CMP_STRIDE
    gw = NSA_HPG * HEAD_DIM
    cmp_spec = pl.BlockSpec((None, None, ncr, HEAD_DIM), lambda b, g, i: (b, g, 0, 0))
    seq_spec = pl.BlockSpec((None, None, SEQ, HEAD_DIM), lambda b, g, i: (b, g, 0, 0))
    ext_spec = pl.BlockSpec((None, None, SEQ, HEAD_DIM + LANE), lambda b, g, i: (b, g, 0, 0))
    val_spec = pl.BlockSpec((None, None, SEQ // TK, HEAD_DIM, TK), lambda b, g, i: (b, g, 0, 0, 0))
    tab_spec = pl.BlockSpec((TQ, HEAD_DIM), lambda b, g, i: (i, 0))
    return pl.pallas_call(
        _nsa_kernel,
        grid=(batch, G, nq),
        in_specs=[
            pl.BlockSpec((TQ, gw), lambda b, g, i: (b * nq + i, OFF_NQ // gw + g)),
            pl.BlockSpec((TQ, LANE), lambda b, g, i: (b * nq + i, OFF_SMALL // LANE)),
            tab_spec, tab_spec, cmp_spec, cmp_spec, ext_spec, val_spec, seq_spec, val_spec,
        ],
        out_specs=pl.BlockSpec((TQ, gw), lambda b, g, i: (b * nq + i, g)),
        out_shape=jax.ShapeDtypeStruct((m, NSA_WIDTH), BF16),
        compiler_params=_cparams("parallel", "parallel", "arbitrary"),
        name="nsa_attn",
    )(proj, proj, cs, sn, kc, vc, ks, vs, kw, vw)


def _out_proj_kernel(ya_ref, yb_ref, yc_ref, wa_ref, wb_ref, wc_ref, x_ref, o_ref):
    acc = jnp.dot(ya_ref[...], wa_ref[...], preferred_element_type=F32)
    acc = acc + jnp.dot(yb_ref[...], wb_ref[...], preferred_element_type=F32)
    acc = acc + jnp.dot(yc_ref[...], wc_ref[...], preferred_element_type=F32)
    o_ref[...] = x_ref[...] + acc


def _out_proj(ya, yb, yc, w, x, layer):
    m, d = x.shape
    wa, wb, wc = ya.shape[1], yb.shape[1], yc.shape[1]
    return pl.pallas_call(
        _out_proj_kernel,
        grid=(m // TM_OUT, d // TN_OUT),
        in_specs=[
            pl.BlockSpec((TM_OUT, wa), lambda i, j: (i, 0)),
            pl.BlockSpec((TM_OUT, wb), lambda i, j: (i, 0)),
            pl.BlockSpec((TM_OUT, wc), lambda i, j: (i, 0)),
            pl.BlockSpec((None, wa, TN_OUT), lambda i, j: (layer, 0, j)),
            pl.BlockSpec((None, wb, TN_OUT), lambda i, j: (layer, wa // wb, j)),
            pl.BlockSpec((None, wc, TN_OUT), lambda i, j: (layer, (wa + wb) // wc, j)),
            pl.BlockSpec((TM_OUT, TN_OUT), lambda i, j: (i, j)),
        ],
        out_specs=pl.BlockSpec((TM_OUT, TN_OUT), lambda i, j: (i, j)),
        out_shape=jax.ShapeDtypeStruct((m, d), F32),
        compiler_params=_cparams("parallel", "arbitrary"),
        name="out_proj",
    )(ya, yb, yc, w, w, w, x)


def _ffn_up_kernel(x_ref, xh_ref, nw_ref, wg_ref, wv_ref, cg_ref, cv_ref, o_ref,
                   h_scr, hh_scr, g_scr, v_scr, out_scr):
    i = pl.program_id(0)

    @pl.when(pl.program_id(1) == 0)
    def _():
        h_scr[...] = _rms(x_ref[...], nw_ref[...]).astype(BF16)
        hh_scr[...] = _rms(xh_ref[...], nw_ref[...]).astype(BF16)

    keep = jnp.where((i * TM_UP) % SEQ == 0, 0.0, 1.0)
    cg = cg_ref[...]
    cv = cv_ref[...]

    def matmul(c0):
        cs = slice(c0, c0 + FFN_SEG_N)
        for w_ref, scr in ((wg_ref, g_scr), (wv_ref, v_scr)):
            w = w_ref[:, cs]
            scr[0:FFN_HALO, cs] = jnp.dot(hh_scr[...], w, preferred_element_type=F32) * keep
            scr[FFN_HALO:, cs] = jnp.dot(h_scr[...], w, preferred_element_type=F32)

    def conv(scr, c, r, cs):
        base = FFN_HALO - (FFN_CONV_LEN - 1) + r
        out = c[0:1, cs] * scr[base:base + FFN_EPI_ROWS, cs]
        for j in range(1, FFN_CONV_LEN):
            out = out + c[j:j + 1, cs] * scr[base + j:base + j + FFN_EPI_ROWS, cs]
        return out

    def gating(c0):
        cs = slice(c0, c0 + FFN_SEG_N)
        for r in range(0, TM_UP, FFN_EPI_ROWS):
            half = 0.5 * conv(g_scr, cg, r, cs)
            val = conv(v_scr, cv, r, cs)
            out_scr[r:r + FFN_EPI_ROWS, cs] = ((half + half * jnp.tanh(half)) * val).astype(BF16)

    for c0 in range(0, TN_UP, FFN_SEG_N):
        matmul(c0)
        gating(c0)
    o_ref[...] = out_scr[...]


def _ffn_up(x, nw, w_up, dw, layer):
    m, d = x.shape
    nj = D_FF // TN_UP
    hb = TM_UP // FFN_HALO
    raw = pltpu.VMEM((FFN_HALO + TM_UP, TN_UP), F32)
    return pl.pallas_call(
        _ffn_up_kernel,
        grid=(m // TM_UP, nj),
        in_specs=[
            pl.BlockSpec((TM_UP, d), lambda i, j: (i, 0)),
            pl.BlockSpec((FFN_HALO, d), lambda i, j: (jnp.maximum(i * hb - 1, 0), 0)),
            pl.BlockSpec((1, d), lambda i, j: (0, 0)),
            pl.BlockSpec((None, d, TN_UP), lambda i, j: (layer, 0, j)),
            pl.BlockSpec((None, d, TN_UP), lambda i, j: (layer, 0, j + nj)),
            pl.BlockSpec((None, FFN_CONV_LEN, TN_UP), lambda i, j: (layer, 0, j)),
            pl.BlockSpec((None, FFN_CONV_LEN, TN_UP), lambda i, j: (layer, 0, j + nj)),
        ],
        out_specs=pl.BlockSpec((TM_UP, TN_UP), lambda i, j: (i, j)),
        out_shape=jax.ShapeDtypeStruct((m, D_FF), BF16),
        scratch_shapes=[pltpu.VMEM((TM_UP, d), BF16), pltpu.VMEM((FFN_HALO, d), BF16), raw, raw,
                        pltpu.VMEM((TM_UP, TN_UP), BF16)],
        compiler_params=_cparams("parallel", "arbitrary"),
        name="ffn_up",
    )(x, x, nw, w_up, w_up, dw, dw)


def _norm_kernel(x_ref, w_ref, o_ref):
    o_ref[...] = _rms(x_ref[...], w_ref[...]).astype(o_ref.dtype)


def _norm(x, w, dtype):
    m, d = x.shape
    return pl.pallas_call(
        _norm_kernel,
        grid=(m // TM_NORM,),
        in_specs=[pl.BlockSpec((TM_NORM, d), lambda i: (i, 0)), pl.BlockSpec((1, d), lambda i: (0, 0))],
        out_specs=pl.BlockSpec((TM_NORM, d), lambda i: (i, 0)),
        out_shape=jax.ShapeDtypeStruct((m, d), dtype),
        compiler_params=_cparams("parallel"),
        name="rms_norm",
    )(x, w)


def _ffn_down_kernel(a_ref, w_ref, x_ref, o_ref):
    o_ref[...] = x_ref[...] + jnp.dot(a_ref[...], w_ref[...], preferred_element_type=F32)


def _ffn_down(a, w, x, layer):
    m, d = x.shape
    f = a.shape[1]
    return pl.pallas_call(
        _ffn_down_kernel,
        grid=(m // TM_DN, d // TN_DN),
        in_specs=[
            pl.BlockSpec((TM_DN, f), lambda i, j: (i, 0)),
            pl.BlockSpec((None, f, TN_DN), lambda i, j: (layer, 0, j)),
            pl.BlockSpec((TM_DN, TN_DN), lambda i, j: (i, j)),
        ],
        out_specs=pl.BlockSpec((TM_DN, TN_DN), lambda i, j: (i, j)),
        out_shape=jax.ShapeDtypeStruct((m, d), F32),
        compiler_params=_cparams("parallel", "arbitrary"),
        name="ffn_down",
    )(a, w, x)


def _rope_tables():
    half = HEAD_DIM // 2
    inv = ROPE_THETA ** (-np.arange(half, dtype=np.float64) / half)

    def tab(pos):
        ang = np.asarray(pos, np.float64)[:, None] * inv[None, :]
        cos, sin = np.cos(ang), np.sin(ang)
        return (jnp.asarray(np.concatenate([cos, cos], -1), F32),
                jnp.asarray(np.concatenate([-sin, sin], -1), F32))

    cs, sn = tab(np.arange(SEQ))
    ccs, csn = tab(np.arange(SEQ // CMP_STRIDE) * CMP_STRIDE + CMP_LEN - 1)
    return cs, sn, ccs, csn


def _reorder_kernel(w_ref, o_ref):
    a0 = 4 * MLSTM_WIDTH
    a1 = a0 + 2 * MLSTM_HEADS
    a2 = a1 + (OFF_SMALL - OFF_CA)
    a3 = a2 + 3 * NSA_HEADS
    s1 = OFF_SMALL + (a1 - a0)
    s2 = s1 + (a3 - a2)
    o_ref[:, 0:a0] = w_ref[:, 0:a0].astype(BF16)
    o_ref[:, a0:OFF_SMALL] = w_ref[:, a1:a2].astype(BF16)
    o_ref[:, OFF_SMALL:s1] = w_ref[:, a0:a1].astype(BF16)
    o_ref[:, s1:s2] = w_ref[:, a2:a3].astype(BF16)
    o_ref[:, s2:] = jnp.zeros((o_ref.shape[0], N_PROJ - s2), BF16)


def _reorder_w_in(w):
    depth, d, n = w.shape
    return pl.pallas_call(
        _reorder_kernel,
        grid=(depth, d // TK_REORDER),
        in_specs=[pl.BlockSpec((None, TK_REORDER, n), lambda l, k: (l, k, 0))],
        out_specs=pl.BlockSpec((None, TK_REORDER, N_PROJ), lambda l, k: (l, k, 0)),
        out_shape=jax.ShapeDtypeStruct((depth, d, N_PROJ), BF16),
        compiler_params=_cparams("parallel", "parallel"),
        name="w_in_layout",
    )(w)


def kernel(x, attn_norm_w, w_in, mlstm_i_bias, mlstm_f_bias, mlstm_norm_w, conv_dw_w, conv_dw_b, conv_ln_w, conv_ln_b, nsa_cmp_pe_k, nsa_cmp_w_k, nsa_cmp_pe_v, nsa_cmp_w_v, w_out, ffn_norm_w, w_up, ffn_dw_w, w_down, final_norm_w):
    batch, seq, d = x.shape
    assert seq == SEQ and d == D_MODEL
    depth = w_in.shape[0]
    tabs = _rope_tables()
    w_in_b = _reorder_w_in(w_in)
    w_out_b = w_out.astype(BF16)
    w_up_b = w_up.astype(BF16)
    w_down_b = w_down.astype(BF16)
    w_ck = nsa_cmp_w_k.reshape(depth, CMP_LEN, HEAD_DIM, HEAD_DIM).astype(BF16)
    w_cv = nsa_cmp_w_v.reshape(depth, CMP_LEN, HEAD_DIM, HEAD_DIM).astype(BF16)
    xf = x.reshape(batch * seq, d)
    for l in range(depth):
        proj = _norm_matmul(xf, attn_norm_w[l][None, :], w_in_b, l)
        gate_bias = jnp.concatenate(
            [mlstm_i_bias[l], mlstm_f_bias[l], jnp.zeros((LANE - 2 * MLSTM_HEADS,), F32)])[None, :]
        y_a = _mlstm(proj, gate_bias, mlstm_norm_w[l][None, :], batch)
        y_b = _conformer(proj, conv_dw_w[l], conv_dw_b[l][None, :], conv_ln_w[l][None, :],
                         conv_ln_b[l][None, :], batch)
        prep = _nsa_prep(proj, nsa_cmp_pe_k, w_ck, nsa_cmp_pe_v, w_cv, tabs, batch, l)
        y_c = _nsa(proj, prep, tabs, batch)
        xf = _out_proj(y_a, y_b, y_c, w_out_b, xf, l)
        act = _ffn_up(xf, ffn_norm_w[l][None, :], w_up_b, ffn_dw_w, l)
        xf = _ffn_down(act, w_down_b, xf, l)
    return _norm(xf, final_norm_w[None, :], F32).reshape(batch, seq, d)
```

```python
import functools

import numpy as np
import jax
import jax.numpy as jnp
from jax import lax
from jax.experimental import pallas as pl
from jax.experimental.pallas import tpu as pltpu

F32 = jnp.float32
BF16 = jnp.bfloat16

D_MODEL = 2048
SEQ = 2048
HEAD_DIM = 128
MLSTM_HEADS = 4
MLSTM_WIDTH = MLSTM_HEADS * HEAD_DIM
MLSTM_CHUNK = 64
CONV_WIDTH = 512
CONV_LEN = 31
NSA_HEADS = 8
NSA_KV_HEADS = 2
NSA_HPG = NSA_HEADS // NSA_KV_HEADS
NSA_WIDTH = NSA_HEADS * HEAD_DIM
CMP_LEN = 32
CMP_STRIDE = 16
N_CMP = (SEQ - CMP_LEN) // CMP_STRIDE + 1
SEL_BLOCK = 64
SEL_COUNT = 16
N_SEL = SEQ // SEL_BLOCK
WINDOW = 512
ROPE_THETA = 10000.0
D_FF = 5632
FFN_CONV_LEN = 3
EPS = 1e-6
NEG = -1e30
FORCE = 1e4

OFF_MQ, OFF_MK, OFF_MV, OFF_MO = 0, 512, 1024, 1536
OFF_CA, OFF_CG = 2048, 2560
OFF_NQ = 3072
OFF_NKC, OFF_NVC, OFF_NKS, OFF_NVS, OFF_NKW, OFF_NVW = 4096, 4352, 4608, 4864, 5120, 5376
OFF_SMALL = 5632
N_PROJ = 5760
LANE = 128
GATE_LANE0 = 8

VMEM_LIMIT = 52 * 1024 * 1024

TM_IN, TN_IN = 1024, 1152
TM_OUT, TN_OUT = 1024, 1024
TM_UP, TN_UP = 1024, 512
TM_DN, TN_DN = 1024, 512
TM_NORM = 512
TK_REORDER = 256
MLSTM_ROWS = 2 * MLSTM_CHUNK
MLSTM_BATCH = 4
CONV_ROWS = 256
CONV_HALO = 32
SUBLANE = 8
CONV_PHASE_ROWS = CONV_HALO + CONV_ROWS - SUBLANE
CONV_CHUNK = 32
FFN_HALO = 16
FFN_EPI_ROWS = 32
TQ = 256
TK = 256
assert TQ == TK and WINDOW == 2 * TK
LOG2E = 1.4426950408889634


def _cparams(*sem):
    return pltpu.CompilerParams(dimension_semantics=sem, vmem_limit_bytes=VMEM_LIMIT)


def _rms(x, w):
    return x * lax.rsqrt(jnp.mean(x * x, axis=-1, keepdims=True) + EPS) * w


def _norm_matmul_kernel(x_ref, nw_ref, w_ref, o_ref, h_scr):
    @pl.when(pl.program_id(1) == 0)
    def _():
        h_scr[...] = _rms(x_ref[...], nw_ref[...]).astype(BF16)

    o_ref[...] = jnp.dot(h_scr[...], w_ref[...], preferred_element_type=F32)


def _norm_matmul(x, nw, w, layer):
    m, d = x.shape
    n = w.shape[2]
    return pl.pallas_call(
        _norm_matmul_kernel,
        grid=(m // TM_IN, n // TN_IN),
        in_specs=[
            pl.BlockSpec((TM_IN, d), lambda i, j: (i, 0)),
            pl.BlockSpec((1, d), lambda i, j: (0, 0)),
            pl.BlockSpec((None, d, TN_IN), lambda i, j: (layer, 0, j)),
        ],
        out_specs=pl.BlockSpec((TM_IN, TN_IN), lambda i, j: (i, j)),
        out_shape=jax.ShapeDtypeStruct((m, n), F32),
        scratch_shapes=[pltpu.VMEM((TM_IN, d), BF16)],
        compiler_params=_cparams("parallel", "arbitrary"),
        name="in_proj",
    )(x, nw, w)


def _log_sigmoid(x):
    return jnp.minimum(x, 0.0) - jnp.log1p(jnp.exp(-jnp.abs(x)))


def _mlstm_kernel(q_ref, k_ref, v_ref, o_ref, g_ref, gb_ref, nw_ref, y_ref, c_scr, n_scr, m_scr):
    L = MLSTM_CHUNK

    @pl.when(pl.program_id(1) == 0)
    def _():
        c_scr[...] = jnp.zeros_like(c_scr)
        n_scr[...] = jnp.zeros_like(n_scr)
        m_scr[...] = jnp.full_like(m_scr, -jnp.inf)

    row = lax.broadcasted_iota(jnp.int32, (L, L), 0)
    col = lax.broadcasted_iota(jnp.int32, (L, L), 1)
    causal = col <= row
    tri = causal.astype(F32)
    seqs = range(q_ref.shape[0])
    gates = [g_ref[b] + gb_ref[...] for b in seqs]
    b_all = [jnp.concatenate(
        [jnp.dot(tri, _log_sigmoid(gates[b][c * L:(c + 1) * L]), precision=lax.Precision.HIGHEST,
                 preferred_element_type=F32) for c in range(2)], axis=0) for b in seqs]
    b_t = [x.T for x in b_all]
    g_t = [x.T for x in gates]
    scale = HEAD_DIM ** -0.5

    pairs = [(b, h) for b in seqs for h in range(MLSTM_HEADS)]
    hsl = [slice(h * HEAD_DIM, (h + 1) * HEAD_DIM) for h in range(MLSTM_HEADS)]
    per_head = lambda f: jnp.stack([f(b, h) for b, h in pairs], axis=0)
    k_all = {(b, h): k_ref[b, :, hsl[h]] * scale for b, h in pairs}
    k_allt = {p: k.T for p, k in k_all.items()}
    nw = per_head(lambda b, h: nw_ref[:, hsl[h]])
    c_st = c_scr[...]
    n_st = n_scr[...]
    m_st = m_scr[...][:, :, 0:1]
    bdot = lambda a, b, ca, cb: lax.dot_general(
        a.astype(BF16), b.astype(BF16), (((ca,), (cb,)), ((0,), (0,))), preferred_element_type=F32)
    for c in range(2):
        rs = slice(c * L, (c + 1) * L)
        qc = per_head(lambda b, h: q_ref[b, rs, hsl[h]])
        kc = per_head(lambda b, h: k_all[b, h][rs])
        kct = per_head(lambda b, h: k_allt[b, h][:, rs])
        vc = per_head(lambda b, h: v_ref[b, rs, hsl[h]])
        b_col = per_head(lambda b, h: b_all[b][rs, 4 + h:5 + h])
        i_col = per_head(lambda b, h: gates[b][rs, h:h + 1])
        b_row = per_head(lambda b, h: b_t[b][4 + h:5 + h, rs])
        i_row = per_head(lambda b, h: g_t[b][h:h + 1, rs])
        dmat = jnp.where(causal[None], b_col - b_row + i_row, -jnp.inf)
        inter = b_col + m_st
        m_t = jnp.maximum(inter, jnp.max(dmat, axis=-1, keepdims=True))
        s = bdot(qc, kc, 2, 2) * jnp.exp(dmat - m_t)
        w_inter = jnp.exp(inter - m_t)
        num = bdot(s, vc, 2, 1) + w_inter * bdot(qc, c_st, 2, 1)
        den = (jnp.sum(s, axis=-1, keepdims=True)
               + w_inter * jnp.sum(qc * n_st, axis=-1, keepdims=True))
        hh = num / jnp.maximum(jnp.abs(den), jnp.exp(-m_t))
        b_last = b_col[:, L - 1:L, :]
        g = b_last - b_col + i_col
        m_new = jnp.maximum(b_last + m_st, jnp.max(g, axis=1, keepdims=True))
        w_k = jnp.exp(g - m_new)
        decay = jnp.exp(b_last + m_st - m_new)
        c_st = decay * c_st + bdot(kct, w_k * vc, 2, 1)
        n_st = decay * n_st + jnp.sum(w_k * kc, axis=1, keepdims=True)
        m_st = m_new
        mu = jnp.mean(hh, axis=-1, keepdims=True)
        hc = hh - mu
        hn = hc * lax.rsqrt(jnp.mean(hc * hc, axis=-1, keepdims=True) + EPS) * nw
        for p, (b, h) in enumerate(pairs):
            y_ref[b, rs, hsl[h]] = (jax.nn.sigmoid(o_ref[b, rs, hsl[h]]) * hn[p]).astype(y_ref.dtype)
    c_scr[...] = c_st
    n_scr[...] = n_st
    m_scr[...] = jnp.broadcast_to(m_st, m_scr.shape)


def _mlstm(proj, gate_bias, norm_w, batch):
    m, n = proj.shape
    proj3 = proj.reshape(batch, SEQ, n)
    nsteps = SEQ // MLSTM_ROWS
    nb = MLSTM_BATCH
    npair = nb * MLSTM_HEADS
    blk = lambda cb: pl.BlockSpec((nb, MLSTM_ROWS, MLSTM_WIDTH), lambda b, c, cb=cb: (b, c, cb))
    y = pl.pallas_call(
        _mlstm_kernel,
        grid=(batch // nb, nsteps),
        in_specs=[
            blk(OFF_MQ // MLSTM_WIDTH), blk(OFF_MK // MLSTM_WIDTH),
            blk(OFF_MV // MLSTM_WIDTH), blk(OFF_MO // MLSTM_WIDTH),
            pl.BlockSpec((nb, MLSTM_ROWS, LANE), lambda b, c: (b, c, OFF_SMALL // LANE)),
            pl.BlockSpec((1, LANE), lambda b, c: (0, 0)),
            pl.BlockSpec((1, MLSTM_WIDTH), lambda b, c: (0, 0)),
        ],
        out_specs=pl.BlockSpec((nb, MLSTM_ROWS, MLSTM_WIDTH), lambda b, c: (b, c, 0)),
        out_shape=jax.ShapeDtypeStruct((batch, SEQ, MLSTM_WIDTH), BF16),
        scratch_shapes=[
            pltpu.VMEM((npair, HEAD_DIM, HEAD_DIM), F32),
            pltpu.VMEM((npair, 1, HEAD_DIM), F32),
            pltpu.VMEM((npair, 1, LANE), F32),
        ],
        compiler_params=_cparams("parallel", "arbitrary"),
        name="mlstm",
    )(proj3, proj3, proj3, proj3, proj3, gate_bias, norm_w)
    return y.reshape(m, MLSTM_WIDTH)


def _conv_kernel(a_ref, g_ref, ah_ref, gh_ref, w_ref, b_ref, lw_ref, lb_ref, y_ref, u_scr, s_scr):
    first = pl.program_id(1) == 0
    uh = ah_ref[...] * jax.nn.sigmoid(gh_ref[...])
    u_scr[0:CONV_HALO, :] = jnp.where(first, 0.0, uh)
    u_scr[CONV_HALO:, :] = a_ref[...] * jax.nn.sigmoid(g_ref[...])
    for ph in range(1, SUBLANE):
        s_scr[ph] = u_scr[ph:ph + CONV_PHASE_ROWS, :]
    w = w_ref[...]
    bias = b_ref[...]
    lw = lw_ref[...]
    lb = lb_ref[...]
    base = CONV_HALO - (CONV_LEN - 1)
    for r in range(0, CONV_ROWS, CONV_CHUNK):
        acc = jnp.zeros((CONV_CHUNK, CONV_WIDTH), F32)
        for j in range(CONV_LEN):
            ph, al = (base + j) % SUBLANE, (base + j) // SUBLANE * SUBLANE
            src = u_scr[al + r:al + r + CONV_CHUNK, :] if ph == 0 else s_scr[ph, al + r:al + r + CONV_CHUNK, :]
            acc = acc + w[j:j + 1, :] * src
        u = acc + bias
        mu = jnp.mean(u, axis=-1, keepdims=True)
        uc = u - mu
        y = uc * lax.rsqrt(jnp.mean(uc * uc, axis=-1, keepdims=True) + EPS) * lw + lb
        y_ref[r:r + CONV_CHUNK, :] = (y * jax.nn.sigmoid(y)).astype(y_ref.dtype)


def _conformer(proj, dw_w, dw_b, ln_w, ln_b, batch):
    m = proj.shape[0]
    nt = SEQ // CONV_ROWS
    hb = CONV_ROWS // CONV_HALO
    main = lambda cb: pl.BlockSpec((CONV_ROWS, CONV_WIDTH), lambda b, i, cb=cb: (b * nt + i, cb))
    halo = lambda cb: pl.BlockSpec(
        (CONV_HALO, CONV_WIDTH), lambda b, i, cb=cb: (jnp.maximum((b * nt + i) * hb - 1, 0), cb))
    vec = pl.BlockSpec((1, CONV_WIDTH), lambda b, i: (0, 0))
    return pl.pallas_call(
        _conv_kernel,
        grid=(batch, nt),
        in_specs=[
            main(OFF_CA // CONV_WIDTH), main(OFF_CG // CONV_WIDTH),
            halo(OFF_CA // CONV_WIDTH), halo(OFF_CG // CONV_WIDTH),
            pl.BlockSpec((CONV_LEN, CONV_WIDTH), lambda b, i: (0, 0)),
            vec, vec, vec,
        ],
        out_specs=pl.BlockSpec((CONV_ROWS, CONV_WIDTH), lambda b, i: (b * nt + i, 0)),
        out_shape=jax.ShapeDtypeStruct((m, CONV_WIDTH), BF16),
        scratch_shapes=[pltpu.VMEM((CONV_HALO + CONV_ROWS, CONV_WIDTH), F32),
                        pltpu.VMEM((SUBLANE, CONV_PHASE_ROWS, CONV_WIDTH), F32)],
        compiler_params=_cparams("parallel", "arbitrary"),
        name="conformer_conv",
    )(proj, proj, proj, proj, dw_w, dw_b, ln_w, ln_b)


def _rope(x, cs, sn):
    return x * cs + pltpu.roll(x, HEAD_DIM // 2, axis=1) * sn


def _compress(x_ref, pe_ref, w_ref):
    nrow = SEQ // CMP_STRIDE
    lo = jnp.zeros((nrow, HEAD_DIM), F32)
    hi = jnp.zeros((nrow, HEAD_DIM), F32)
    for l in range(CMP_STRIDE):
        xl = x_ref[pl.ds(l, nrow, stride=CMP_STRIDE), :]
        lo = lo + jnp.dot((xl + pe_ref[l:l + 1, :]).astype(BF16), w_ref[l],
                          preferred_element_type=F32)
        hi = hi + jnp.dot((xl + pe_ref[CMP_STRIDE + l:CMP_STRIDE + l + 1, :]).astype(BF16),
                          w_ref[CMP_STRIDE + l], preferred_element_type=F32)
    return lo + pltpu.roll(hi, nrow - 1, axis=0)


def _nsa_prep_kernel(kc_ref, vc_ref, ks_ref, vs_ref, kw_ref, vw_ref, pek_ref, wk_ref, pev_ref, wv_ref,
                     cs_ref, sn_ref, ccs_ref, csn_ref,
                     kco_ref, vco_ref, kso_ref, vso_ref, kwo_ref, vwo_ref):
    kco_ref[...] = _rope(_compress(kc_ref, pek_ref, wk_ref), ccs_ref[...], csn_ref[...]).astype(BF16)
    vco_ref[...] = _compress(vc_ref, pev_ref, wv_ref).T.astype(BF16)
    cs = cs_ref[...]
    sn = sn_ref[...]
    kso_ref[:, 0:HEAD_DIM] = _rope(ks_ref[...], cs, sn).astype(BF16)
    key = lax.broadcasted_iota(jnp.int32, (SEQ, LANE), 0)
    blk = lax.broadcasted_iota(jnp.int32, (SEQ, LANE), 1)
    kso_ref[:, HEAD_DIM:] = jnp.where(key // SEL_BLOCK == blk, 1.0, 0.0).astype(BF16)
    kwo_ref[...] = _rope(kw_ref[...], cs, sn).astype(BF16)
    for kt in range(SEQ // TK):
        vso_ref[kt] = vs_ref[kt * TK:(kt + 1) * TK, :].T.astype(BF16)
        vwo_ref[kt] = vw_ref[kt * TK:(kt + 1) * TK, :].T.astype(BF16)


def _nsa_prep(proj, pe_k, w_k, pe_v, w_v, tabs, batch, layer):
    cs, sn, ccs, csn = tabs
    G = NSA_KV_HEADS
    ncr = SEQ // CMP_STRIDE
    col = lambda off: pl.BlockSpec((SEQ, HEAD_DIM), lambda b, g, off=off: (b, off // HEAD_DIM + g))
    full2 = lambda r: pl.BlockSpec((r, HEAD_DIM), lambda b, g: (0, 0))
    pespec = pl.BlockSpec((None, CMP_LEN, HEAD_DIM), lambda b, g: (layer, 0, 0))
    wspec = pl.BlockSpec((None, CMP_LEN, HEAD_DIM, HEAD_DIM), lambda b, g: (layer, 0, 0, 0))
    ocmp = pl.BlockSpec((None, None, ncr, HEAD_DIM), lambda b, g: (b, g, 0, 0))
    oseq = pl.BlockSpec((None, None, SEQ, HEAD_DIM), lambda b, g: (b, g, 0, 0))
    oext = pl.BlockSpec((None, None, SEQ, HEAD_DIM + LANE), lambda b, g: (b, g, 0, 0))
    cmp_shape = jax.ShapeDtypeStruct((batch, G, ncr, HEAD_DIM), BF16)
    seq_shape = jax.ShapeDtypeStruct((batch, G, SEQ, HEAD_DIM), BF16)
    ext_shape = jax.ShapeDtypeStruct((batch, G, SEQ, HEAD_DIM + LANE), BF16)
    oval = pl.BlockSpec((None, None, SEQ // TK, HEAD_DIM, TK), lambda b, g: (b, g, 0, 0, 0))
    val_shape = jax.ShapeDtypeStruct((batch, G, SEQ // TK, HEAD_DIM, TK), BF16)
    return pl.pallas_call(
        _nsa_prep_kernel,
        grid=(batch, G),
        in_specs=[
            col(OFF_NKC), col(OFF_NVC), col(OFF_NKS), col(OFF_NVS), col(OFF_NKW), col(OFF_NVW),
            pespec, wspec, pespec, wspec,
            full2(SEQ), full2(SEQ), full2(ncr), full2(ncr),
        ],
        out_specs=[ocmp, ocmp, oext, oval, oseq, oval],
        out_shape=[cmp_shape, cmp_shape, ext_shape, val_shape, seq_shape, val_shape],
        compiler_params=_cparams("parallel", "arbitrary"),
        name="nsa_prep",
    )(proj, proj, proj, proj, proj, proj, pe_k, w_k, pe_v, w_v, cs, sn, ccs, csn)


def _attend(k, v, q, bias, carry, acc_ref):
    m, l = carry
    s = jnp.dot(k, q, preferred_element_type=F32)
    if bias is not None:
        s = s + bias
    m_new = jnp.maximum(m, jnp.max(s, axis=0, keepdims=True))
    alpha = jnp.exp2(m - m_new)
    p = jnp.exp2(s - m_new)
    l = alpha * l + jnp.sum(p, axis=0, keepdims=True)
    acc_ref[...] = alpha * acc_ref[...] + jnp.dot(v, p.astype(BF16), preferred_element_type=F32)
    return m_new, l


def _nsa_kernel(q_ref, sm_ref, cs_ref, sn_ref, kc_ref, vc_ref, ks_ref, vs_ref, kw_ref, vw_ref, y_ref,
                as_scr, aw_scr):
    g = pl.program_id(1)
    qi = pl.program_id(2)
    q0 = qi * TQ
    R = NSA_HPG * TQ
    scale = HEAD_DIM ** -0.5 * LOG2E
    cs = cs_ref[...]
    sn = sn_ref[...]
    q = jnp.concatenate(
        [(_rope(q_ref[:, h * HEAD_DIM:(h + 1) * HEAD_DIM], cs, sn) * scale).T.astype(BF16)
         for h in range(NSA_HPG)], axis=1)
    tok = lax.broadcasted_iota(jnp.int32, (LANE, R), 1) & (TQ - 1)
    rowi = lax.broadcasted_iota(jnp.int32, (LANE, R), 0)

    ncr = SEQ // CMP_STRIDE
    valid_c = (rowi * CMP_STRIDE + (CMP_LEN - 1)) <= q0 + tok
    s_c = jnp.dot(kc_ref[...], q, preferred_element_type=F32) + jnp.where(valid_c, 0.0, NEG)
    m_c = jnp.max(s_c, axis=0, keepdims=True)
    e_c = jnp.where(valid_c, jnp.exp2(s_c - m_c), 0.0)
    p_c = e_c * (1.0 / jnp.maximum(jnp.sum(e_c, axis=0, keepdims=True), 1e-30))
    o_cmp = jnp.dot(vc_ref[...], p_c.astype(BF16), preferred_element_type=F32)
    p_sum = p_c[:, 0:TQ] + p_c[:, TQ:2 * TQ] + p_c[:, 2 * TQ:3 * TQ] + p_c[:, 3 * TQ:4 * TQ]

    jj = lax.broadcasted_iota(jnp.int32, (LANE, ncr), 0)
    nn = lax.broadcasted_iota(jnp.int32, (LANE, ncr), 1)
    ov_t = ((nn * CMP_STRIDE <= jj * SEL_BLOCK + (SEL_BLOCK - 1))
            & (nn * CMP_STRIDE + (CMP_LEN - 1) >= jj * SEL_BLOCK)
            & (jj < N_SEL) & (nn < N_CMP)).astype(F32)
    imp_t = jnp.dot(ov_t, p_sum, precision=lax.Precision.HIGHEST,
                    preferred_element_type=F32)[0:N_SEL]
    j_i = lax.broadcasted_iota(jnp.int32, (N_SEL, TQ), 0)
    t_i = q0 + lax.broadcasted_iota(jnp.int32, (N_SEL, TQ), 1)
    qblk = t_i // SEL_BLOCK
    forced = (j_i == 0) | (j_i == qblk) | (j_i == qblk - 1)
    score = jnp.where(forced, FORCE, jnp.where(j_i * SEL_BLOCK <= t_i, imp_t, -FORCE))
    rank = jnp.zeros((N_SEL, TQ), jnp.int32)
    for jp in range(N_SEL):
        row = score[jp:jp + 1, :]
        beats = (row > score) | ((row == score) & (j_i > jp))
        rank = rank + beats.astype(jnp.int32)
    sel_bias_t = jnp.concatenate(
        [jnp.where(rank < SEL_COUNT, 0.0, NEG), jnp.zeros((LANE - N_SEL, TQ), F32)], axis=0)
    qx = jnp.concatenate([q, jnp.concatenate([sel_bias_t.astype(BF16)] * NSA_HPG, axis=1)], axis=0)

    keyi = lax.broadcasted_iota(jnp.int32, (TK, R), 0)
    toki = lax.broadcasted_iota(jnp.int32, (TK, R), 1) & (TQ - 1)
    causal = jnp.where(keyi <= toki, 0.0, NEG)

    init = (jnp.full((1, R), NEG, F32), jnp.zeros((1, R), F32))
    as_scr[...] = jnp.zeros_like(as_scr)
    aw_scr[...] = jnp.zeros_like(aw_scr)
    diag = pl.multiple_of(q0, TK)

    def sel_step(kt, carry):
        k0 = pl.multiple_of(kt * TK, TK)
        return _attend(ks_ref[pl.ds(k0, TK), :], vs_ref[kt], qx, None, carry, as_scr)

    carry = lax.fori_loop(0, qi, sel_step, init)
    _, l_s = _attend(ks_ref[pl.ds(diag, TK), :], vs_ref[qi], qx, causal, carry, as_scr)

    carry = _attend(kw_ref[pl.ds(diag, TK), :], vw_ref[qi], q, causal, init, aw_scr)
    t1 = jnp.maximum(qi - 1, 0)
    carry = _attend(kw_ref[pl.ds(pl.multiple_of(t1 * TK, TK), TK), :], vw_ref[t1], q,
                    jnp.where(qi >= 1, jnp.zeros((TK, R), F32), NEG), carry, aw_scr)
    t2 = jnp.maximum(qi - 2, 0)
    _, l_w = _attend(kw_ref[pl.ds(pl.multiple_of(t2 * TK, TK), TK), :], vw_ref[t2], q,
                     jnp.where(qi >= 2, jnp.where(keyi > toki, 0.0, NEG), NEG), carry, aw_scr)

    gl = jax.nn.sigmoid(sm_ref[...]).T
    o_slc = as_scr[...] * (1.0 / l_s)
    o_win = aw_scr[...] * (1.0 / l_w)
    for h in range(NSA_HPG):
        cs_ = slice(h * TQ, (h + 1) * TQ)
        gate = []
        for c in range(3):
            r0 = GATE_LANE0 + h * 3 + c
            r1 = r0 + NSA_HPG * 3
            gate.append(jnp.where(g == 0, gl[r0:r0 + 1, :], gl[r1:r1 + 1, :]))
        y = gate[0] * o_cmp[:, cs_] + gate[1] * o_slc[:, cs_] + gate[2] * o_win[:, cs_]
        y_ref[:, h * HEAD_DIM:(h + 1) * HEAD_DIM] = y.T.astype(y_ref.dtype)


def _nsa(proj, prep, tabs, batch):
    kc, vc, ks, vs, kw, vw = prep
    cs, sn = tabs[0], tabs[1]
    m = proj.shape[0]
    G = NSA_KV_HEADS
    nq = SEQ // TQ
    ncr = SEQ // CMP_STRIDE
    gw = NSA_HPG * HEAD_DIM
    cmp_spec = pl.BlockSpec((None, None, ncr, HEAD_DIM), lambda b, g, i: (b, g, 0, 0))
    seq_spec = pl.BlockSpec((None, None, SEQ, HEAD_DIM), lambda b, g, i: (b, g, 0, 0))
    ext_spec = pl.BlockSpec((None, None, SEQ, HEAD_DIM + LANE), lambda b, g, i: (b, g, 0, 0))
    val_spec = pl.BlockSpec((None, None, SEQ // TK, HEAD_DIM, TK), lambda b, g, i: (b, g, 0, 0, 0))
    tab_spec = pl.BlockSpec((TQ, HEAD_DIM), lambda b, g, i: (i, 0))
    return pl.pallas_call(
        _nsa_kernel,
        grid=(batch, G, nq),
        in_specs=[
            pl.BlockSpec((TQ, gw), lambda b, g, i: (b * nq + i, OFF_NQ // gw + g)),
            pl.BlockSpec((TQ, LANE), lambda b, g, i: (b * nq + i, OFF_SMALL // LANE)),
            tab_spec, tab_spec, cmp_spec, cmp_spec, ext_spec, val_spec, seq_spec, val_spec,
        ],
        out_specs=pl.BlockSpec((TQ, gw), lambda b, g, i: (b * nq + i, g)),
        out_shape=jax.ShapeDtypeStruct((m, NSA_WIDTH), BF16),
        scratch_shapes=[pltpu.VMEM((HEAD_DIM, NSA_HPG * TQ), F32), pltpu.VMEM((HEAD_DIM, NSA_HPG * TQ), F32)],
        compiler_params=_cparams("parallel", "parallel", "arbitrary"),
        name="nsa_attn",
    )(proj, proj, cs, sn, kc, vc, ks, vs, kw, vw)


def _out_proj_kernel(ya_ref, yb_ref, yc_ref, wa_ref, wb_ref, wc_ref, x_ref, o_ref):
    acc = jnp.dot(ya_ref[...], wa_ref[...], preferred_element_type=F32)
    acc = acc + jnp.dot(yb_ref[...], wb_ref[...], preferred_element_type=F32)
    acc = acc + jnp.dot(yc_ref[...], wc_ref[...], preferred_element_type=F32)
    o_ref[...] = x_ref[...] + acc


def _out_proj(ya, yb, yc, w, x, layer):
    m, d = x.shape
    wa, wb, wc = ya.shape[1], yb.shape[1], yc.shape[1]
    return pl.pallas_call(
        _out_proj_kernel,
        grid=(m // TM_OUT, d // TN_OUT),
        in_specs=[
            pl.BlockSpec((TM_OUT, wa), lambda i, j: (i, 0)),
            pl.BlockSpec((TM_OUT, wb), lambda i, j: (i, 0)),
            pl.BlockSpec((TM_OUT, wc), lambda i, j: (i, 0)),
            pl.BlockSpec((None, wa, TN_OUT), lambda i, j: (layer, 0, j)),
            pl.BlockSpec((None, wb, TN_OUT), lambda i, j: (layer, wa // wb, j)),
            pl.BlockSpec((None, wc, TN_OUT), lambda i, j: (layer, (wa + wb) // wc, j)),
            pl.BlockSpec((TM_OUT, TN_OUT), lambda i, j: (i, j)),
        ],
        out_specs=pl.BlockSpec((TM_OUT, TN_OUT), lambda i, j: (i, j)),
        out_shape=jax.ShapeDtypeStruct((m, d), F32),
        compiler_params=_cparams("parallel", "arbitrary"),
        name="out_proj",
    )(ya, yb, yc, w, w, w, x)


def _ffn_up_kernel(x_ref, xh_ref, nw_ref, wg_ref, wv_ref, cg_ref, cv_ref, o_ref,
                   h_scr, hh_scr, g_scr, v_scr):
    i = pl.program_id(0)

    @pl.when(pl.program_id(1) == 0)
    def _():
        h_scr[...] = _rms(x_ref[...], nw_ref[...]).astype(BF16)
        hh_scr[...] = _rms(xh_ref[...], nw_ref[...]).astype(BF16)

    keep = jnp.where((i * TM_UP) % SEQ == 0, 0.0, 1.0)
    cg = 0.5 * cg_ref[...]
    cv = cv_ref[...]

    for w_ref, scr in ((wg_ref, g_scr), (wv_ref, v_scr)):
        w = w_ref[...]
        scr[0:FFN_HALO, :] = jnp.dot(hh_scr[...], w, preferred_element_type=F32) * keep
        scr[FFN_HALO:, :] = jnp.dot(h_scr[...], w, preferred_element_type=F32)

    def conv(scr, c, r):
        base = FFN_HALO - (FFN_CONV_LEN - 1) + r
        out = c[0:1, :] * scr[base:base + FFN_EPI_ROWS, :]
        for t in range(1, FFN_CONV_LEN):
            out = out + c[t:t + 1, :] * scr[base + t:base + t + FFN_EPI_ROWS, :]
        return out

    for r in range(0, TM_UP, FFN_EPI_ROWS):
        half = conv(g_scr, cg, r)
        val = conv(v_scr, cv, r)
        o_ref[r:r + FFN_EPI_ROWS, :] = ((half + half * jnp.tanh(half)) * val).astype(o_ref.dtype)


def _ffn_up(x, nw, w_up, dw, layer):
    m, d = x.shape
    nj = D_FF // TN_UP
    hb = TM_UP // FFN_HALO
    raw = pltpu.VMEM((FFN_HALO + TM_UP, TN_UP), F32)
    return pl.pallas_call(
        _ffn_up_kernel,
        grid=(m // TM_UP, nj),
        in_specs=[
            pl.BlockSpec((TM_UP, d), lambda i, j: (i, 0)),
            pl.BlockSpec((FFN_HALO, d), lambda i, j: (jnp.maximum(i * hb - 1, 0), 0)),
            pl.BlockSpec((1, d), lambda i, j: (0, 0)),
            pl.BlockSpec((None, d, TN_UP), lambda i, j: (layer, 0, j)),
            pl.BlockSpec((None, d, TN_UP), lambda i, j: (layer, 0, j + nj)),
            pl.BlockSpec((None, FFN_CONV_LEN, TN_UP), lambda i, j: (layer, 0, j)),
            pl.BlockSpec((None, FFN_CONV_LEN, TN_UP), lambda i, j: (layer, 0, j + nj)),
        ],
        out_specs=pl.BlockSpec((TM_UP, TN_UP), lambda i, j: (i, j)),
        out_shape=jax.ShapeDtypeStruct((m, D_FF), BF16),
        scratch_shapes=[pltpu.VMEM((TM_UP, d), BF16), pltpu.VMEM((FFN_HALO, d), BF16), raw, raw],
        compiler_params=_cparams("parallel", "arbitrary"),
        name="ffn_up",
    )(x, x, nw, w_up, w_up, dw, dw)


def _norm_kernel(x_ref, w_ref, o_ref):
    o_ref[...] = _rms(x_ref[...], w_ref[...]).astype(o_ref.dtype)


def _norm(x, w, dtype):
    m, d = x.shape
    return pl.pallas_call(
        _norm_kernel,
        grid=(m // TM_NORM,),
        in_specs=[pl.BlockSpec((TM_NORM, d), lambda i: (i, 0)), pl.BlockSpec((1, d), lambda i: (0, 0))],
        out_specs=pl.BlockSpec((TM_NORM, d), lambda i: (i, 0)),
        out_shape=jax.ShapeDtypeStruct((m, d), dtype),
        compiler_params=_cparams("parallel"),
        name="rms_norm",
    )(x, w)


def _ffn_down_kernel(a_ref, w_ref, x_ref, o_ref):
    o_ref[...] = x_ref[...] + jnp.dot(a_ref[...], w_ref[...], preferred_element_type=F32)


def _ffn_down(a, w, x, layer):
    m, d = x.shape
    f = a.shape[1]
    return pl.pallas_call(
        _ffn_down_kernel,
        grid=(m // TM_DN, d // TN_DN),
        in_specs=[
            pl.BlockSpec((TM_DN, f), lambda i, j: (i, 0)),
            pl.BlockSpec((None, f, TN_DN), lambda i, j: (layer, 0, j)),
            pl.BlockSpec((TM_DN, TN_DN), lambda i, j: (i, j)),
        ],
        out_specs=pl.BlockSpec((TM_DN, TN_DN), lambda i, j: (i, j)),
        out_shape=jax.ShapeDtypeStruct((m, d), F32),
        compiler_params=_cparams("parallel", "arbitrary"),
        name="ffn_down",
    )(a, w, x)


def _rope_tables():
    half = HEAD_DIM // 2
    inv = ROPE_THETA ** (-np.arange(half, dtype=np.float64) / half)

    def tab(pos):
        ang = np.asarray(pos, np.float64)[:, None] * inv[None, :]
        cos, sin = np.cos(ang), np.sin(ang)
        return (jnp.asarray(np.concatenate([cos, cos], -1), F32),
                jnp.asarray(np.concatenate([-sin, sin], -1), F32))

    cs, sn = tab(np.arange(SEQ))
    ccs, csn = tab(np.arange(SEQ // CMP_STRIDE) * CMP_STRIDE + CMP_LEN - 1)
    return cs, sn, ccs, csn


def _reorder_kernel(w_ref, o_ref):
    a0 = 4 * MLSTM_WIDTH
    a1 = a0 + 2 * MLSTM_HEADS
    a2 = a1 + (OFF_SMALL - OFF_CA)
    a3 = a2 + 3 * NSA_HEADS
    s1 = OFF_SMALL + (a1 - a0)
    s2 = s1 + (a3 - a2)
    o_ref[:, 0:a0] = w_ref[:, 0:a0].astype(BF16)
    o_ref[:, a0:OFF_SMALL] = w_ref[:, a1:a2].astype(BF16)
    o_ref[:, OFF_SMALL:s1] = w_ref[:, a0:a1].astype(BF16)
    o_ref[:, s1:s2] = w_ref[:, a2:a3].astype(BF16)
    o_ref[:, s2:] = jnp.zeros((o_ref.shape[0], N_PROJ - s2), BF16)


def _reorder_w_in(w):
    depth, d, n = w.shape
    return pl.pallas_call(
        _reorder_kernel,
        grid=(depth, d // TK_REORDER),
        in_specs=[pl.BlockSpec((None, TK_REORDER, n), lambda l, k: (l, k, 0))],
        out_specs=pl.BlockSpec((None, TK_REORDER, N_PROJ), lambda l, k: (l, k, 0)),
        out_shape=jax.ShapeDtypeStruct((depth, d, N_PROJ), BF16),
        compiler_params=_cparams("parallel", "parallel"),
        name="w_in_layout",
    )(w)


def kernel(x, attn_norm_w, w_in, mlstm_i_bias, mlstm_f_bias, mlstm_norm_w, conv_dw_w, conv_dw_b, conv_ln_w, conv_ln_b, nsa_cmp_pe_k, nsa_cmp_w_k, nsa_cmp_pe_v, nsa_cmp_w_v, w_out, ffn_norm_w, w_up, ffn_dw_w, w_down, final_norm_w):
    batch, seq, d = x.shape
    assert seq == SEQ and d == D_MODEL
    depth = w_in.shape[0]
    tabs = _rope_tables()
    w_in_b = _reorder_w_in(w_in)
    w_out_b = w_out.astype(BF16)
    w_up_b = w_up.astype(BF16)
    w_down_b = w_down.astype(BF16)
    w_ck = nsa_cmp_w_k.reshape(depth, CMP_LEN, HEAD_DIM, HEAD_DIM).astype(BF16)
    w_cv = nsa_cmp_w_v.reshape(depth, CMP_LEN, HEAD_DIM, HEAD_DIM).astype(BF16)
    xf = x.reshape(batch * seq, d)
    for l in range(depth):
        proj = _norm_matmul(xf, attn_norm_w[l][None, :], w_in_b, l)
        gate_bias = jnp.concatenate(
            [mlstm_i_bias[l], mlstm_f_bias[l], jnp.zeros((LANE - 2 * MLSTM_HEADS,), F32)])[None, :]
        y_a = _mlstm(proj, gate_bias, mlstm_norm_w[l][None, :], batch)
        y_b = _conformer(proj, conv_dw_w[l], conv_dw_b[l][None, :], conv_ln_w[l][None, :],
                         conv_ln_b[l][None, :], batch)
        prep = _nsa_prep(proj, nsa_cmp_pe_k, w_ck, nsa_cmp_pe_v, w_cv, tabs, batch, l)
        y_c = _nsa(proj, prep, tabs, batch)
        xf = _out_proj(y_a, y_b, y_c, w_out_b, xf, l)
        act = _ffn_up(xf, ffn_norm_w[l][None, :], w_up_b, ffn_dw_w, l)
        xf = _ffn_down(act, w_down_b, xf, l)
    return _norm(xf, final_norm_w[None, :], F32).reshape(batch, seq, d)
```

```python
import functools

import numpy as np
import jax
import jax.numpy as jnp
from jax import lax
from jax.experimental import pallas as pl
from jax.experimental.pallas import tpu as pltpu

F32 = jnp.float32
BF16 = jnp.bfloat16

D_MODEL = 2048
SEQ = 2048
HEAD_DIM = 128
MLSTM_HEADS = 4
MLSTM_WIDTH = MLSTM_HEADS * HEAD_DIM
MLSTM_CHUNK = 64
CONV_WIDTH = 512
CONV_LEN = 31
NSA_HEADS = 8
NSA_KV_HEADS = 2
NSA_HPG = NSA_HEADS // NSA_KV_HEADS
NSA_WIDTH = NSA_HEADS * HEAD_DIM
CMP_LEN = 32
CMP_STRIDE = 16
N_CMP = (SEQ - CMP_LEN) // CMP_STRIDE + 1
SEL_BLOCK = 64
SEL_COUNT = 16
N_SEL = SEQ // SEL_BLOCK
WINDOW = 512
ROPE_THETA = 10000.0
D_FF = 5632
FFN_CONV_LEN = 3
EPS = 1e-6
NEG = -1e30
FORCE = 1e4

OFF_MQ, OFF_MK, OFF_MV, OFF_MO = 0, 512, 1024, 1536
OFF_CA, OFF_CG = 2048, 2560
OFF_NQ = 3072
OFF_NKC, OFF_NVC, OFF_NKS, OFF_NVS, OFF_NKW, OFF_NVW = 4096, 4352, 4608, 4864, 5120, 5376
OFF_SMALL = 5632
N_PROJ = 5760
LANE = 128
GATE_LANE0 = 8

VMEM_LIMIT = 52 * 1024 * 1024

TM_IN, TN_IN = 1024, 1152
TM_OUT, TN_OUT = 2048, 512
TM_UP, TN_UP = 1024, 512
TM_DN, TN_DN = 1024, 512
TM_NORM = 512
TK_REORDER = 256
MLSTM_ROWS = 2 * MLSTM_CHUNK
MLSTM_BATCH = 4
CONV_ROWS = 256
CONV_HALO = 32
SUBLANE = 8
CONV_PHASE_ROWS = CONV_HALO + CONV_ROWS - SUBLANE
CONV_CHUNK = 32
FFN_HALO = 16
FFN_EPI_ROWS = 32
TQ = 256
TK = 256
assert TQ == TK and WINDOW == 2 * TK
LOG2E = 1.4426950408889634


def _cparams(*sem):
    return pltpu.CompilerParams(dimension_semantics=sem, vmem_limit_bytes=VMEM_LIMIT)


def _rms(x, w):
    return x * lax.rsqrt(jnp.mean(x * x, axis=-1, keepdims=True) + EPS) * w


def _norm_matmul_kernel(x_ref, nw_ref, w_ref, o_ref, h_scr):
    @pl.when(pl.program_id(1) == 0)
    def _():
        h_scr[...] = _rms(x_ref[...], nw_ref[...]).astype(BF16)

    o_ref[...] = lax.dot_general(h_scr[...], w_ref[...], (((1,), (1,)), ((), ())),
                                 preferred_element_type=F32)


def _norm_matmul(x, nw, w, layer):
    m, d = x.shape
    n = w.shape[1]
    return pl.pallas_call(
        _norm_matmul_kernel,
        grid=(m // TM_IN, n // TN_IN),
        in_specs=[
            pl.BlockSpec((TM_IN, d), lambda i, j: (i, 0)),
            pl.BlockSpec((1, d), lambda i, j: (0, 0)),
            pl.BlockSpec((None, TN_IN, d), lambda i, j: (layer, j, 0)),
        ],
        out_specs=pl.BlockSpec((TM_IN, TN_IN), lambda i, j: (i, j)),
        out_shape=jax.ShapeDtypeStruct((m, n), F32),
        scratch_shapes=[pltpu.VMEM((TM_IN, d), BF16)],
        compiler_params=_cparams("parallel", "arbitrary"),
        name="in_proj",
    )(x, nw, w)


def _log_sigmoid(x):
    return jnp.minimum(x, 0.0) - jnp.log1p(jnp.exp(-jnp.abs(x)))


def _mlstm_kernel(q_ref, k_ref, v_ref, o_ref, g_ref, gb_ref, nw_ref, y_ref, c_scr, n_scr, m_scr):
    L = MLSTM_CHUNK

    @pl.when(pl.program_id(1) == 0)
    def _():
        c_scr[...] = jnp.zeros_like(c_scr)
        n_scr[...] = jnp.zeros_like(n_scr)
        m_scr[...] = jnp.full_like(m_scr, -jnp.inf)

    row = lax.broadcasted_iota(jnp.int32, (L, L), 0)
    col = lax.broadcasted_iota(jnp.int32, (L, L), 1)
    causal = col <= row
    seqs = range(q_ref.shape[0])
    gates = [g_ref[b] + gb_ref[...] for b in seqs]
    in_chunk = lax.broadcasted_iota(jnp.int32, (MLSTM_ROWS, LANE), 0) & (L - 1)

    def chunk_cumsum(x):
        d = 1
        while d < L:
            x = x + jnp.where(in_chunk >= d, pltpu.roll(x, d, axis=0), 0.0)
            d *= 2
        return x

    b_all = [chunk_cumsum(_log_sigmoid(gates[b])) for b in seqs]
    b_t = [x.T for x in b_all]
    g_t = [x.T for x in gates]
    scale = HEAD_DIM ** -0.5

    pairs = [(b, h) for b in seqs for h in range(MLSTM_HEADS)]
    hsl = [slice(h * HEAD_DIM, (h + 1) * HEAD_DIM) for h in range(MLSTM_HEADS)]
    per_head = lambda f: jnp.stack([f(b, h) for b, h in pairs], axis=0)
    k_all = {(b, h): k_ref[b, :, hsl[h]] * scale for b, h in pairs}
    k_allt = {p: k.T for p, k in k_all.items()}
    nw = per_head(lambda b, h: nw_ref[:, hsl[h]])
    c_st = c_scr[...]
    n_st = n_scr[...]
    m_st = m_scr[...][:, :, 0:1]
    bdot = lambda a, b, ca, cb: lax.dot_general(
        a.astype(BF16), b.astype(BF16), (((ca,), (cb,)), ((0,), (0,))), preferred_element_type=F32)
    for c in range(2):
        rs = slice(c * L, (c + 1) * L)
        qc = per_head(lambda b, h: q_ref[b, rs, hsl[h]])
        kc = per_head(lambda b, h: k_all[b, h][rs])
        kct = per_head(lambda b, h: k_allt[b, h][:, rs])
        vc = per_head(lambda b, h: v_ref[b, rs, hsl[h]])
        b_col = per_head(lambda b, h: b_all[b][rs, 4 + h:5 + h])
        i_col = per_head(lambda b, h: gates[b][rs, h:h + 1])
        b_row = per_head(lambda b, h: b_t[b][4 + h:5 + h, rs])
        i_row = per_head(lambda b, h: g_t[b][h:h + 1, rs])
        dmat = jnp.where(causal[None], b_col - b_row + i_row, -jnp.inf)
        inter = b_col + m_st
        m_t = jnp.maximum(inter, jnp.max(dmat, axis=-1, keepdims=True))
        s = bdot(qc, kc, 2, 2) * jnp.exp(dmat - m_t)
        w_inter = jnp.exp(inter - m_t)
        num = bdot(s, vc, 2, 1) + w_inter * bdot(qc, c_st, 2, 1)
        den = (jnp.sum(s, axis=-1, keepdims=True)
               + w_inter * jnp.sum(qc * n_st, axis=-1, keepdims=True))
        hh = num / jnp.maximum(jnp.abs(den), jnp.exp(-m_t))
        b_last = b_col[:, L - 1:L, :]
        g = b_last - b_col + i_col
        m_new = jnp.maximum(b_last + m_st, jnp.max(g, axis=1, keepdims=True))
        w_k = jnp.exp(g - m_new)
        decay = jnp.exp(b_last + m_st - m_new)
        c_st = decay * c_st + bdot(kct, w_k * vc, 2, 1)
        n_st = decay * n_st + jnp.sum(w_k * kc, axis=1, keepdims=True)
        m_st = m_new
        mu = jnp.mean(hh, axis=-1, keepdims=True)
        hc = hh - mu
        hn = hc * lax.rsqrt(jnp.mean(hc * hc, axis=-1, keepdims=True) + EPS) * nw
        for p, (b, h) in enumerate(pairs):
            y_ref[b, rs, hsl[h]] = (jax.nn.sigmoid(o_ref[b, rs, hsl[h]]) * hn[p]).astype(y_ref.dtype)
    c_scr[...] = c_st
    n_scr[...] = n_st
    m_scr[...] = jnp.broadcast_to(m_st, m_scr.shape)


def _mlstm(proj, gate_bias, norm_w, batch):
    m, n = proj.shape
    proj3 = proj.reshape(batch, SEQ, n)
    nsteps = SEQ // MLSTM_ROWS
    nb = MLSTM_BATCH
    npair = nb * MLSTM_HEADS
    blk = lambda cb: pl.BlockSpec((nb, MLSTM_ROWS, MLSTM_WIDTH), lambda b, c, cb=cb: (b, c, cb))
    y = pl.pallas_call(
        _mlstm_kernel,
        grid=(batch // nb, nsteps),
        in_specs=[
            blk(OFF_MQ // MLSTM_WIDTH), blk(OFF_MK // MLSTM_WIDTH),
            blk(OFF_MV // MLSTM_WIDTH), blk(OFF_MO // MLSTM_WIDTH),
            pl.BlockSpec((nb, MLSTM_ROWS, LANE), lambda b, c: (b, c, OFF_SMALL // LANE)),
            pl.BlockSpec((1, LANE), lambda b, c: (0, 0)),
            pl.BlockSpec((1, MLSTM_WIDTH), lambda b, c: (0, 0)),
        ],
        out_specs=pl.BlockSpec((nb, MLSTM_ROWS, MLSTM_WIDTH), lambda b, c: (b, c, 0)),
        out_shape=jax.ShapeDtypeStruct((batch, SEQ, MLSTM_WIDTH), BF16),
        scratch_shapes=[
            pltpu.VMEM((npair, HEAD_DIM, HEAD_DIM), F32),
            pltpu.VMEM((npair, 1, HEAD_DIM), F32),
            pltpu.VMEM((npair, 1, LANE), F32),
        ],
        compiler_params=_cparams("parallel", "arbitrary"),
        name="mlstm",
    )(proj3, proj3, proj3, proj3, proj3, gate_bias, norm_w)
    return y.reshape(m, MLSTM_WIDTH)


def _conv_kernel(a_ref, g_ref, ah_ref, gh_ref, w_ref, b_ref, lw_ref, lb_ref, y_ref, u_scr, s_scr):
    first = pl.program_id(1) == 0
    uh = ah_ref[...] * jax.nn.sigmoid(gh_ref[...])
    u_scr[0:CONV_HALO, :] = jnp.where(first, 0.0, uh)
    u_scr[CONV_HALO:, :] = a_ref[...] * jax.nn.sigmoid(g_ref[...])
    for ph in range(1, SUBLANE):
        s_scr[ph] = u_scr[ph:ph + CONV_PHASE_ROWS, :]
    w = w_ref[...]
    bias = b_ref[...]
    lw = lw_ref[...]
    lb = lb_ref[...]
    base = CONV_HALO - (CONV_LEN - 1)
    for r in range(0, CONV_ROWS, CONV_CHUNK):
        acc = jnp.zeros((CONV_CHUNK, CONV_WIDTH), F32)
        for j in range(CONV_LEN):
            ph, al = (base + j) % SUBLANE, (base + j) // SUBLANE * SUBLANE
            src = u_scr[al + r:al + r + CONV_CHUNK, :] if ph == 0 else s_scr[ph, al + r:al + r + CONV_CHUNK, :]
            acc = acc + w[j:j + 1, :] * src
        u = acc + bias
        mu = jnp.mean(u, axis=-1, keepdims=True)
        uc = u - mu
        y = uc * lax.rsqrt(jnp.mean(uc * uc, axis=-1, keepdims=True) + EPS) * lw + lb
        y_ref[r:r + CONV_CHUNK, :] = (y * jax.nn.sigmoid(y)).astype(y_ref.dtype)


def _conformer(proj, dw_w, dw_b, ln_w, ln_b, batch):
    m = proj.shape[0]
    nt = SEQ // CONV_ROWS
    hb = CONV_ROWS // CONV_HALO
    main = lambda cb: pl.BlockSpec((CONV_ROWS, CONV_WIDTH), lambda b, i, cb=cb: (b * nt + i, cb))
    halo = lambda cb: pl.BlockSpec(
        (CONV_HALO, CONV_WIDTH), lambda b, i, cb=cb: (jnp.maximum((b * nt + i) * hb - 1, 0), cb))
    vec = pl.BlockSpec((1, CONV_WIDTH), lambda b, i: (0, 0))
    return pl.pallas_call(
        _conv_kernel,
        grid=(batch, nt),
        in_specs=[
            main(OFF_CA // CONV_WIDTH), main(OFF_CG // CONV_WIDTH),
            halo(OFF_CA // CONV_WIDTH), halo(OFF_CG // CONV_WIDTH),
            pl.BlockSpec((CONV_LEN, CONV_WIDTH), lambda b, i: (0, 0)),
            vec, vec, vec,
        ],
        out_specs=pl.BlockSpec((CONV_ROWS, CONV_WIDTH), lambda b, i: (b * nt + i, 0)),
        out_shape=jax.ShapeDtypeStruct((m, CONV_WIDTH), BF16),
        scratch_shapes=[pltpu.VMEM((CONV_HALO + CONV_ROWS, CONV_WIDTH), F32),
                        pltpu.VMEM((SUBLANE, CONV_PHASE_ROWS, CONV_WIDTH), F32)],
        compiler_params=_cparams("parallel", "arbitrary"),
        name="conformer_conv",
    )(proj, proj, proj, proj, dw_w, dw_b, ln_w, ln_b)


def _rope(x, cs, sn):
    return x * cs + pltpu.roll(x, HEAD_DIM // 2, axis=1) * sn


def _compress(x_ref, pe_ref, w_ref):
    nrow = SEQ // CMP_STRIDE
    lo = jnp.zeros((nrow, HEAD_DIM), F32)
    hi = jnp.zeros((nrow, HEAD_DIM), F32)
    for l in range(CMP_STRIDE):
        xl = x_ref[pl.ds(l, nrow, stride=CMP_STRIDE), :]
        lo = lo + jnp.dot((xl + pe_ref[l:l + 1, :]).astype(BF16), w_ref[l],
                          preferred_element_type=F32)
        hi = hi + jnp.dot((xl + pe_ref[CMP_STRIDE + l:CMP_STRIDE + l + 1, :]).astype(BF16),
                          w_ref[CMP_STRIDE + l], preferred_element_type=F32)
    return lo + pltpu.roll(hi, nrow - 1, axis=0)


def _nsa_prep_kernel(kc_ref, vc_ref, ks_ref, vs_ref, kw_ref, vw_ref, pek_ref, wk_ref, pev_ref, wv_ref,
                     cs_ref, sn_ref, ccs_ref, csn_ref,
                     kco_ref, vco_ref, kso_ref, vso_ref, kwo_ref, vwo_ref):
    kco_ref[...] = _rope(_compress(kc_ref, pek_ref, wk_ref), ccs_ref[...], csn_ref[...]).astype(BF16)
    vco_ref[...] = _compress(vc_ref, pev_ref, wv_ref).T.astype(BF16)
    cs = cs_ref[...]
    sn = sn_ref[...]
    kso_ref[:, 0:HEAD_DIM] = _rope(ks_ref[...], cs, sn).astype(BF16)
    key = lax.broadcasted_iota(jnp.int32, (SEQ, LANE), 0)
    blk = lax.broadcasted_iota(jnp.int32, (SEQ, LANE), 1)
    kso_ref[:, HEAD_DIM:] = jnp.where(key // SEL_BLOCK == blk, 1.0, 0.0).astype(BF16)
    kwo_ref[...] = _rope(kw_ref[...], cs, sn).astype(BF16)
    for kt in range(SEQ // TK):
        vso_ref[kt] = vs_ref[kt * TK:(kt + 1) * TK, :].T.astype(BF16)
        vwo_ref[kt] = vw_ref[kt * TK:(kt + 1) * TK, :].T.astype(BF16)


def _nsa_prep(proj, pe_k, w_k, pe_v, w_v, tabs, batch, layer):
    cs, sn, ccs, csn = tabs
    G = NSA_KV_HEADS
    ncr = SEQ // CMP_STRIDE
    col = lambda off: pl.BlockSpec((SEQ, HEAD_DIM), lambda b, g, off=off: (b, off // HEAD_DIM + g))
    full2 = lambda r: pl.BlockSpec((r, HEAD_DIM), lambda b, g: (0, 0))
    pespec = pl.BlockSpec((None, CMP_LEN, HEAD_DIM), lambda b, g: (layer, 0, 0))
    wspec = pl.BlockSpec((None, CMP_LEN, HEAD_DIM, HEAD_DIM), lambda b, g: (layer, 0, 0, 0))
    ocmp = pl.BlockSpec((None, None, ncr, HEAD_DIM), lambda b, g: (b, g, 0, 0))
    oseq = pl.BlockSpec((None, None, SEQ, HEAD_DIM), lambda b, g: (b, g, 0, 0))
    oext = pl.BlockSpec((None, None, SEQ, HEAD_DIM + LANE), lambda b, g: (b, g, 0, 0))
    cmp_shape = jax.ShapeDtypeStruct((batch, G, ncr, HEAD_DIM), BF16)
    seq_shape = jax.ShapeDtypeStruct((batch, G, SEQ, HEAD_DIM), BF16)
    ext_shape = jax.ShapeDtypeStruct((batch, G, SEQ, HEAD_DIM + LANE), BF16)
    oval = pl.BlockSpec((None, None, SEQ // TK, HEAD_DIM, TK), lambda b, g: (b, g, 0, 0, 0))
    val_shape = jax.ShapeDtypeStruct((batch, G, SEQ // TK, HEAD_DIM, TK), BF16)
    return pl.pallas_call(
        _nsa_prep_kernel,
        grid=(batch, G),
        in_specs=[
            col(OFF_NKC), col(OFF_NVC), col(OFF_NKS), col(OFF_NVS), col(OFF_NKW), col(OFF_NVW),
            pespec, wspec, pespec, wspec,
            full2(SEQ), full2(SEQ), full2(ncr), full2(ncr),
        ],
        out_specs=[ocmp, ocmp, oext, oval, oseq, oval],
        out_shape=[cmp_shape, cmp_shape, ext_shape, val_shape, seq_shape, val_shape],
        compiler_params=_cparams("parallel", "arbitrary"),
        name="nsa_prep",
    )(proj, proj, proj, proj, proj, proj, pe_k, w_k, pe_v, w_v, cs, sn, ccs, csn)


def _attend(k, v, q, bias, carry, acc_ref):
    m, l = carry
    s = jnp.dot(k, q, preferred_element_type=F32)
    if bias is not None:
        s = s + bias
    m_new = jnp.maximum(m, jnp.max(s, axis=0, keepdims=True))
    alpha = jnp.exp2(m - m_new)
    p = jnp.exp2(s - m_new)
    l = alpha * l + jnp.sum(p, axis=0, keepdims=True)
    acc_ref[...] = alpha * acc_ref[...] + jnp.dot(v, p.astype(BF16), preferred_element_type=F32)
    return m_new, l


def _nsa_kernel(q_ref, sm_ref, cs_ref, sn_ref, kc_ref, vc_ref, ks_ref, vs_ref, kw_ref, vw_ref, y_ref,
                as_scr, aw_scr):
    g = pl.program_id(1)
    qi = pl.program_id(2)
    q0 = qi * TQ
    R = NSA_HPG * TQ
    scale = HEAD_DIM ** -0.5 * LOG2E
    cs = cs_ref[...]
    sn = sn_ref[...]
    q = jnp.concatenate(
        [(_rope(q_ref[:, h * HEAD_DIM:(h + 1) * HEAD_DIM], cs, sn) * scale).T.astype(BF16)
         for h in range(NSA_HPG)], axis=1)
    tok = lax.broadcasted_iota(jnp.int32, (LANE, R), 1) & (TQ - 1)
    rowi = lax.broadcasted_iota(jnp.int32, (LANE, R), 0)

    ncr = SEQ // CMP_STRIDE
    valid_c = (rowi * CMP_STRIDE + (CMP_LEN - 1)) <= q0 + tok
    s_c = jnp.dot(kc_ref[...], q, preferred_element_type=F32) + jnp.where(valid_c, 0.0, NEG)
    m_c = jnp.max(s_c, axis=0, keepdims=True)
    e_c = jnp.where(valid_c, jnp.exp2(s_c - m_c), 0.0)
    p_c = e_c * (1.0 / jnp.maximum(jnp.sum(e_c, axis=0, keepdims=True), 1e-30))
    o_cmp = jnp.dot(vc_ref[...], p_c.astype(BF16), preferred_element_type=F32)
    p_sum = p_c[:, 0:TQ] + p_c[:, TQ:2 * TQ] + p_c[:, 2 * TQ:3 * TQ] + p_c[:, 3 * TQ:4 * TQ]

    jj = lax.broadcasted_iota(jnp.int32, (LANE, ncr), 0)
    nn = lax.broadcasted_iota(jnp.int32, (LANE, ncr), 1)
    ov_t = ((nn * CMP_STRIDE <= jj * SEL_BLOCK + (SEL_BLOCK - 1))
            & (nn * CMP_STRIDE + (CMP_LEN - 1) >= jj * SEL_BLOCK)
            & (jj < N_SEL) & (nn < N_CMP)).astype(F32)
    imp_t = jnp.dot(ov_t, p_sum, precision=lax.Precision.HIGHEST,
                    preferred_element_type=F32)[0:N_SEL]
    j_i = lax.broadcasted_iota(jnp.int32, (N_SEL, TQ), 0)
    t_i = q0 + lax.broadcasted_iota(jnp.int32, (N_SEL, TQ), 1)
    qblk = t_i // SEL_BLOCK
    forced = (j_i == 0) | (j_i == qblk) | (j_i == qblk - 1)
    score = jnp.where(forced, FORCE, jnp.where(j_i * SEL_BLOCK <= t_i, imp_t, -FORCE))
    rank = jnp.zeros((N_SEL, TQ), jnp.int32)
    for jp in range(N_SEL):
        row = score[jp:jp + 1, :]
        beats = (row > score) | ((row == score) & (j_i > jp))
        rank = rank + beats.astype(jnp.int32)
    sel_bias_t = jnp.concatenate(
        [jnp.where(rank < SEL_COUNT, 0.0, NEG), jnp.zeros((LANE - N_SEL, TQ), F32)], axis=0)
    qx = jnp.concatenate([q, jnp.concatenate([sel_bias_t.astype(BF16)] * NSA_HPG, axis=1)], axis=0)

    keyi = lax.broadcasted_iota(jnp.int32, (TK, R), 0)
    toki = lax.broadcasted_iota(jnp.int32, (TK, R), 1) & (TQ - 1)
    causal = jnp.where(keyi <= toki, 0.0, NEG)

    init = (jnp.full((1, R), NEG, F32), jnp.zeros((1, R), F32))
    as_scr[...] = jnp.zeros_like(as_scr)
    aw_scr[...] = jnp.zeros_like(aw_scr)
    diag = pl.multiple_of(q0, TK)

    def sel_step(kt, carry):
        k0 = pl.multiple_of(kt * TK, TK)
        return _attend(ks_ref[pl.ds(k0, TK), :], vs_ref[kt], qx, None, carry, as_scr)

    carry = lax.fori_loop(0, qi, sel_step, init)
    _, l_s = _attend(ks_ref[pl.ds(diag, TK), :], vs_ref[qi], qx, causal, carry, as_scr)

    carry = _attend(kw_ref[pl.ds(diag, TK), :], vw_ref[qi], q, causal, init, aw_scr)
    t1 = jnp.maximum(qi - 1, 0)
    carry = _attend(kw_ref[pl.ds(pl.multiple_of(t1 * TK, TK), TK), :], vw_ref[t1], q,
                    jnp.where(qi >= 1, jnp.zeros((TK, R), F32), NEG), carry, aw_scr)
    t2 = jnp.maximum(qi - 2, 0)
    _, l_w = _attend(kw_ref[pl.ds(pl.multiple_of(t2 * TK, TK), TK), :], vw_ref[t2], q,
                     jnp.where(qi >= 2, jnp.where(keyi > toki, 0.0, NEG), NEG), carry, aw_scr)

    gl = jax.nn.sigmoid(sm_ref[...]).T
    o_slc = as_scr[...] * (1.0 / l_s)
    o_win = aw_scr[...] * (1.0 / l_w)
    for h in range(NSA_HPG):
        cs_ = slice(h * TQ, (h + 1) * TQ)
        gate = []
        for c in range(3):
            r0 = GATE_LANE0 + h * 3 + c
            r1 = r0 + NSA_HPG * 3
            gate.append(jnp.where(g == 0, gl[r0:r0 + 1, :], gl[r1:r1 + 1, :]))
        y = gate[0] * o_cmp[:, cs_] + gate[1] * o_slc[:, cs_] + gate[2] * o_win[:, cs_]
        y_ref[:, h * HEAD_DIM:(h + 1) * HEAD_DIM] = y.T.astype(y_ref.dtype)


def _nsa(proj, prep, tabs, batch):
    kc, vc, ks, vs, kw, vw = prep
    cs, sn = tabs[0], tabs[1]
    m = proj.shape[0]
    G = NSA_KV_HEADS
    nq = SEQ // TQ
    ncr = SEQ // CMP_STRIDE
    gw = NSA_HPG * HEAD_DIM
    cmp_spec = pl.BlockSpec((None, None, ncr, HEAD_DIM), lambda b, g, i: (b, g, 0, 0))
    seq_spec = pl.BlockSpec((None, None, SEQ, HEAD_DIM), lambda b, g, i: (b, g, 0, 0))
    ext_spec = pl.BlockSpec((None, None, SEQ, HEAD_DIM + LANE), lambda b, g, i: (b, g, 0, 0))
    val_spec = pl.BlockSpec((None, None, SEQ // TK, HEAD_DIM, TK), lambda b, g, i: (b, g, 0, 0, 0))
    tab_spec = pl.BlockSpec((TQ, HEAD_DIM), lambda b, g, i: (i, 0))
    return pl.pallas_call(
        _nsa_kernel,
        grid=(batch, G, nq),
        in_specs=[
            pl.BlockSpec((TQ, gw), lambda b, g, i: (b * nq + i, OFF_NQ // gw + g)),
            pl.BlockSpec((TQ, LANE), lambda b, g, i: (b * nq + i, OFF_SMALL // LANE)),
            tab_spec, tab_spec, cmp_spec, cmp_spec, ext_spec, val_spec, seq_spec, val_spec,
        ],
        out_specs=pl.BlockSpec((TQ, gw), lambda b, g, i: (b * nq + i, g)),
        out_shape=jax.ShapeDtypeStruct((m, NSA_WIDTH), BF16),
        scratch_shapes=[pltpu.VMEM((HEAD_DIM, NSA_HPG * TQ), F32), pltpu.VMEM((HEAD_DIM, NSA_HPG * TQ), F32)],
        compiler_params=_cparams("parallel", "parallel", "arbitrary"),
        name="nsa_attn",
    )(proj, proj, cs, sn, kc, vc, ks, vs, kw, vw)


def _out_proj_kernel(ya_ref, yb_ref, yc_ref, wa_ref, wb_ref, wc_ref, x_ref, o_ref):
    acc = jnp.dot(ya_ref[...], wa_ref[...], preferred_element_type=F32)
    acc = acc + jnp.dot(yb_ref[...], wb_ref[...], preferred_element_type=F32)
    acc = acc + jnp.dot(yc_ref[...], wc_ref[...], preferred_element_type=F32)
    o_ref[...] = x_ref[...] + acc


def _out_proj(ya, yb, yc, w, x, layer):
    m, d = x.shape
    wa, wb, wc = ya.shape[1], yb.shape[1], yc.shape[1]
    return pl.pallas_call(
        _out_proj_kernel,
        grid=(m // TM_OUT, d // TN_OUT),
        in_specs=[
            pl.BlockSpec((TM_OUT, wa), lambda i, j: (i, 0)),
            pl.BlockSpec((TM_OUT, wb), lambda i, j: (i, 0)),
            pl.BlockSpec((TM_OUT, wc), lambda i, j: (i, 0)),
            pl.BlockSpec((None, wa, TN_OUT), lambda i, j: (layer, 0, j)),
            pl.BlockSpec((None, wb, TN_OUT), lambda i, j: (layer, wa // wb, j)),
            pl.BlockSpec((None, wc, TN_OUT), lambda i, j: (layer, (wa + wb) // wc, j)),
            pl.BlockSpec((TM_OUT, TN_OUT), lambda i, j: (i, j)),
        ],
        out_specs=pl.BlockSpec((TM_OUT, TN_OUT), lambda i, j: (i, j)),
        out_shape=jax.ShapeDtypeStruct((m, d), F32),
        compiler_params=_cparams("parallel", "arbitrary"),
        name="out_proj",
    )(ya, yb, yc, w, w, w, x)


def _ffn_up_kernel(x_ref, xh_ref, nw_ref, wg_ref, wv_ref, cg_ref, cv_ref, o_ref,
                   h_scr, hh_scr, g_scr, v_scr):
    i = pl.program_id(0)

    @pl.when(pl.program_id(1) == 0)
    def _():
        h_scr[...] = _rms(x_ref[...], nw_ref[...]).astype(BF16)
        hh_scr[...] = _rms(xh_ref[...], nw_ref[...]).astype(BF16)

    keep = jnp.where((i * TM_UP) % SEQ == 0, 0.0, 1.0)
    cg = 0.5 * cg_ref[...]
    cv = cv_ref[...]

    for w_ref, scr in ((wg_ref, g_scr), (wv_ref, v_scr)):
        w = w_ref[...]
        scr[0:FFN_HALO, :] = jnp.dot(hh_scr[...], w, preferred_element_type=F32) * keep
        scr[FFN_HALO:, :] = jnp.dot(h_scr[...], w, preferred_element_type=F32)

    def conv(scr, c, r):
        base = FFN_HALO - (FFN_CONV_LEN - 1) + r
        out = c[0:1, :] * scr[base:base + FFN_EPI_ROWS, :]
        for t in range(1, FFN_CONV_LEN):
            out = out + c[t:t + 1, :] * scr[base + t:base + t + FFN_EPI_ROWS, :]
        return out

    for r in range(0, TM_UP, FFN_EPI_ROWS):
        half = conv(g_scr, cg, r)
        val = conv(v_scr, cv, r)
        o_ref[r:r + FFN_EPI_ROWS, :] = ((half + half * jnp.tanh(half)) * val).astype(o_ref.dtype)


def _ffn_up(x, nw, w_up, dw, layer):
    m, d = x.shape
    nj = D_FF // TN_UP
    hb = TM_UP // FFN_HALO
    raw = pltpu.VMEM((FFN_HALO + TM_UP, TN_UP), F32)
    return pl.pallas_call(
        _ffn_up_kernel,
        grid=(m // TM_UP, nj),
        in_specs=[
            pl.BlockSpec((TM_UP, d), lambda i, j: (i, 0)),
            pl.BlockSpec((FFN_HALO, d), lambda i, j: (jnp.maximum(i * hb - 1, 0), 0)),
            pl.BlockSpec((1, d), lambda i, j: (0, 0)),
            pl.BlockSpec((None, d, TN_UP), lambda i, j: (layer, 0, j)),
            pl.BlockSpec((None, d, TN_UP), lambda i, j: (layer, 0, j + nj)),
            pl.BlockSpec((None, FFN_CONV_LEN, TN_UP), lambda i, j: (layer, 0, j)),
            pl.BlockSpec((None, FFN_CONV_LEN, TN_UP), lambda i, j: (layer, 0, j + nj)),
        ],
        out_specs=pl.BlockSpec((TM_UP, TN_UP), lambda i, j: (i, j)),
        out_shape=jax.ShapeDtypeStruct((m, D_FF), BF16),
        scratch_shapes=[pltpu.VMEM((TM_UP, d), BF16), pltpu.VMEM((FFN_HALO, d), BF16), raw, raw],
        compiler_params=_cparams("parallel", "arbitrary"),
        name="ffn_up",
    )(x, x, nw, w_up, w_up, dw, dw)


def _norm_kernel(x_ref, w_ref, o_ref):
    o_ref[...] = _rms(x_ref[...], w_ref[...]).astype(o_ref.dtype)


def _norm(x, w, dtype):
    m, d = x.shape
    return pl.pallas_call(
        _norm_kernel,
        grid=(m // TM_NORM,),
        in_specs=[pl.BlockSpec((TM_NORM, d), lambda i: (i, 0)), pl.BlockSpec((1, d), lambda i: (0, 0))],
        out_specs=pl.BlockSpec((TM_NORM, d), lambda i: (i, 0)),
        out_shape=jax.ShapeDtypeStruct((m, d), dtype),
        compiler_params=_cparams("parallel"),
        name="rms_norm",
    )(x, w)


def _ffn_down_kernel(a_ref, w_ref, x_ref, o_ref):
    o_ref[...] = x_ref[...] + jnp.dot(a_ref[...], w_ref[...], preferred_element_type=F32)


def _ffn_down(a, w, x, layer):
    m, d = x.shape
    f = a.shape[1]
    return pl.pallas_call(
        _ffn_down_kernel,
        grid=(m // TM_DN, d // TN_DN),
        in_specs=[
            pl.BlockSpec((TM_DN, f), lambda i, j: (i, 0)),
            pl.BlockSpec((None, f, TN_DN), lambda i, j: (layer, 0, j)),
            pl.BlockSpec((TM_DN, TN_DN), lambda i, j: (i, j)),
        ],
        out_specs=pl.BlockSpec((TM_DN, TN_DN), lambda i, j: (i, j)),
        out_shape=jax.ShapeDtypeStruct((m, d), F32),
        compiler_params=_cparams("parallel", "arbitrary"),
        name="ffn_down",
    )(a, w, x)


def _rope_tables():
    half = HEAD_DIM // 2
    inv = ROPE_THETA ** (-np.arange(half, dtype=np.float64) / half)

    def tab(pos):
        ang = np.asarray(pos, np.float64)[:, None] * inv[None, :]
        cos, sin = np.cos(ang), np.sin(ang)
        return (jnp.asarray(np.concatenate([cos, cos], -1), F32),
                jnp.asarray(np.concatenate([-sin, sin], -1), F32))

    cs, sn = tab(np.arange(SEQ))
    ccs, csn = tab(np.arange(SEQ // CMP_STRIDE) * CMP_STRIDE + CMP_LEN - 1)
    return cs, sn, ccs, csn


def _reorder_kernel(w_ref, o_ref):
    a0 = 4 * MLSTM_WIDTH
    a1 = a0 + 2 * MLSTM_HEADS
    a2 = a1 + (OFF_SMALL - OFF_CA)
    a3 = a2 + 3 * NSA_HEADS
    cols = o_ref.shape[1]
    o_ref[0:a0, :] = w_ref[0:a0, :].astype(BF16)
    o_ref[a0:OFF_SMALL, :] = w_ref[a1:a2, :].astype(BF16)
    small = jnp.concatenate(
        [w_ref[a0:a1, :], w_ref[a2:a3, :], jnp.zeros((N_PROJ - OFF_SMALL - (a1 - a0) - (a3 - a2), cols), F32)],
        axis=0)
    o_ref[OFF_SMALL:, :] = small.astype(BF16)


def _reorder_w_in(w):
    wt = jnp.swapaxes(w, 1, 2)
    depth, n, d = wt.shape
    return pl.pallas_call(
        _reorder_kernel,
        grid=(depth, d // TK_REORDER),
        in_specs=[pl.BlockSpec((None, n, TK_REORDER), lambda l, k: (l, 0, k))],
        out_specs=pl.BlockSpec((None, N_PROJ, TK_REORDER), lambda l, k: (l, 0, k)),
        out_shape=jax.ShapeDtypeStruct((depth, N_PROJ, d), BF16),
        compiler_params=_cparams("parallel", "parallel"),
        name="w_in_layout",
    )(wt)


def kernel(x, attn_norm_w, w_in, mlstm_i_bias, mlstm_f_bias, mlstm_norm_w, conv_dw_w, conv_dw_b, conv_ln_w, conv_ln_b, nsa_cmp_pe_k, nsa_cmp_w_k, nsa_cmp_pe_v, nsa_cmp_w_v, w_out, ffn_norm_w, w_up, ffn_dw_w, w_down, final_norm_w):
    batch, seq, d = x.shape
    assert seq == SEQ and d == D_MODEL
    depth = w_in.shape[0]
    tabs = _rope_tables()
    w_in_b = _reorder_w_in(w_in)
    w_out_b = w_out.astype(BF16)
    w_up_b = w_up.astype(BF16)
    w_down_b = w_down.astype(BF16)
    w_ck = nsa_cmp_w_k.reshape(depth, CMP_LEN, HEAD_DIM, HEAD_DIM).astype(BF16)
    w_cv = nsa_cmp_w_v.reshape(depth, CMP_LEN, HEAD_DIM, HEAD_DIM).astype(BF16)
    xf = x.reshape(batch * seq, d)
    for l in range(depth):
        proj = _norm_matmul(xf, attn_norm_w[l][None, :], w_in_b, l)
        gate_bias = jnp.concatenate(
            [mlstm_i_bias[l], mlstm_f_bias[l], jnp.zeros((LANE - 2 * MLSTM_HEADS,), F32)])[None, :]
        y_a = _mlstm(proj, gate_bias, mlstm_norm_w[l][None, :], batch)
        y_b = _conformer(proj, conv_dw_w[l], conv_dw_b[l][None, :], conv_ln_w[l][None, :],
                         conv_ln_b[l][None, :], batch)
        prep = _nsa_prep(proj, nsa_cmp_pe_k, w_ck, nsa_cmp_pe_v, w_cv, tabs, batch, l)
        y_c = _nsa(proj, prep, tabs, batch)
        xf = _out_proj(y_a, y_b, y_c, w_out_b, xf, l)
        act = _ffn_up(xf, ffn_norm_w[l][None, :], w_up_b, ffn_dw_w, l)
        xf = _ffn_down(act, w_down_b, xf, l)
    return _norm(xf, final_norm_w[None, :], F32).reshape(batch, seq, d)
```

```python
import functools

import numpy as np
import jax
import jax.numpy as jnp
from jax import lax
from jax.experimental import pallas as pl
from jax.experimental.pallas import tpu as pltpu

F32 = jnp.float32
BF16 = jnp.bfloat16

D_MODEL = 2048
SEQ = 2048
HEAD_DIM = 128
MLSTM_HEADS = 4
MLSTM_WIDTH = MLSTM_HEADS * HEAD_DIM
MLSTM_CHUNK = 64
CONV_WIDTH = 512
CONV_LEN = 31
NSA_HEADS = 8
NSA_KV_HEADS = 2
NSA_HPG = NSA_HEADS // NSA_KV_HEADS
NSA_WIDTH = NSA_HEADS * HEAD_DIM
CMP_LEN = 32
CMP_STRIDE = 16
N_CMP = (SEQ - CMP_LEN) // CMP_STRIDE + 1
SEL_BLOCK = 64
SEL_COUNT = 16
N_SEL = SEQ // SEL_BLOCK
WINDOW = 512
ROPE_THETA = 10000.0
D_FF = 5632
FFN_CONV_LEN = 3
EPS = 1e-6
NEG = -1e30
FORCE = 1e4

OFF_MQ, OFF_MK, OFF_MV, OFF_MO = 0, 512, 1024, 1536
OFF_CA, OFF_CG = 2048, 2560
OFF_NQ = 3072
OFF_NKC, OFF_NVC, OFF_NKS, OFF_NVS, OFF_NKW, OFF_NVW = 4096, 4352, 4608, 4864, 5120, 5376
OFF_SMALL = 5632
N_PROJ = 5760
LANE = 128
GATE_LANE0 = 8

VMEM_LIMIT = 52 * 1024 * 1024

TM_IN, TN_IN = 1024, 1152
TM_OUT, TN_OUT = 2048, 512
TM_UP, TN_UP = 1024, 512
TM_DN, TN_DN = 1024, 512
TM_NORM = 512
TK_REORDER = 256
MLSTM_ROWS = 2 * MLSTM_CHUNK
MLSTM_BATCH = 4
CONV_ROWS = 512
CONV_HALO = 32
SUBLANE = 8
CONV_PHASE_ROWS = CONV_HALO + CONV_ROWS - SUBLANE
CONV_CHUNK = 32
FFN_HALO = 16
FFN_EPI_ROWS = 32
TQ = 256
TK = 256
assert TQ == TK and WINDOW == 2 * TK
LOG2E = 1.4426950408889634


def _cparams(*sem):
    return pltpu.CompilerParams(dimension_semantics=sem, vmem_limit_bytes=VMEM_LIMIT)


def _rms(x, w):
    return x * lax.rsqrt(jnp.mean(x * x, axis=-1, keepdims=True) + EPS) * w


def _norm_matmul_kernel(x_ref, nw_ref, w_ref, o_ref, h_scr):
    @pl.when(pl.program_id(1) == 0)
    def _():
        h_scr[...] = _rms(x_ref[...], nw_ref[...]).astype(BF16)

    o_ref[...] = lax.dot_general(h_scr[...], w_ref[...], (((1,), (1,)), ((), ())),
                                 preferred_element_type=F32)


def _norm_matmul(x, nw, w, layer):
    m, d = x.shape
    n = w.shape[1]
    return pl.pallas_call(
        _norm_matmul_kernel,
        grid=(m // TM_IN, n // TN_IN),
        in_specs=[
            pl.BlockSpec((TM_IN, d), lambda i, j: (i, 0)),
            pl.BlockSpec((1, d), lambda i, j: (0, 0)),
            pl.BlockSpec((None, TN_IN, d), lambda i, j: (layer, j, 0)),
        ],
        out_specs=pl.BlockSpec((TM_IN, TN_IN), lambda i, j: (i, j)),
        out_shape=jax.ShapeDtypeStruct((m, n), F32),
        scratch_shapes=[pltpu.VMEM((TM_IN, d), BF16)],
        compiler_params=_cparams("parallel", "arbitrary"),
        name="in_proj",
    )(x, nw, w)


def _log_sigmoid(x):
    return jnp.minimum(x, 0.0) - jnp.log1p(jnp.exp(-jnp.abs(x)))


def _mlstm_kernel(q_ref, k_ref, v_ref, o_ref, g_ref, gb_ref, nw_ref, y_ref, c_scr, n_scr, m_scr):
    L = MLSTM_CHUNK

    @pl.when(pl.program_id(1) == 0)
    def _():
        c_scr[...] = jnp.zeros_like(c_scr)
        n_scr[...] = jnp.zeros_like(n_scr)
        m_scr[...] = jnp.full_like(m_scr, -jnp.inf)

    row = lax.broadcasted_iota(jnp.int32, (L, L), 0)
    col = lax.broadcasted_iota(jnp.int32, (L, L), 1)
    causal = col <= row
    seqs = range(q_ref.shape[0])
    gates = [g_ref[b] + gb_ref[...] for b in seqs]
    in_chunk = lax.broadcasted_iota(jnp.int32, (MLSTM_ROWS, LANE), 0) & (L - 1)

    def chunk_cumsum(x):
        d = 1
        while d < L:
            x = x + jnp.where(in_chunk >= d, pltpu.roll(x, d, axis=0), 0.0)
            d *= 2
        return x

    b_all = [chunk_cumsum(_log_sigmoid(gates[b])) for b in seqs]
    b_t = [x.T for x in b_all]
    g_t = [x.T for x in gates]
    scale = HEAD_DIM ** -0.5

    pairs = [(b, h) for b in seqs for h in range(MLSTM_HEADS)]
    hsl = [slice(h * HEAD_DIM, (h + 1) * HEAD_DIM) for h in range(MLSTM_HEADS)]
    per_head = lambda f: jnp.stack([f(b, h) for b, h in pairs], axis=0)
    k_all = {(b, h): k_ref[b, :, hsl[h]] * scale for b, h in pairs}
    k_allt = {p: k.T for p, k in k_all.items()}
    nw = per_head(lambda b, h: nw_ref[:, hsl[h]])
    c_st = c_scr[...]
    n_st = n_scr[...]
    m_st = m_scr[...][:, :, 0:1]
    bdot = lambda a, b, ca, cb: lax.dot_general(
        a.astype(BF16), b.astype(BF16), (((ca,), (cb,)), ((0,), (0,))), preferred_element_type=F32)
    for c in range(2):
        rs = slice(c * L, (c + 1) * L)
        qc = per_head(lambda b, h: q_ref[b, rs, hsl[h]])
        kc = per_head(lambda b, h: k_all[b, h][rs])
        kct = per_head(lambda b, h: k_allt[b, h][:, rs])
        vc = per_head(lambda b, h: v_ref[b, rs, hsl[h]])
        b_col = per_head(lambda b, h: b_all[b][rs, 4 + h:5 + h])
        i_col = per_head(lambda b, h: gates[b][rs, h:h + 1])
        b_row = per_head(lambda b, h: b_t[b][4 + h:5 + h, rs])
        i_row = per_head(lambda b, h: g_t[b][h:h + 1, rs])
        dmat = jnp.where(causal[None], b_col - b_row + i_row, -jnp.inf)
        inter = b_col + m_st
        m_t = jnp.maximum(inter, jnp.max(dmat, axis=-1, keepdims=True))
        s = bdot(qc, kc, 2, 2) * jnp.exp(dmat - m_t)
        w_inter = jnp.exp(inter - m_t)
        num = bdot(s, vc, 2, 1) + w_inter * bdot(qc, c_st, 2, 1)
        den = (jnp.sum(s, axis=-1, keepdims=True)
               + w_inter * jnp.sum(qc * n_st, axis=-1, keepdims=True))
        hh = num / jnp.maximum(jnp.abs(den), jnp.exp(-m_t))
        b_last = b_col[:, L - 1:L, :]
        g = b_last - b_col + i_col
        m_new = jnp.maximum(b_last + m_st, jnp.max(g, axis=1, keepdims=True))
        w_k = jnp.exp(g - m_new)
        decay = jnp.exp(b_last + m_st - m_new)
        c_st = decay * c_st + bdot(kct, w_k * vc, 2, 1)
        n_st = decay * n_st + jnp.sum(w_k * kc, axis=1, keepdims=True)
        m_st = m_new
        mu = jnp.mean(hh, axis=-1, keepdims=True)
        hc = hh - mu
        hn = hc * lax.rsqrt(jnp.mean(hc * hc, axis=-1, keepdims=True) + EPS) * nw
        for p, (b, h) in enumerate(pairs):
            y_ref[b, rs, hsl[h]] = (jax.nn.sigmoid(o_ref[b, rs, hsl[h]]) * hn[p]).astype(y_ref.dtype)
    c_scr[...] = c_st
    n_scr[...] = n_st
    m_scr[...] = jnp.broadcast_to(m_st, m_scr.shape)


def _mlstm(proj, gate_bias, norm_w, batch):
    m, n = proj.shape
    proj3 = proj.reshape(batch, SEQ, n)
    nsteps = SEQ // MLSTM_ROWS
    nb = MLSTM_BATCH
    npair = nb * MLSTM_HEADS
    blk = lambda cb: pl.BlockSpec((nb, MLSTM_ROWS, MLSTM_WIDTH), lambda b, c, cb=cb: (b, c, cb))
    y = pl.pallas_call(
        _mlstm_kernel,
        grid=(batch // nb, nsteps),
        in_specs=[
            blk(OFF_MQ // MLSTM_WIDTH), blk(OFF_MK // MLSTM_WIDTH),
            blk(OFF_MV // MLSTM_WIDTH), blk(OFF_MO // MLSTM_WIDTH),
            pl.BlockSpec((nb, MLSTM_ROWS, LANE), lambda b, c: (b, c, OFF_SMALL // LANE)),
            pl.BlockSpec((1, LANE), lambda b, c: (0, 0)),
            pl.BlockSpec((1, MLSTM_WIDTH), lambda b, c: (0, 0)),
        ],
        out_specs=pl.BlockSpec((nb, MLSTM_ROWS, MLSTM_WIDTH), lambda b, c: (b, c, 0)),
        out_shape=jax.ShapeDtypeStruct((batch, SEQ, MLSTM_WIDTH), BF16),
        scratch_shapes=[
            pltpu.VMEM((npair, HEAD_DIM, HEAD_DIM), F32),
            pltpu.VMEM((npair, 1, HEAD_DIM), F32),
            pltpu.VMEM((npair, 1, LANE), F32),
        ],
        compiler_params=_cparams("parallel", "arbitrary"),
        name="mlstm",
    )(proj3, proj3, proj3, proj3, proj3, gate_bias, norm_w)
    return y.reshape(m, MLSTM_WIDTH)


def _conv_kernel(a_ref, g_ref, ah_ref, gh_ref, w_ref, b_ref, lw_ref, lb_ref, y_ref, u_scr, s_scr):
    first = pl.program_id(1) == 0
    uh = ah_ref[...] * jax.nn.sigmoid(gh_ref[...])
    u_scr[0:CONV_HALO, :] = jnp.where(first, 0.0, uh)
    u_scr[CONV_HALO:, :] = a_ref[...] * jax.nn.sigmoid(g_ref[...])
    for ph in range(1, SUBLANE):
        s_scr[ph] = u_scr[ph:ph + CONV_PHASE_ROWS, :]
    w = w_ref[...]
    bias = b_ref[...]
    lw = lw_ref[...]
    lb = lb_ref[...]
    base = CONV_HALO - (CONV_LEN - 1)
    for r in range(0, CONV_ROWS, CONV_CHUNK):
        acc = jnp.zeros((CONV_CHUNK, CONV_WIDTH), F32)
        for j in range(CONV_LEN):
            ph, al = (base + j) % SUBLANE, (base + j) // SUBLANE * SUBLANE
            src = u_scr[al + r:al + r + CONV_CHUNK, :] if ph == 0 else s_scr[ph, al + r:al + r + CONV_CHUNK, :]
            acc = acc + w[j:j + 1, :] * src
        u = acc + bias
        mu = jnp.mean(u, axis=-1, keepdims=True)
        uc = u - mu
        y = uc * lax.rsqrt(jnp.mean(uc * uc, axis=-1, keepdims=True) + EPS) * lw + lb
        y_ref[r:r + CONV_CHUNK, :] = (y * jax.nn.sigmoid(y)).astype(y_ref.dtype)


def _conformer(proj, dw_w, dw_b, ln_w, ln_b, batch):
    m = proj.shape[0]
    nt = SEQ // CONV_ROWS
    hb = CONV_ROWS // CONV_HALO
    main = lambda cb: pl.BlockSpec((CONV_ROWS, CONV_WIDTH), lambda b, i, cb=cb: (b * nt + i, cb))
    halo = lambda cb: pl.BlockSpec(
        (CONV_HALO, CONV_WIDTH), lambda b, i, cb=cb: (jnp.maximum((b * nt + i) * hb - 1, 0), cb))
    vec = pl.BlockSpec((1, CONV_WIDTH), lambda b, i: (0, 0))
    return pl.pallas_call(
        _conv_kernel,
        grid=(batch, nt),
        in_specs=[
            main(OFF_CA // CONV_WIDTH), main(OFF_CG // CONV_WIDTH),
            halo(OFF_CA // CONV_WIDTH), halo(OFF_CG // CONV_WIDTH),
            pl.BlockSpec((CONV_LEN, CONV_WIDTH), lambda b, i: (0, 0)),
            vec, vec, vec,
        ],
        out_specs=pl.BlockSpec((CONV_ROWS, CONV_WIDTH), lambda b, i: (b * nt + i, 0)),
        out_shape=jax.ShapeDtypeStruct((m, CONV_WIDTH), BF16),
        scratch_shapes=[pltpu.VMEM((CONV_HALO + CONV_ROWS, CONV_WIDTH), F32),
                        pltpu.VMEM((SUBLANE, CONV_PHASE_ROWS, CONV_WIDTH), F32)],
        compiler_params=_cparams("parallel", "arbitrary"),
        name="conformer_conv",
    )(proj, proj, proj, proj, dw_w, dw_b, ln_w, ln_b)


def _rope(x, cs, sn):
    return x * cs + pltpu.roll(x, HEAD_DIM // 2, axis=1) * sn


def _compress(x_ref, pe_ref, w_ref):
    nrow = SEQ // CMP_STRIDE
    lo = jnp.zeros((nrow, HEAD_DIM), F32)
    hi = jnp.zeros((nrow, HEAD_DIM), F32)
    for l in range(CMP_STRIDE):
        xl = x_ref[pl.ds(l, nrow, stride=CMP_STRIDE), :]
        lo = lo + jnp.dot((xl + pe_ref[l:l + 1, :]).astype(BF16), w_ref[l],
                          preferred_element_type=F32)
        hi = hi + jnp.dot((xl + pe_ref[CMP_STRIDE + l:CMP_STRIDE + l + 1, :]).astype(BF16),
                          w_ref[CMP_STRIDE + l], preferred_element_type=F32)
    return lo + pltpu.roll(hi, nrow - 1, axis=0)


def _nsa_prep_kernel(kc_ref, vc_ref, ks_ref, vs_ref, kw_ref, vw_ref, pek_ref, wk_ref, pev_ref, wv_ref,
                     cs_ref, sn_ref, ccs_ref, csn_ref,
                     kco_ref, vco_ref, kso_ref, vso_ref, kwo_ref, vwo_ref):
    kco_ref[...] = _rope(_compress(kc_ref, pek_ref, wk_ref), ccs_ref[...], csn_ref[...]).astype(BF16)
    vco_ref[...] = _compress(vc_ref, pev_ref, wv_ref).T.astype(BF16)
    cs = cs_ref[...]
    sn = sn_ref[...]
    kso_ref[:, 0:HEAD_DIM] = _rope(ks_ref[...], cs, sn).astype(BF16)
    key = lax.broadcasted_iota(jnp.int32, (SEQ, LANE), 0)
    blk = lax.broadcasted_iota(jnp.int32, (SEQ, LANE), 1)
    kso_ref[:, HEAD_DIM:] = jnp.where(key // SEL_BLOCK == blk, 1.0, 0.0).astype(BF16)
    kwo_ref[...] = _rope(kw_ref[...], cs, sn).astype(BF16)
    for kt in range(SEQ // TK):
        vso_ref[kt] = vs_ref[kt * TK:(kt + 1) * TK, :].T.astype(BF16)
        vwo_ref[kt] = vw_ref[kt * TK:(kt + 1) * TK, :].T.astype(BF16)


def _nsa_prep(proj, pe_k, w_k, pe_v, w_v, tabs, batch, layer):
    cs, sn, ccs, csn = tabs
    G = NSA_KV_HEADS
    ncr = SEQ // CMP_STRIDE
    col = lambda off: pl.BlockSpec((SEQ, HEAD_DIM), lambda b, g, off=off: (b, off // HEAD_DIM + g))
    full2 = lambda r: pl.BlockSpec((r, HEAD_DIM), lambda b, g: (0, 0))
    pespec = pl.BlockSpec((None, CMP_LEN, HEAD_DIM), lambda b, g: (layer, 0, 0))
    wspec = pl.BlockSpec((None, CMP_LEN, HEAD_DIM, HEAD_DIM), lambda b, g: (layer, 0, 0, 0))
    ocmp = pl.BlockSpec((None, None, ncr, HEAD_DIM), lambda b, g: (b, g, 0, 0))
    oseq = pl.BlockSpec((None, None, SEQ, HEAD_DIM), lambda b, g: (b, g, 0, 0))
    oext = pl.BlockSpec((None, None, SEQ, HEAD_DIM + LANE), lambda b, g: (b, g, 0, 0))
    cmp_shape = jax.ShapeDtypeStruct((batch, G, ncr, HEAD_DIM), BF16)
    seq_shape = jax.ShapeDtypeStruct((batch, G, SEQ, HEAD_DIM), BF16)
    ext_shape = jax.ShapeDtypeStruct((batch, G, SEQ, HEAD_DIM + LANE), BF16)
    oval = pl.BlockSpec((None, None, SEQ // TK, HEAD_DIM, TK), lambda b, g: (b, g, 0, 0, 0))
    val_shape = jax.ShapeDtypeStruct((batch, G, SEQ // TK, HEAD_DIM, TK), BF16)
    return pl.pallas_call(
        _nsa_prep_kernel,
        grid=(batch, G),
        in_specs=[
            col(OFF_NKC), col(OFF_NVC), col(OFF_NKS), col(OFF_NVS), col(OFF_NKW), col(OFF_NVW),
            pespec, wspec, pespec, wspec,
            full2(SEQ), full2(SEQ), full2(ncr), full2(ncr),
        ],
        out_specs=[ocmp, ocmp, oext, oval, oseq, oval],
        out_shape=[cmp_shape, cmp_shape, ext_shape, val_shape, seq_shape, val_shape],
        compiler_params=_cparams("parallel", "arbitrary"),
        name="nsa_prep",
    )(proj, proj, proj, proj, proj, proj, pe_k, w_k, pe_v, w_v, cs, sn, ccs, csn)


def _attend(k, v, q, bias, carry, acc_ref):
    m, l = carry
    s = jnp.dot(k, q, preferred_element_type=F32)
    if bias is not None:
        s = s + bias
    m_new = jnp.maximum(m, jnp.max(s, axis=0, keepdims=True))
    alpha = jnp.exp2(m - m_new)
    p = jnp.exp2(s - m_new)
    l = alpha * l + jnp.sum(p, axis=0, keepdims=True)
    acc_ref[...] = alpha * acc_ref[...] + jnp.dot(v, p.astype(BF16), preferred_element_type=F32)
    return m_new, l


def _nsa_kernel(q_ref, sm_ref, cs_ref, sn_ref, kc_ref, vc_ref, ks_ref, vs_ref, kw_ref, vw_ref, y_ref,
                as_scr, aw_scr):
    g = pl.program_id(1)
    qi = pl.program_id(2)
    q0 = qi * TQ
    R = NSA_HPG * TQ
    scale = HEAD_DIM ** -0.5 * LOG2E
    cs = cs_ref[...]
    sn = sn_ref[...]
    q = jnp.concatenate(
        [(_rope(q_ref[:, h * HEAD_DIM:(h + 1) * HEAD_DIM], cs, sn) * scale).T.astype(BF16)
         for h in range(NSA_HPG)], axis=1)
    tok = lax.broadcasted_iota(jnp.int32, (LANE, R), 1) & (TQ - 1)
    rowi = lax.broadcasted_iota(jnp.int32, (LANE, R), 0)

    ncr = SEQ // CMP_STRIDE
    valid_c = (rowi * CMP_STRIDE + (CMP_LEN - 1)) <= q0 + tok
    s_c = jnp.dot(kc_ref[...], q, preferred_element_type=F32) + jnp.where(valid_c, 0.0, NEG)
    m_c = jnp.max(s_c, axis=0, keepdims=True)
    e_c = jnp.where(valid_c, jnp.exp2(s_c - m_c), 0.0)
    p_c = e_c * (1.0 / jnp.maximum(jnp.sum(e_c, axis=0, keepdims=True), 1e-30))
    o_cmp = jnp.dot(vc_ref[...], p_c.astype(BF16), preferred_element_type=F32)
    p_sum = p_c[:, 0:TQ] + p_c[:, TQ:2 * TQ] + p_c[:, 2 * TQ:3 * TQ] + p_c[:, 3 * TQ:4 * TQ]

    jj = lax.broadcasted_iota(jnp.int32, (LANE, ncr), 0)
    nn = lax.broadcasted_iota(jnp.int32, (LANE, ncr), 1)
    ov_t = ((nn * CMP_STRIDE <= jj * SEL_BLOCK + (SEL_BLOCK - 1))
            & (nn * CMP_STRIDE + (CMP_LEN - 1) >= jj * SEL_BLOCK)
            & (jj < N_SEL) & (nn < N_CMP)).astype(F32)
    imp_t = jnp.dot(ov_t, p_sum, precision=lax.Precision.HIGHEST,
                    preferred_element_type=F32)[0:N_SEL]
    j_i = lax.broadcasted_iota(jnp.int32, (N_SEL, TQ), 0)
    t_i = q0 + lax.broadcasted_iota(jnp.int32, (N_SEL, TQ), 1)
    qblk = t_i // SEL_BLOCK
    forced = (j_i == 0) | (j_i == qblk) | (j_i == qblk - 1)
    score = jnp.where(forced, FORCE, jnp.where(j_i * SEL_BLOCK <= t_i, imp_t, -FORCE))
    rank = jnp.zeros((N_SEL, TQ), jnp.int32)
    for jp in range(N_SEL):
        row = score[jp:jp + 1, :]
        beats = (row > score) | ((row == score) & (j_i > jp))
        rank = rank + beats.astype(jnp.int32)
    sel_bias_t = jnp.concatenate(
        [jnp.where(rank < SEL_COUNT, 0.0, NEG), jnp.zeros((LANE - N_SEL, TQ), F32)], axis=0)
    qx = jnp.concatenate([q, jnp.concatenate([sel_bias_t.astype(BF16)] * NSA_HPG, axis=1)], axis=0)

    keyi = lax.broadcasted_iota(jnp.int32, (TK, R), 0)
    toki = lax.broadcasted_iota(jnp.int32, (TK, R), 1) & (TQ - 1)
    causal = jnp.where(keyi <= toki, 0.0, NEG)

    init = (jnp.full((1, R), NEG, F32), jnp.zeros((1, R), F32))
    as_scr[...] = jnp.zeros_like(as_scr)
    aw_scr[...] = jnp.zeros_like(aw_scr)
    diag = pl.multiple_of(q0, TK)

    def sel_step(kt, carry):
        k0 = pl.multiple_of(kt * TK, TK)
        return _attend(ks_ref[pl.ds(k0, TK), :], vs_ref[kt], qx, None, carry, as_scr)

    carry = lax.fori_loop(0, qi, sel_step, init)
    _, l_s = _attend(ks_ref[pl.ds(diag, TK), :], vs_ref[qi], qx, causal, carry, as_scr)

    carry = _attend(kw_ref[pl.ds(diag, TK), :], vw_ref[qi], q, causal, init, aw_scr)
    t1 = jnp.maximum(qi - 1, 0)
    carry = _attend(kw_ref[pl.ds(pl.multiple_of(t1 * TK, TK), TK), :], vw_ref[t1], q,
                    jnp.where(qi >= 1, jnp.zeros((TK, R), F32), NEG), carry, aw_scr)
    t2 = jnp.maximum(qi - 2, 0)
    _, l_w = _attend(kw_ref[pl.ds(pl.multiple_of(t2 * TK, TK), TK), :], vw_ref[t2], q,
                     jnp.where(qi >= 2, jnp.where(keyi > toki, 0.0, NEG), NEG), carry, aw_scr)

    gl = jax.nn.sigmoid(sm_ref[...]).T
    o_slc = as_scr[...] * (1.0 / l_s)
    o_win = aw_scr[...] * (1.0 / l_w)
    for h in range(NSA_HPG):
        cs_ = slice(h * TQ, (h + 1) * TQ)
        gate = []
        for c in range(3):
            r0 = GATE_LANE0 + h * 3 + c
            r1 = r0 + NSA_HPG * 3
            gate.append(jnp.where(g == 0, gl[r0:r0 + 1, :], gl[r1:r1 + 1, :]))
        y = gate[0] * o_cmp[:, cs_] + gate[1] * o_slc[:, cs_] + gate[2] * o_win[:, cs_]
        y_ref[:, h * HEAD_DIM:(h + 1) * HEAD_DIM] = y.T.astype(y_ref.dtype)


def _nsa(proj, prep, tabs, batch):
    kc, vc, ks, vs, kw, vw = prep
    cs, sn = tabs[0], tabs[1]
    m = proj.shape[0]
    G = NSA_KV_HEADS
    nq = SEQ // TQ
    ncr = SEQ // CMP_STRIDE
    gw = NSA_HPG * HEAD_DIM
    cmp_spec = pl.BlockSpec((None, None, ncr, HEAD_DIM), lambda b, g, i: (b, g, 0, 0))
    seq_spec = pl.BlockSpec((None, None, SEQ, HEAD_DIM), lambda b, g, i: (b, g, 0, 0))
    ext_spec = pl.BlockSpec((None, None, SEQ, HEAD_DIM + LANE), lambda b, g, i: (b, g, 0, 0))
    val_spec = pl.BlockSpec((None, None, SEQ // TK, HEAD_DIM, TK), lambda b, g, i: (b, g, 0, 0, 0))
    tab_spec = pl.BlockSpec((TQ, HEAD_DIM), lambda b, g, i: (i, 0))
    return pl.pallas_call(
        _nsa_kernel,
        grid=(batch, G, nq),
        in_specs=[
            pl.BlockSpec((TQ, gw), lambda b, g, i: (b * nq + i, OFF_NQ // gw + g)),
            pl.BlockSpec((TQ, LANE), lambda b, g, i: (b * nq + i, OFF_SMALL // LANE)),
            tab_spec, tab_spec, cmp_spec, cmp_spec, ext_spec, val_spec, seq_spec, val_spec,
        ],
        out_specs=pl.BlockSpec((TQ, gw), lambda b, g, i: (b * nq + i, g)),
        out_shape=jax.ShapeDtypeStruct((m, NSA_WIDTH), BF16),
        scratch_shapes=[pltpu.VMEM((HEAD_DIM, NSA_HPG * TQ), F32), pltpu.VMEM((HEAD_DIM, NSA_HPG * TQ), F32)],
        compiler_params=_cparams("parallel", "parallel", "arbitrary"),
        name="nsa_attn",
    )(proj, proj, cs, sn, kc, vc, ks, vs, kw, vw)


def _out_proj_kernel(ya_ref, yb_ref, yc_ref, wa_ref, wb_ref, wc_ref, x_ref, o_ref):
    acc = jnp.dot(ya_ref[...], wa_ref[...], preferred_element_type=F32)
    acc = acc + jnp.dot(yb_ref[...], wb_ref[...], preferred_element_type=F32)
    acc = acc + jnp.dot(yc_ref[...], wc_ref[...], preferred_element_type=F32)
    o_ref[...] = x_ref[...] + acc


def _out_proj(ya, yb, yc, w, x, layer):
    m, d = x.shape
    wa, wb, wc = ya.shape[1], yb.shape[1], yc.shape[1]
    return pl.pallas_call(
        _out_proj_kernel,
        grid=(m // TM_OUT, d // TN_OUT),
        in_specs=[
            pl.BlockSpec((TM_OUT, wa), lambda i, j: (i, 0)),
            pl.BlockSpec((TM_OUT, wb), lambda i, j: (i, 0)),
            pl.BlockSpec((TM_OUT, wc), lambda i, j: (i, 0)),
            pl.BlockSpec((None, wa, TN_OUT), lambda i, j: (layer, 0, j)),
            pl.BlockSpec((None, wb, TN_OUT), lambda i, j: (layer, wa // wb, j)),
            pl.BlockSpec((None, wc, TN_OUT), lambda i, j: (layer, (wa + wb) // wc, j)),
            pl.BlockSpec((TM_OUT, TN_OUT), lambda i, j: (i, j)),
        ],
        out_specs=pl.BlockSpec((TM_OUT, TN_OUT), lambda i, j: (i, j)),
        out_shape=jax.ShapeDtypeStruct((m, d), F32),
        compiler_params=_cparams("parallel", "arbitrary"),
        name="out_proj",
    )(ya, yb, yc, w, w, w, x)


def _ffn_up_kernel(x_ref, xh_ref, nw_ref, wg_ref, wv_ref, cg_ref, cv_ref, o_ref,
                   h_scr, hh_scr, g_scr, v_scr):
    i = pl.program_id(0)

    @pl.when(pl.program_id(1) == 0)
    def _():
        h_scr[...] = _rms(x_ref[...], nw_ref[...]).astype(BF16)
        hh_scr[...] = _rms(xh_ref[...], nw_ref[...]).astype(BF16)

    keep = jnp.where((i * TM_UP) % SEQ == 0, 0.0, 1.0)
    cg = 0.5 * cg_ref[...]
    cv = cv_ref[...]

    for w_ref, scr in ((wg_ref, g_scr), (wv_ref, v_scr)):
        w = w_ref[...]
        scr[0:FFN_HALO, :] = jnp.dot(hh_scr[...], w, preferred_element_type=F32) * keep
        scr[FFN_HALO:, :] = jnp.dot(h_scr[...], w, preferred_element_type=F32)

    def conv(scr, c, r):
        lo = FFN_HALO + r - SUBLANE
        win = scr[lo:lo + SUBLANE + FFN_EPI_ROWS, :]
        out = c[FFN_CONV_LEN - 1:FFN_CONV_LEN, :] * win[SUBLANE:, :]
        for t in range(FFN_CONV_LEN - 1):
            back = FFN_CONV_LEN - 1 - t
            out = out + c[t:t + 1, :] * pltpu.roll(win, back, axis=0)[SUBLANE:, :]
        return out

    for r in range(0, TM_UP, FFN_EPI_ROWS):
        half = conv(g_scr, cg, r)
        val = conv(v_scr, cv, r)
        o_ref[r:r + FFN_EPI_ROWS, :] = ((half + half * jnp.tanh(half)) * val).astype(o_ref.dtype)


def _ffn_up(x, nw, w_up, dw, layer):
    m, d = x.shape
    nj = D_FF // TN_UP
    hb = TM_UP // FFN_HALO
    raw = pltpu.VMEM((FFN_HALO + TM_UP, TN_UP), F32)
    return pl.pallas_call(
        _ffn_up_kernel,
        grid=(m // TM_UP, nj),
        in_specs=[
            pl.BlockSpec((TM_UP, d), lambda i, j: (i, 0)),
            pl.BlockSpec((FFN_HALO, d), lambda i, j: (jnp.maximum(i * hb - 1, 0), 0)),
            pl.BlockSpec((1, d), lambda i, j: (0, 0)),
            pl.BlockSpec((None, d, TN_UP), lambda i, j: (layer, 0, j)),
            pl.BlockSpec((None, d, TN_UP), lambda i, j: (layer, 0, j + nj)),
            pl.BlockSpec((None, FFN_CONV_LEN, TN_UP), lambda i, j: (layer, 0, j)),
            pl.BlockSpec((None, FFN_CONV_LEN, TN_UP), lambda i, j: (layer, 0, j + nj)),
        ],
        out_specs=pl.BlockSpec((TM_UP, TN_UP), lambda i, j: (i, j)),
        out_shape=jax.ShapeDtypeStruct((m, D_FF), BF16),
        scratch_shapes=[pltpu.VMEM((TM_UP, d), BF16), pltpu.VMEM((FFN_HALO, d), BF16), raw, raw],
        compiler_params=_cparams("parallel", "arbitrary"),
        name="ffn_up",
    )(x, x, nw, w_up, w_up, dw, dw)


def _norm_kernel(x_ref, w_ref, o_ref):
    o_ref[...] = _rms(x_ref[...], w_ref[...]).astype(o_ref.dtype)


def _norm(x, w, dtype):
    m, d = x.shape
    return pl.pallas_call(
        _norm_kernel,
        grid=(m // TM_NORM,),
        in_specs=[pl.BlockSpec((TM_NORM, d), lambda i: (i, 0)), pl.BlockSpec((1, d), lambda i: (0, 0))],
        out_specs=pl.BlockSpec((TM_NORM, d), lambda i: (i, 0)),
        out_shape=jax.ShapeDtypeStruct((m, d), dtype),
        compiler_params=_cparams("parallel"),
        name="rms_norm",
    )(x, w)


def _ffn_down_kernel(a_ref, w_ref, x_ref, o_ref):
    o_ref[...] = x_ref[...] + jnp.dot(a_ref[...], w_ref[...], preferred_element_type=F32)


def _ffn_down(a, w, x, layer):
    m, d = x.shape
    f = a.shape[1]
    return pl.pallas_call(
        _ffn_down_kernel,
        grid=(m // TM_DN, d // TN_DN),
        in_specs=[
            pl.BlockSpec((TM_DN, f), lambda i, j: (i, 0)),
            pl.BlockSpec((None, f, TN_DN), lambda i, j: (layer, 0, j)),
            pl.BlockSpec((TM_DN, TN_DN), lambda i, j: (i, j)),
        ],
        out_specs=pl.BlockSpec((TM_DN, TN_DN), lambda i, j: (i, j)),
        out_shape=jax.ShapeDtypeStruct((m, d), F32),
        compiler_params=_cparams("parallel", "arbitrary"),
        name="ffn_down",
    )(a, w, x)


def _rope_tables():
    half = HEAD_DIM // 2
    inv = ROPE_THETA ** (-np.arange(half, dtype=np.float64) / half)

    def tab(pos):
        ang = np.asarray(pos, np.float64)[:, None] * inv[None, :]
        cos, sin = np.cos(ang), np.sin(ang)
        return (jnp.asarray(np.concatenate([cos, cos], -1), F32),
                jnp.asarray(np.concatenate([-sin, sin], -1), F32))

    cs, sn = tab(np.arange(SEQ))
    ccs, csn = tab(np.arange(SEQ // CMP_STRIDE) * CMP_STRIDE + CMP_LEN - 1)
    return cs, sn, ccs, csn


def _reorder_kernel(w_ref, o_ref):
    a0 = 4 * MLSTM_WIDTH
    a1 = a0 + 2 * MLSTM_HEADS
    a2 = a1 + (OFF_SMALL - OFF_CA)
    a3 = a2 + 3 * NSA_HEADS
    cols = o_ref.shape[1]
    o_ref[0:a0, :] = w_ref[0:a0, :].astype(BF16)
    o_ref[a0:OFF_SMALL, :] = w_ref[a1:a2, :].astype(BF16)
    small = jnp.concatenate(
        [w_ref[a0:a1, :], w_ref[a2:a3, :], jnp.zeros((N_PROJ - OFF_SMALL - (a1 - a0) - (a3 - a2), cols), F32)],
        axis=0)
    o_ref[OFF_SMALL:, :] = small.astype(BF16)


def _reorder_w_in(w):
    wt = jnp.swapaxes(w, 1, 2)
    depth, n, d = wt.shape
    return pl.pallas_call(
        _reorder_kernel,
        grid=(depth, d // TK_REORDER),
        in_specs=[pl.BlockSpec((None, n, TK_REORDER), lambda l, k: (l, 0, k))],
        out_specs=pl.BlockSpec((None, N_PROJ, TK_REORDER), lambda l, k: (l, 0, k)),
        out_shape=jax.ShapeDtypeStruct((depth, N_PROJ, d), BF16),
        compiler_params=_cparams("parallel", "parallel"),
        name="w_in_layout",
    )(wt)


def kernel(x, attn_norm_w, w_in, mlstm_i_bias, mlstm_f_bias, mlstm_norm_w, conv_dw_w, conv_dw_b, conv_ln_w, conv_ln_b, nsa_cmp_pe_k, nsa_cmp_w_k, nsa_cmp_pe_v, nsa_cmp_w_v, w_out, ffn_norm_w, w_up, ffn_dw_w, w_down, final_norm_w):
    batch, seq, d = x.shape
    assert seq == SEQ and d == D_MODEL
    depth = w_in.shape[0]
    tabs = _rope_tables()
    w_in_b = _reorder_w_in(w_in)
    w_out_b = w_out.astype(BF16)
    w_up_b = w_up.astype(BF16)
    w_down_b = w_down.astype(BF16)
    w_ck = nsa_cmp_w_k.reshape(depth, CMP_LEN, HEAD_DIM, HEAD_DIM).astype(BF16)
    w_cv = nsa_cmp_w_v.reshape(depth, CMP_LEN, HEAD_DIM, HEAD_DIM).astype(BF16)
    xf = x.reshape(batch * seq, d)
    for l in range(depth):
        proj = _norm_matmul(xf, attn_norm_w[l][None, :], w_in_b, l)
        gate_bias = jnp.concatenate(
            [mlstm_i_bias[l], mlstm_f_bias[l], jnp.zeros((LANE - 2 * MLSTM_HEADS,), F32)])[None, :]
        y_a = _mlstm(proj, gate_bias, mlstm_norm_w[l][None, :], batch)
        y_b = _conformer(proj, conv_dw_w[l], conv_dw_b[l][None, :], conv_ln_w[l][None, :],
                         conv_ln_b[l][None, :], batch)
        prep = _nsa_prep(proj, nsa_cmp_pe_k, w_ck, nsa_cmp_pe_v, w_cv, tabs, batch, l)
        y_c = _nsa(proj, prep, tabs, batch)
        xf = _out_proj(y_a, y_b, y_c, w_out_b, xf, l)
        act = _ffn_up(xf, ffn_norm_w[l][None, :], w_up_b, ffn_dw_w, l)
        xf = _ffn_down(act, w_down_b, xf, l)
    return _norm(xf, final_norm_w[None, :], F32).reshape(batch, seq, d)
```

```python
import functools

import numpy as np
import jax
import jax.numpy as jnp
from jax import lax
from jax.experimental import pallas as pl
from jax.experimental.pallas import tpu as pltpu

F32 = jnp.float32
BF16 = jnp.bfloat16

D_MODEL = 2048
SEQ = 2048
HEAD_DIM = 128
MLSTM_HEADS = 4
MLSTM_WIDTH = MLSTM_HEADS * HEAD_DIM
MLSTM_CHUNK = 64
CONV_WIDTH = 512
CONV_LEN = 31
NSA_HEADS = 8
NSA_KV_HEADS = 2
NSA_HPG = NSA_HEADS // NSA_KV_HEADS
NSA_WIDTH = NSA_HEADS * HEAD_DIM
CMP_LEN = 32
CMP_STRIDE = 16
N_CMP = (SEQ - CMP_LEN) // CMP_STRIDE + 1
SEL_BLOCK = 64
SEL_COUNT = 16
N_SEL = SEQ // SEL_BLOCK
WINDOW = 512
ROPE_THETA = 10000.0
D_FF = 5632
FFN_CONV_LEN = 3
EPS = 1e-6
NEG = -1e30
FORCE = 1e4

OFF_MQ, OFF_MK, OFF_MV, OFF_MO = 0, 512, 1024, 1536
OFF_CA, OFF_CG = 2048, 2560
OFF_NQ = 3072
OFF_NKC, OFF_NVC, OFF_NKS, OFF_NVS, OFF_NKW, OFF_NVW = 4096, 4352, 4608, 4864, 5120, 5376
OFF_SMALL = 5632
N_PROJ = 5760
LANE = 128
GATE_LANE0 = 8

VMEM_LIMIT = 52 * 1024 * 1024

TM_IN, TN_IN = 1024, 1152
TM_OUT, TN_OUT = 2048, 512
TM_UP, TN_UP = 1024, 512
TM_DN, TN_DN = 1024, 512
TM_NORM = 512
TK_REORDER = 256
MLSTM_ROWS = 2 * MLSTM_CHUNK
MLSTM_BATCH = 4
CONV_ROWS = 512
CONV_HALO = 32
SUBLANE = 8
CONV_PHASE_ROWS = CONV_HALO + CONV_ROWS - SUBLANE
CONV_CHUNK = 32
FFN_HALO = 16
FFN_EPI_ROWS = 32
TQ = 256
TK = 256
assert TQ == TK and WINDOW == 2 * TK
LOG2E = 1.4426950408889634


def _cparams(*sem):
    return pltpu.CompilerParams(dimension_semantics=sem, vmem_limit_bytes=VMEM_LIMIT)


def _rms(x, w):
    return x * lax.rsqrt(jnp.mean(x * x, axis=-1, keepdims=True) + EPS) * w


def _norm_matmul_kernel(x_ref, nw_ref, w_ref, o_ref, h_scr):
    @pl.when(pl.program_id(1) == 0)
    def _():
        h_scr[...] = _rms(x_ref[...], nw_ref[...]).astype(BF16)

    o_ref[...] = lax.dot_general(h_scr[...], w_ref[...], (((1,), (1,)), ((), ())),
                                 preferred_element_type=F32)


def _norm_matmul(x, nw, w, layer):
    m, d = x.shape
    n = w.shape[1]
    return pl.pallas_call(
        _norm_matmul_kernel,
        grid=(m // TM_IN, n // TN_IN),
        in_specs=[
            pl.BlockSpec((TM_IN, d), lambda i, j: (i, 0)),
            pl.BlockSpec((1, d), lambda i, j: (0, 0)),
            pl.BlockSpec((None, TN_IN, d), lambda i, j: (layer, j, 0)),
        ],
        out_specs=pl.BlockSpec((TM_IN, TN_IN), lambda i, j: (i, j)),
        out_shape=jax.ShapeDtypeStruct((m, n), F32),
        scratch_shapes=[pltpu.VMEM((TM_IN, d), BF16)],
        compiler_params=_cparams("parallel", "arbitrary"),
        name="in_proj",
    )(x, nw, w)


def _log_sigmoid(x):
    return jnp.minimum(x, 0.0) - jnp.log1p(jnp.exp(-jnp.abs(x)))


def _mlstm_kernel(q_ref, k_ref, v_ref, o_ref, g_ref, gb_ref, nw_ref, y_ref, c_scr, n_scr, m_scr):
    L = MLSTM_CHUNK

    @pl.when(pl.program_id(1) == 0)
    def _():
        c_scr[...] = jnp.zeros_like(c_scr)
        n_scr[...] = jnp.zeros_like(n_scr)
        m_scr[...] = jnp.full_like(m_scr, -jnp.inf)

    row = lax.broadcasted_iota(jnp.int32, (L, L), 0)
    col = lax.broadcasted_iota(jnp.int32, (L, L), 1)
    causal = col <= row
    seqs = range(q_ref.shape[0])
    gates = [g_ref[b] + gb_ref[...] for b in seqs]
    in_chunk = lax.broadcasted_iota(jnp.int32, (MLSTM_ROWS, LANE), 0) & (L - 1)

    def chunk_cumsum(x):
        d = 1
        while d < L:
            x = x + jnp.where(in_chunk >= d, pltpu.roll(x, d, axis=0), 0.0)
            d *= 2
        return x

    b_all = [chunk_cumsum(_log_sigmoid(gates[b])) for b in seqs]
    b_t = [x.T for x in b_all]
    g_t = [x.T for x in gates]
    scale = HEAD_DIM ** -0.5

    pairs = [(b, h) for b in seqs for h in range(MLSTM_HEADS)]
    hsl = [slice(h * HEAD_DIM, (h + 1) * HEAD_DIM) for h in range(MLSTM_HEADS)]
    per_head = lambda f: jnp.stack([f(b, h) for b, h in pairs], axis=0)
    k_all = {(b, h): k_ref[b, :, hsl[h]] * scale for b, h in pairs}
    k_allt = {p: k.T for p, k in k_all.items()}
    nw = per_head(lambda b, h: nw_ref[:, hsl[h]])
    c_st = c_scr[...]
    n_st = n_scr[...]
    m_st = m_scr[...][:, :, 0:1]
    bdot = lambda a, b, ca, cb: lax.dot_general(
        a.astype(BF16), b.astype(BF16), (((ca,), (cb,)), ((0,), (0,))), preferred_element_type=F32)
    for c in range(2):
        rs = slice(c * L, (c + 1) * L)
        qc = per_head(lambda b, h: q_ref[b, rs, hsl[h]])
        kc = per_head(lambda b, h: k_all[b, h][rs])
        kct = per_head(lambda b, h: k_allt[b, h][:, rs])
        vc = per_head(lambda b, h: v_ref[b, rs, hsl[h]])
        b_col = per_head(lambda b, h: b_all[b][rs, 4 + h:5 + h])
        i_col = per_head(lambda b, h: gates[b][rs, h:h + 1])
        b_row = per_head(lambda b, h: b_t[b][4 + h:5 + h, rs])
        i_row = per_head(lambda b, h: g_t[b][h:h + 1, rs])
        dmat = jnp.where(causal[None], b_col - b_row + i_row, -jnp.inf)
        inter = b_col + m_st
        m_t = jnp.maximum(inter, jnp.max(dmat, axis=-1, keepdims=True))
        s = bdot(qc, kc, 2, 2) * jnp.exp(dmat - m_t)
        w_inter = jnp.exp(inter - m_t)
        num = bdot(s, vc, 2, 1) + w_inter * bdot(qc, c_st, 2, 1)
        den = (jnp.sum(s, axis=-1, keepdims=True)
               + w_inter * jnp.sum(qc * n_st, axis=-1, keepdims=True))
        hh = num / jnp.maximum(jnp.abs(den), jnp.exp(-m_t))
        b_last = b_col[:, L - 1:L, :]
        g = b_last - b_col + i_col
        m_new = jnp.maximum(b_last + m_st, jnp.max(g, axis=1, keepdims=True))
        w_k = jnp.exp(g - m_new)
        decay = jnp.exp(b_last + m_st - m_new)
        c_st = decay * c_st + bdot(kct, w_k * vc, 2, 1)
        n_st = decay * n_st + jnp.sum(w_k * kc, axis=1, keepdims=True)
        m_st = m_new
        mu = jnp.mean(hh, axis=-1, keepdims=True)
        hc = hh - mu
        hn = hc * lax.rsqrt(jnp.mean(hc * hc, axis=-1, keepdims=True) + EPS) * nw
        for p, (b, h) in enumerate(pairs):
            y_ref[b, rs, hsl[h]] = (jax.nn.sigmoid(o_ref[b, rs, hsl[h]]) * hn[p]).astype(y_ref.dtype)
    c_scr[...] = c_st
    n_scr[...] = n_st
    m_scr[...] = jnp.broadcast_to(m_st, m_scr.shape)


def _mlstm(proj, gate_bias, norm_w, batch):
    m, n = proj.shape
    proj3 = proj.reshape(batch, SEQ, n)
    nsteps = SEQ // MLSTM_ROWS
    nb = MLSTM_BATCH
    npair = nb * MLSTM_HEADS
    blk = lambda cb: pl.BlockSpec((nb, MLSTM_ROWS, MLSTM_WIDTH), lambda b, c, cb=cb: (b, c, cb))
    y = pl.pallas_call(
        _mlstm_kernel,
        grid=(batch // nb, nsteps),
        in_specs=[
            blk(OFF_MQ // MLSTM_WIDTH), blk(OFF_MK // MLSTM_WIDTH),
            blk(OFF_MV // MLSTM_WIDTH), blk(OFF_MO // MLSTM_WIDTH),
            pl.BlockSpec((nb, MLSTM_ROWS, LANE), lambda b, c: (b, c, OFF_SMALL // LANE)),
            pl.BlockSpec((1, LANE), lambda b, c: (0, 0)),
            pl.BlockSpec((1, MLSTM_WIDTH), lambda b, c: (0, 0)),
        ],
        out_specs=pl.BlockSpec((nb, MLSTM_ROWS, MLSTM_WIDTH), lambda b, c: (b, c, 0)),
        out_shape=jax.ShapeDtypeStruct((batch, SEQ, MLSTM_WIDTH), BF16),
        scratch_shapes=[
            pltpu.VMEM((npair, HEAD_DIM, HEAD_DIM), F32),
            pltpu.VMEM((npair, 1, HEAD_DIM), F32),
            pltpu.VMEM((npair, 1, LANE), F32),
        ],
        compiler_params=_cparams("parallel", "arbitrary"),
        name="mlstm",
    )(proj3, proj3, proj3, proj3, proj3, gate_bias, norm_w)
    return y.reshape(m, MLSTM_WIDTH)


def _conv_kernel(a_ref, g_ref, ah_ref, gh_ref, w_ref, b_ref, lw_ref, lb_ref, y_ref, u_scr, s_scr):
    first = pl.program_id(1) == 0
    uh = ah_ref[...] * jax.nn.sigmoid(gh_ref[...])
    u_scr[0:CONV_HALO, :] = jnp.where(first, 0.0, uh)
    u_scr[CONV_HALO:, :] = a_ref[...] * jax.nn.sigmoid(g_ref[...])
    for ph in range(1, SUBLANE):
        s_scr[ph] = u_scr[ph:ph + CONV_PHASE_ROWS, :]
    w = w_ref[...]
    bias = b_ref[...]
    lw = lw_ref[...]
    lb = lb_ref[...]
    base = CONV_HALO - (CONV_LEN - 1)
    for r in range(0, CONV_ROWS, CONV_CHUNK):
        acc = jnp.zeros((CONV_CHUNK, CONV_WIDTH), F32)
        for j in range(CONV_LEN):
            ph, al = (base + j) % SUBLANE, (base + j) // SUBLANE * SUBLANE
            src = u_scr[al + r:al + r + CONV_CHUNK, :] if ph == 0 else s_scr[ph, al + r:al + r + CONV_CHUNK, :]
            acc = acc + w[j:j + 1, :] * src
        u = acc + bias
        mu = jnp.mean(u, axis=-1, keepdims=True)
        uc = u - mu
        y = uc * lax.rsqrt(jnp.mean(uc * uc, axis=-1, keepdims=True) + EPS) * lw + lb
        y_ref[r:r + CONV_CHUNK, :] = (y * jax.nn.sigmoid(y)).astype(y_ref.dtype)


def _conformer(proj, dw_w, dw_b, ln_w, ln_b, batch):
    m = proj.shape[0]
    nt = SEQ // CONV_ROWS
    hb = CONV_ROWS // CONV_HALO
    main = lambda cb: pl.BlockSpec((CONV_ROWS, CONV_WIDTH), lambda b, i, cb=cb: (b * nt + i, cb))
    halo = lambda cb: pl.BlockSpec(
        (CONV_HALO, CONV_WIDTH), lambda b, i, cb=cb: (jnp.maximum((b * nt + i) * hb - 1, 0), cb))
    vec = pl.BlockSpec((1, CONV_WIDTH), lambda b, i: (0, 0))
    return pl.pallas_call(
        _conv_kernel,
        grid=(batch, nt),
        in_specs=[
            main(OFF_CA // CONV_WIDTH), main(OFF_CG // CONV_WIDTH),
            halo(OFF_CA // CONV_WIDTH), halo(OFF_CG // CONV_WIDTH),
            pl.BlockSpec((CONV_LEN, CONV_WIDTH), lambda b, i: (0, 0)),
            vec, vec, vec,
        ],
        out_specs=pl.BlockSpec((CONV_ROWS, CONV_WIDTH), lambda b, i: (b * nt + i, 0)),
        out_shape=jax.ShapeDtypeStruct((m, CONV_WIDTH), BF16),
        scratch_shapes=[pltpu.VMEM((CONV_HALO + CONV_ROWS, CONV_WIDTH), F32),
                        pltpu.VMEM((SUBLANE, CONV_PHASE_ROWS, CONV_WIDTH), F32)],
        compiler_params=_cparams("parallel", "arbitrary"),
        name="conformer_conv",
    )(proj, proj, proj, proj, dw_w, dw_b, ln_w, ln_b)


def _rope(x, cs, sn):
    return x * cs + pltpu.roll(x, HEAD_DIM // 2, axis=1) * sn


def _compress(x_ref, pe_ref, w_ref):
    nrow = SEQ // CMP_STRIDE
    lo = jnp.zeros((nrow, HEAD_DIM), F32)
    hi = jnp.zeros((nrow, HEAD_DIM), F32)
    for l in range(CMP_STRIDE):
        xl = x_ref[pl.ds(l, nrow, stride=CMP_STRIDE), :]
        lo = lo + jnp.dot((xl + pe_ref[l:l + 1, :]).astype(BF16), w_ref[l],
                          preferred_element_type=F32)
        hi = hi + jnp.dot((xl + pe_ref[CMP_STRIDE + l:CMP_STRIDE + l + 1, :]).astype(BF16),
                          w_ref[CMP_STRIDE + l], preferred_element_type=F32)
    return lo + pltpu.roll(hi, nrow - 1, axis=0)


def _nsa_prep_kernel(kc_ref, vc_ref, ks_ref, vs_ref, kw_ref, vw_ref, pek_ref, wk_ref, pev_ref, wv_ref,
                     cs_ref, sn_ref, ccs_ref, csn_ref,
                     kco_ref, vco_ref, kso_ref, vso_ref, kwo_ref, vwo_ref):
    kco_ref[...] = _rope(_compress(kc_ref, pek_ref, wk_ref), ccs_ref[...], csn_ref[...]).astype(BF16)
    vco_ref[...] = _compress(vc_ref, pev_ref, wv_ref).T.astype(BF16)
    cs = cs_ref[...]
    sn = sn_ref[...]
    kso_ref[:, 0:HEAD_DIM] = _rope(ks_ref[...], cs, sn).astype(BF16)
    key = lax.broadcasted_iota(jnp.int32, (SEQ, LANE), 0)
    blk = lax.broadcasted_iota(jnp.int32, (SEQ, LANE), 1)
    kso_ref[:, HEAD_DIM:] = jnp.where(key // SEL_BLOCK == blk, 1.0, 0.0).astype(BF16)
    kwo_ref[...] = _rope(kw_ref[...], cs, sn).astype(BF16)
    for kt in range(SEQ // TK):
        vso_ref[kt] = vs_ref[kt * TK:(kt + 1) * TK, :].T.astype(BF16)
        vwo_ref[kt] = vw_ref[kt * TK:(kt + 1) * TK, :].T.astype(BF16)


def _nsa_prep(proj, pe_k, w_k, pe_v, w_v, tabs, batch, layer):
    cs, sn, ccs, csn = tabs
    G = NSA_KV_HEADS
    ncr = SEQ // CMP_STRIDE
    col = lambda off: pl.BlockSpec((SEQ, HEAD_DIM), lambda b, g, off=off: (b, off // HEAD_DIM + g))
    full2 = lambda r: pl.BlockSpec((r, HEAD_DIM), lambda b, g: (0, 0))
    pespec = pl.BlockSpec((None, CMP_LEN, HEAD_DIM), lambda b, g: (layer, 0, 0))
    wspec = pl.BlockSpec((None, CMP_LEN, HEAD_DIM, HEAD_DIM), lambda b, g: (layer, 0, 0, 0))
    ocmp = pl.BlockSpec((None, None, ncr, HEAD_DIM), lambda b, g: (b, g, 0, 0))
    oseq = pl.BlockSpec((None, None, SEQ, HEAD_DIM), lambda b, g: (b, g, 0, 0))
    oext = pl.BlockSpec((None, None, SEQ, HEAD_DIM + LANE), lambda b, g: (b, g, 0, 0))
    cmp_shape = jax.ShapeDtypeStruct((batch, G, ncr, HEAD_DIM), BF16)
    seq_shape = jax.ShapeDtypeStruct((batch, G, SEQ, HEAD_DIM), BF16)
    ext_shape = jax.ShapeDtypeStruct((batch, G, SEQ, HEAD_DIM + LANE), BF16)
    oval = pl.BlockSpec((None, None, SEQ // TK, HEAD_DIM, TK), lambda b, g: (b, g, 0, 0, 0))
    val_shape = jax.ShapeDtypeStruct((batch, G, SEQ // TK, HEAD_DIM, TK), BF16)
    return pl.pallas_call(
        _nsa_prep_kernel,
        grid=(batch, G),
        in_specs=[
            col(OFF_NKC), col(OFF_NVC), col(OFF_NKS), col(OFF_NVS), col(OFF_NKW), col(OFF_NVW),
            pespec, wspec, pespec, wspec,
            full2(SEQ), full2(SEQ), full2(ncr), full2(ncr),
        ],
        out_specs=[ocmp, ocmp, oext, oval, oseq, oval],
        out_shape=[cmp_shape, cmp_shape, ext_shape, val_shape, seq_shape, val_shape],
        compiler_params=_cparams("parallel", "arbitrary"),
        name="nsa_prep",
    )(proj, proj, proj, proj, proj, proj, pe_k, w_k, pe_v, w_v, cs, sn, ccs, csn)


def _attend(k, v, q, bias, carry, acc_ref):
    s = jnp.dot(k, q, preferred_element_type=F32)
    if bias is not None:
        s = s + bias
    if carry is None:
        m_new = jnp.max(s, axis=0, keepdims=True)
        p = jnp.exp2(s - m_new)
        acc_ref[...] = jnp.dot(v, p.astype(BF16), preferred_element_type=F32)
        return m_new, jnp.sum(p, axis=0, keepdims=True)
    m, l = carry
    m_new = jnp.maximum(m, jnp.max(s, axis=0, keepdims=True))
    alpha = jnp.exp2(m - m_new)
    p = jnp.exp2(s - m_new)
    l = alpha * l + jnp.sum(p, axis=0, keepdims=True)
    acc_ref[...] = alpha * acc_ref[...] + jnp.dot(v, p.astype(BF16), preferred_element_type=F32)
    return m_new, l


def _nsa_kernel(q_ref, sm_ref, cs_ref, sn_ref, kc_ref, vc_ref, ks_ref, vs_ref, kw_ref, vw_ref, y_ref,
                as_scr, aw_scr):
    g = pl.program_id(1)
    qi = pl.program_id(2)
    q0 = qi * TQ
    R = NSA_HPG * TQ
    scale = HEAD_DIM ** -0.5 * LOG2E
    cs = cs_ref[...]
    sn = sn_ref[...]
    q = jnp.concatenate(
        [(_rope(q_ref[:, h * HEAD_DIM:(h + 1) * HEAD_DIM], cs, sn) * scale).T.astype(BF16)
         for h in range(NSA_HPG)], axis=1)
    tok = lax.broadcasted_iota(jnp.int32, (LANE, R), 1) & (TQ - 1)
    rowi = lax.broadcasted_iota(jnp.int32, (LANE, R), 0)

    ncr = SEQ // CMP_STRIDE
    valid_c = (rowi * CMP_STRIDE + (CMP_LEN - 1)) <= q0 + tok
    s_c = jnp.dot(kc_ref[...], q, preferred_element_type=F32) + jnp.where(valid_c, 0.0, NEG)
    m_c = jnp.max(s_c, axis=0, keepdims=True)
    e_c = jnp.where(valid_c, jnp.exp2(s_c - m_c), 0.0)
    p_c = e_c * (1.0 / jnp.maximum(jnp.sum(e_c, axis=0, keepdims=True), 1e-30))
    o_cmp = jnp.dot(vc_ref[...], p_c.astype(BF16), preferred_element_type=F32)
    p_sum = p_c[:, 0:TQ] + p_c[:, TQ:2 * TQ] + p_c[:, 2 * TQ:3 * TQ] + p_c[:, 3 * TQ:4 * TQ]

    jj = lax.broadcasted_iota(jnp.int32, (LANE, ncr), 0)
    nn = lax.broadcasted_iota(jnp.int32, (LANE, ncr), 1)
    ov_t = ((nn * CMP_STRIDE <= jj * SEL_BLOCK + (SEL_BLOCK - 1))
            & (nn * CMP_STRIDE + (CMP_LEN - 1) >= jj * SEL_BLOCK)
            & (jj < N_SEL) & (nn < N_CMP)).astype(F32)
    hi = p_sum.astype(BF16)
    r1 = p_sum - hi.astype(F32)
    mid = r1.astype(BF16)
    lo = (r1 - mid.astype(F32)).astype(BF16)
    imp_t = jnp.dot(jnp.concatenate([ov_t.astype(BF16)] * 3, axis=1), jnp.concatenate([hi, mid, lo], axis=0),
                    preferred_element_type=F32)[0:N_SEL]
    j_i = lax.broadcasted_iota(jnp.int32, (N_SEL, TQ), 0)
    t_i = q0 + lax.broadcasted_iota(jnp.int32, (N_SEL, TQ), 1)
    qblk = t_i // SEL_BLOCK
    forced = (j_i == 0) | (j_i == qblk) | (j_i == qblk - 1)
    score = jnp.where(forced, FORCE, jnp.where(j_i * SEL_BLOCK <= t_i, imp_t, -FORCE))
    rank = jnp.zeros((N_SEL, TQ), jnp.int32)
    for jp in range(N_SEL):
        row = score[jp:jp + 1, :]
        beats = (row > score) | ((row == score) & (j_i > jp))
        rank = rank + beats.astype(jnp.int32)
    sel_bias_t = jnp.concatenate(
        [jnp.where(rank < SEL_COUNT, 0.0, NEG), jnp.zeros((LANE - N_SEL, TQ), F32)], axis=0)
    qx = jnp.concatenate([q, jnp.concatenate([sel_bias_t.astype(BF16)] * NSA_HPG, axis=1)], axis=0)

    keyi = lax.broadcasted_iota(jnp.int32, (TK, R), 0)
    toki = lax.broadcasted_iota(jnp.int32, (TK, R), 1) & (TQ - 1)
    causal = jnp.where(keyi <= toki, 0.0, NEG)

    diag = pl.multiple_of(q0, TK)

    def sel_step(kt, carry):
        k0 = pl.multiple_of(kt * TK, TK)
        return _attend(ks_ref[pl.ds(k0, TK), :], vs_ref[kt], qx, None, carry, as_scr)

    carry = _attend(ks_ref[pl.ds(diag, TK), :], vs_ref[qi], qx, causal, None, as_scr)
    _, l_s = lax.fori_loop(0, qi, sel_step, carry)

    carry = _attend(kw_ref[pl.ds(diag, TK), :], vw_ref[qi], q, causal, None, aw_scr)
    t1 = jnp.maximum(qi - 1, 0)
    carry = _attend(kw_ref[pl.ds(pl.multiple_of(t1 * TK, TK), TK), :], vw_ref[t1], q,
                    jnp.where(qi >= 1, jnp.zeros((TK, R), F32), NEG), carry, aw_scr)
    t2 = jnp.maximum(qi - 2, 0)
    _, l_w = _attend(kw_ref[pl.ds(pl.multiple_of(t2 * TK, TK), TK), :], vw_ref[t2], q,
                     jnp.where(qi >= 2, jnp.where(keyi > toki, 0.0, NEG), NEG), carry, aw_scr)

    gl = jax.nn.sigmoid(sm_ref[...]).T
    o_slc = as_scr[...] * (1.0 / l_s)
    o_win = aw_scr[...] * (1.0 / l_w)
    for h in range(NSA_HPG):
        cs_ = slice(h * TQ, (h + 1) * TQ)
        gate = []
        for c in range(3):
            r0 = GATE_LANE0 + h * 3 + c
            r1 = r0 + NSA_HPG * 3
            gate.append(jnp.where(g == 0, gl[r0:r0 + 1, :], gl[r1:r1 + 1, :]))
        y = gate[0] * o_cmp[:, cs_] + gate[1] * o_slc[:, cs_] + gate[2] * o_win[:, cs_]
        y_ref[:, h * HEAD_DIM:(h + 1) * HEAD_DIM] = y.T.astype(y_ref.dtype)


def _nsa(proj, prep, tabs, batch):
    kc, vc, ks, vs, kw, vw = prep
    cs, sn = tabs[0], tabs[1]
    m = proj.shape[0]
    G = NSA_KV_HEADS
    nq = SEQ // TQ
    ncr = SEQ // CMP_STRIDE
    gw = NSA_HPG * HEAD_DIM
    cmp_spec = pl.BlockSpec((None, None, ncr, HEAD_DIM), lambda b, g, i: (b, g, 0, 0))
    seq_spec = pl.BlockSpec((None, None, SEQ, HEAD_DIM), lambda b, g, i: (b, g, 0, 0))
    ext_spec = pl.BlockSpec((None, None, SEQ, HEAD_DIM + LANE), lambda b, g, i: (b, g, 0, 0))
    val_spec = pl.BlockSpec((None, None, SEQ // TK, HEAD_DIM, TK), lambda b, g, i: (b, g, 0, 0, 0))
    tab_spec = pl.BlockSpec((TQ, HEAD_DIM), lambda b, g, i: (i, 0))
    return pl.pallas_call(
        _nsa_kernel,
        grid=(batch, G, nq),
        in_specs=[
            pl.BlockSpec((TQ, gw), lambda b, g, i: (b * nq + i, OFF_NQ // gw + g)),
            pl.BlockSpec((TQ, LANE), lambda b, g, i: (b * nq + i, OFF_SMALL // LANE)),
            tab_spec, tab_spec, cmp_spec, cmp_spec, ext_spec, val_spec, seq_spec, val_spec,
        ],
        out_specs=pl.BlockSpec((TQ, gw), lambda b, g, i: (b * nq + i, g)),
        out_shape=jax.ShapeDtypeStruct((m, NSA_WIDTH), BF16),
        scratch_shapes=[pltpu.VMEM((HEAD_DIM, NSA_HPG * TQ), F32), pltpu.VMEM((HEAD_DIM, NSA_HPG * TQ), F32)],
        compiler_params=_cparams("parallel", "parallel", "arbitrary"),
        name="nsa_attn",
    )(proj, proj, cs, sn, kc, vc, ks, vs, kw, vw)


def _out_proj_kernel(ya_ref, yb_ref, yc_ref, wa_ref, wb_ref, wc_ref, x_ref, o_ref):
    acc = jnp.dot(ya_ref[...], wa_ref[...], preferred_element_type=F32)
    acc = acc + jnp.dot(yb_ref[...], wb_ref[...], preferred_element_type=F32)
    acc = acc + jnp.dot(yc_ref[...], wc_ref[...], preferred_element_type=F32)
    o_ref[...] = x_ref[...] + acc


def _out_proj(ya, yb, yc, w, x, layer):
    m, d = x.shape
    wa, wb, wc = ya.shape[1], yb.shape[1], yc.shape[1]
    return pl.pallas_call(
        _out_proj_kernel,
        grid=(m // TM_OUT, d // TN_OUT),
        in_specs=[
            pl.BlockSpec((TM_OUT, wa), lambda i, j: (i, 0)),
            pl.BlockSpec((TM_OUT, wb), lambda i, j: (i, 0)),
            pl.BlockSpec((TM_OUT, wc), lambda i, j: (i, 0)),
            pl.BlockSpec((None, wa, TN_OUT), lambda i, j: (layer, 0, j)),
            pl.BlockSpec((None, wb, TN_OUT), lambda i, j: (layer, wa // wb, j)),
            pl.BlockSpec((None, wc, TN_OUT), lambda i, j: (layer, (wa + wb) // wc, j)),
            pl.BlockSpec((TM_OUT, TN_OUT), lambda i, j: (i, j)),
        ],
        out_specs=pl.BlockSpec((TM_OUT, TN_OUT), lambda i, j: (i, j)),
        out_shape=jax.ShapeDtypeStruct((m, d), F32),
        compiler_params=_cparams("parallel", "arbitrary"),
        name="out_proj",
    )(ya, yb, yc, w, w, w, x)


def _ffn_up_kernel(x_ref, xh_ref, nw_ref, wg_ref, wv_ref, cg_ref, cv_ref, o_ref,
                   h_scr, hh_scr, g_scr, v_scr):
    i = pl.program_id(0)

    @pl.when(pl.program_id(1) == 0)
    def _():
        h_scr[...] = _rms(x_ref[...], nw_ref[...]).astype(BF16)
        hh_scr[...] = _rms(xh_ref[...], nw_ref[...]).astype(BF16)

    keep = jnp.where((i * TM_UP) % SEQ == 0, 0.0, 1.0)
    cg = 0.5 * cg_ref[...]
    cv = cv_ref[...]

    for w_ref, scr in ((wg_ref, g_scr), (wv_ref, v_scr)):
        w = w_ref[...]
        scr[0:FFN_HALO, :] = jnp.dot(hh_scr[...], w, preferred_element_type=F32) * keep
        scr[FFN_HALO:, :] = jnp.dot(h_scr[...], w, preferred_element_type=F32)

    def conv(scr, c, r):
        lo = FFN_HALO + r - SUBLANE
        win = scr[lo:lo + SUBLANE + FFN_EPI_ROWS, :]
        out = c[FFN_CONV_LEN - 1:FFN_CONV_LEN, :] * win[SUBLANE:, :]
        for t in range(FFN_CONV_LEN - 1):
            back = FFN_CONV_LEN - 1 - t
            out = out + c[t:t + 1, :] * pltpu.roll(win, back, axis=0)[SUBLANE:, :]
        return out

    for r in range(0, TM_UP, FFN_EPI_ROWS):
        half = conv(g_scr, cg, r)
        val = conv(v_scr, cv, r)
        o_ref[r:r + FFN_EPI_ROWS, :] = ((half + half * jnp.tanh(half)) * val).astype(o_ref.dtype)


def _ffn_up(x, nw, w_up, dw, layer):
    m, d = x.shape
    nj = D_FF // TN_UP
    hb = TM_UP // FFN_HALO
    raw = pltpu.VMEM((FFN_HALO + TM_UP, TN_UP), F32)
    return pl.pallas_call(
        _ffn_up_kernel,
        grid=(m // TM_UP, nj),
        in_specs=[
            pl.BlockSpec((TM_UP, d), lambda i, j: (i, 0)),
            pl.BlockSpec((FFN_HALO, d), lambda i, j: (jnp.maximum(i * hb - 1, 0), 0)),
            pl.BlockSpec((1, d), lambda i, j: (0, 0)),
            pl.BlockSpec((None, d, TN_UP), lambda i, j: (layer, 0, j)),
            pl.BlockSpec((None, d, TN_UP), lambda i, j: (layer, 0, j + nj)),
            pl.BlockSpec((None, FFN_CONV_LEN, TN_UP), lambda i, j: (layer, 0, j)),
            pl.BlockSpec((None, FFN_CONV_LEN, TN_UP), lambda i, j: (layer, 0, j + nj)),
        ],
        out_specs=pl.BlockSpec((TM_UP, TN_UP), lambda i, j: (i, j)),
        out_shape=jax.ShapeDtypeStruct((m, D_FF), BF16),
        scratch_shapes=[pltpu.VMEM((TM_UP, d), BF16), pltpu.VMEM((FFN_HALO, d), BF16), raw, raw],
        compiler_params=_cparams("parallel", "arbitrary"),
        name="ffn_up",
    )(x, x, nw, w_up, w_up, dw, dw)


def _norm_kernel(x_ref, w_ref, o_ref):
    o_ref[...] = _rms(x_ref[...], w_ref[...]).astype(o_ref.dtype)


def _norm(x, w, dtype):
    m, d = x.shape
    return pl.pallas_call(
        _norm_kernel,
        grid=(m // TM_NORM,),
        in_specs=[pl.BlockSpec((TM_NORM, d), lambda i: (i, 0)), pl.BlockSpec((1, d), lambda i: (0, 0))],
        out_specs=pl.BlockSpec((TM_NORM, d), lambda i: (i, 0)),
        out_shape=jax.ShapeDtypeStruct((m, d), dtype),
        compiler_params=_cparams("parallel"),
        name="rms_norm",
    )(x, w)


def _ffn_down_kernel(a_ref, w_ref, x_ref, o_ref):
    o_ref[...] = x_ref[...] + jnp.dot(a_ref[...], w_ref[...], preferred_element_type=F32)


def _ffn_down(a, w, x, layer):
    m, d = x.shape
    f = a.shape[1]
    return pl.pallas_call(
        _ffn_down_kernel,
        grid=(m // TM_DN, d // TN_DN),
        in_specs=[
            pl.BlockSpec((TM_DN, f), lambda i, j: (i, 0)),
            pl.BlockSpec((None, f, TN_DN), lambda i, j: (layer, 0, j)),
            pl.BlockSpec((TM_DN, TN_DN), lambda i, j: (i, j)),
        ],
        out_specs=pl.BlockSpec((TM_DN, TN_DN), lambda i, j: (i, j)),
        out_shape=jax.ShapeDtypeStruct((m, d), F32),
        compiler_params=_cparams("parallel", "arbitrary"),
        name="ffn_down",
    )(a, w, x)


def _rope_tables():
    half = HEAD_DIM // 2
    inv = ROPE_THETA ** (-np.arange(half, dtype=np.float64) / half)

    def tab(pos):
        ang = np.asarray(pos, np.float64)[:, None] * inv[None, :]
        cos, sin = np.cos(ang), np.sin(ang)
        return (jnp.asarray(np.concatenate([cos, cos], -1), F32),
                jnp.asarray(np.concatenate([-sin, sin], -1), F32))

    cs, sn = tab(np.arange(SEQ))
    ccs, csn = tab(np.arange(SEQ // CMP_STRIDE) * CMP_STRIDE + CMP_LEN - 1)
    return cs, sn, ccs, csn


def _reorder_kernel(w_ref, o_ref):
    a0 = 4 * MLSTM_WIDTH
    a1 = a0 + 2 * MLSTM_HEADS
    a2 = a1 + (OFF_SMALL - OFF_CA)
    a3 = a2 + 3 * NSA_HEADS
    cols = o_ref.shape[1]
    o_ref[0:a0, :] = w_ref[0:a0, :].astype(BF16)
    o_ref[a0:OFF_SMALL, :] = w_ref[a1:a2, :].astype(BF16)
    small = jnp.concatenate(
        [w_ref[a0:a1, :], w_ref[a2:a3, :], jnp.zeros((N_PROJ - OFF_SMALL - (a1 - a0) - (a3 - a2), cols), F32)],
        axis=0)
    o_ref[OFF_SMALL:, :] = small.astype(BF16)


def _reorder_w_in(w):
    wt = jnp.swapaxes(w, 1, 2)
    depth, n, d = wt.shape
    return pl.pallas_call(
        _reorder_kernel,
        grid=(depth, d // TK_REORDER),
        in_specs=[pl.BlockSpec((None, n, TK_REORDER), lambda l, k: (l, 0, k))],
        out_specs=pl.BlockSpec((None, N_PROJ, TK_REORDER), lambda l, k: (l, 0, k)),
        out_shape=jax.ShapeDtypeStruct((depth, N_PROJ, d), BF16),
        compiler_params=_cparams("parallel", "parallel"),
        name="w_in_layout",
    )(wt)


def kernel(x, attn_norm_w, w_in, mlstm_i_bias, mlstm_f_bias, mlstm_norm_w, conv_dw_w, conv_dw_b, conv_ln_w, conv_ln_b, nsa_cmp_pe_k, nsa_cmp_w_k, nsa_cmp_pe_v, nsa_cmp_w_v, w_out, ffn_norm_w, w_up, ffn_dw_w, w_down, final_norm_w):
    batch, seq, d = x.shape
    assert seq == SEQ and d == D_MODEL
    depth = w_in.shape[0]
    tabs = _rope_tables()
    w_in_b = _reorder_w_in(w_in)
    w_out_b = w_out.astype(BF16)
    w_up_b = w_up.astype(BF16)
    w_down_b = w_down.astype(BF16)
    w_ck = nsa_cmp_w_k.reshape(depth, CMP_LEN, HEAD_DIM, HEAD_DIM).astype(BF16)
    w_cv = nsa_cmp_w_v.reshape(depth, CMP_LEN, HEAD_DIM, HEAD_DIM).astype(BF16)
    xf = x.reshape(batch * seq, d)
    for l in range(depth):
        proj = _norm_matmul(xf, attn_norm_w[l][None, :], w_in_b, l)
        gate_bias = jnp.concatenate(
            [mlstm_i_bias[l], mlstm_f_bias[l], jnp.zeros((LANE - 2 * MLSTM_HEADS,), F32)])[None, :]
        y_a = _mlstm(proj, gate_bias, mlstm_norm_w[l][None, :], batch)
        y_b = _conformer(proj, conv_dw_w[l], conv_dw_b[l][None, :], conv_ln_w[l][None, :],
                         conv_ln_b[l][None, :], batch)
        prep = _nsa_prep(proj, nsa_cmp_pe_k, w_ck, nsa_cmp_pe_v, w_cv, tabs, batch, l)
        y_c = _nsa(proj, prep, tabs, batch)
        xf = _out_proj(y_a, y_b, y_c, w_out_b, xf, l)
        act = _ffn_up(xf, ffn_norm_w[l][None, :], w_up_b, ffn_dw_w, l)
        xf = _ffn_down(act, w_down_b, xf, l)
    return _norm(xf, final_norm_w[None, :], F32).reshape(batch, seq, d)
```

```python
import numpy as np
import jax
import jax.numpy as jnp
from jax import lax
from jax.experimental import pallas as pl
from jax.experimental.pallas import tpu as pltpu

F32 = jnp.float32
BF16 = jnp.bfloat16

D_MODEL = 2048
SEQ = 2048
HEAD_DIM = 128
MLSTM_HEADS = 4
MLSTM_WIDTH = MLSTM_HEADS * HEAD_DIM
MLSTM_CHUNK = 64
CONV_WIDTH = 512
CONV_LEN = 31
NSA_HEADS = 8
NSA_KV_HEADS = 2
NSA_HPG = NSA_HEADS // NSA_KV_HEADS
NSA_WIDTH = NSA_HEADS * HEAD_DIM
CMP_LEN = 32
CMP_STRIDE = 16
N_CMP = (SEQ - CMP_LEN) // CMP_STRIDE + 1
SEL_BLOCK = 64
SEL_COUNT = 16
N_SEL = SEQ // SEL_BLOCK
WINDOW = 512
ROPE_THETA = 10000.0
D_FF = 5632
FFN_CONV_LEN = 3
EPS = 1e-6
NEG = -1e30
FORCE = 1e4

OFF_MQ, OFF_MK, OFF_MV, OFF_MO = 0, 512, 1024, 1536
OFF_CA, OFF_CG = 2048, 2560
OFF_NQ = 3072
OFF_NKC, OFF_NVC, OFF_NKS, OFF_NVS, OFF_NKW, OFF_NVW = 4096, 4352, 4608, 4864, 5120, 5376
OFF_SMALL = 5632
N_PROJ = 5760
LANE = 128
GATE_LANE0 = 8

VMEM_LIMIT = 52 * 1024 * 1024

TM_IN, TN_IN = 1024, 1152
TM_OUT, TN_OUT = 2048, 512
TM_UP, TN_UP = 1024, 512
TM_DN, TN_DN = 1024, 512
TM_NORM = 512
TK_REORDER = 256
MLSTM_ROWS = 2 * MLSTM_CHUNK
MLSTM_BATCH = 4
CONV_ROWS = 512
CONV_HALO = 32
SUBLANE = 8
CONV_PHASE_ROWS = CONV_HALO + CONV_ROWS - SUBLANE
CONV_CHUNK = 32
FFN_HALO = 16
FFN_EPI_ROWS = 32
TQ = 256
TK = 256
assert TQ == TK and WINDOW == 2 * TK
LOG2E = 1.4426950408889634


def _cparams(*sem):
    return pltpu.CompilerParams(dimension_semantics=sem, vmem_limit_bytes=VMEM_LIMIT)


def _rms(x, w):
    return x * lax.rsqrt(jnp.mean(x * x, axis=-1, keepdims=True) + EPS) * w


def _norm_matmul_kernel(x_ref, nw_ref, w_ref, o_ref, h_scr):
    @pl.when(pl.program_id(1) == 0)
    def _():
        h_scr[...] = _rms(x_ref[...], nw_ref[...]).astype(BF16)

    o_ref[...] = lax.dot_general(h_scr[...], w_ref[...], (((1,), (1,)), ((), ())),
                                 preferred_element_type=F32)


def _norm_matmul(x, nw, w, layer):
    m, d = x.shape
    n = w.shape[1]
    return pl.pallas_call(
        _norm_matmul_kernel,
        grid=(m // TM_IN, n // TN_IN),
        in_specs=[
            pl.BlockSpec((TM_IN, d), lambda i, j: (i, 0)),
            pl.BlockSpec((1, d), lambda i, j: (0, 0)),
            pl.BlockSpec((None, TN_IN, d), lambda i, j: (layer, j, 0)),
        ],
        out_specs=pl.BlockSpec((TM_IN, TN_IN), lambda i, j: (i, j)),
        out_shape=jax.ShapeDtypeStruct((m, n), F32),
        scratch_shapes=[pltpu.VMEM((TM_IN, d), BF16)],
        compiler_params=_cparams("parallel", "arbitrary"),
        name="in_proj",
    )(x, nw, w)


def _log_sigmoid(x):
    return jnp.minimum(x, 0.0) - jnp.log1p(jnp.exp(-jnp.abs(x)))


def _mlstm_kernel(q_ref, k_ref, v_ref, o_ref, g_ref, gb_ref, nw_ref, y_ref, c_scr, n_scr, m_scr):
    L = MLSTM_CHUNK

    @pl.when(pl.program_id(1) == 0)
    def _():
        c_scr[...] = jnp.zeros_like(c_scr)
        n_scr[...] = jnp.zeros_like(n_scr)
        m_scr[...] = jnp.full_like(m_scr, -jnp.inf)

    row = lax.broadcasted_iota(jnp.int32, (L, L), 0)
    col = lax.broadcasted_iota(jnp.int32, (L, L), 1)
    causal = col <= row
    seqs = range(q_ref.shape[0])
    gates = [g_ref[b] + gb_ref[...] for b in seqs]
    in_chunk = lax.broadcasted_iota(jnp.int32, (MLSTM_ROWS, LANE), 0) & (L - 1)

    def chunk_cumsum(x):
        d = 1
        while d < L:
            x = x + jnp.where(in_chunk >= d, pltpu.roll(x, d, axis=0), 0.0)
            d *= 2
        return x

    b_all = [chunk_cumsum(_log_sigmoid(gates[b])) for b in seqs]
    b_t = [x.T for x in b_all]
    g_t = [x.T for x in gates]
    scale = HEAD_DIM ** -0.5

    pairs = [(b, h) for b in seqs for h in range(MLSTM_HEADS)]
    hsl = [slice(h * HEAD_DIM, (h + 1) * HEAD_DIM) for h in range(MLSTM_HEADS)]
    per_head = lambda f: jnp.stack([f(b, h) for b, h in pairs], axis=0)
    k_all = {(b, h): k_ref[b, :, hsl[h]] * scale for b, h in pairs}
    k_allt = {p: k.T for p, k in k_all.items()}
    nw = per_head(lambda b, h: nw_ref[:, hsl[h]])
    c_st = c_scr[...]
    n_st = n_scr[...]
    m_st = m_scr[...][:, :, 0:1]
    bdot = lambda a, b, ca, cb: lax.dot_general(
        a.astype(BF16), b.astype(BF16), (((ca,), (cb,)), ((0,), (0,))), preferred_element_type=F32)
    for c in range(2):
        rs = slice(c * L, (c + 1) * L)
        qc = per_head(lambda b, h: q_ref[b, rs, hsl[h]])
        kc = per_head(lambda b, h: k_all[b, h][rs])
        kct = per_head(lambda b, h: k_allt[b, h][:, rs])
        vc = per_head(lambda b, h: v_ref[b, rs, hsl[h]])
        b_col = per_head(lambda b, h: b_all[b][rs, 4 + h:5 + h])
        i_col = per_head(lambda b, h: gates[b][rs, h:h + 1])
        b_row = per_head(lambda b, h: b_t[b][4 + h:5 + h, rs])
        i_row = per_head(lambda b, h: g_t[b][h:h + 1, rs])
        dmat = jnp.where(causal[None], b_col - b_row + i_row, -jnp.inf)
        inter = b_col + m_st
        m_t = jnp.maximum(inter, jnp.max(dmat, axis=-1, keepdims=True))
        s = bdot(qc, kc, 2, 2) * jnp.exp(dmat - m_t)
        w_inter = jnp.exp(inter - m_t)
        num = bdot(s, vc, 2, 1) + w_inter * bdot(qc, c_st, 2, 1)
        den = (jnp.sum(s, axis=-1, keepdims=True)
               + w_inter * jnp.sum(qc * n_st, axis=-1, keepdims=True))
        hh = num / jnp.maximum(jnp.abs(den), jnp.exp(-m_t))
        b_last = b_col[:, L - 1:L, :]
        g = b_last - b_col + i_col
        m_new = jnp.maximum(b_last + m_st, jnp.max(g, axis=1, keepdims=True))
        w_k = jnp.exp(g - m_new)
        decay = jnp.exp(b_last + m_st - m_new)
        c_st = decay * c_st + bdot(kct, w_k * vc, 2, 1)
        n_st = decay * n_st + jnp.sum(w_k * kc, axis=1, keepdims=True)
        m_st = m_new
        mu = jnp.mean(hh, axis=-1, keepdims=True)
        hc = hh - mu
        hn = hc * lax.rsqrt(jnp.mean(hc * hc, axis=-1, keepdims=True) + EPS) * nw
        for p, (b, h) in enumerate(pairs):
            y_ref[b, rs, hsl[h]] = (jax.nn.sigmoid(o_ref[b, rs, hsl[h]]) * hn[p]).astype(y_ref.dtype)
    c_scr[...] = c_st
    n_scr[...] = n_st
    m_scr[...] = jnp.broadcast_to(m_st, m_scr.shape)


def _mlstm(proj, gate_bias, norm_w, batch):
    m, n = proj.shape
    proj3 = proj.reshape(batch, SEQ, n)
    nsteps = SEQ // MLSTM_ROWS
    nb = MLSTM_BATCH
    npair = nb * MLSTM_HEADS
    blk = lambda cb: pl.BlockSpec((nb, MLSTM_ROWS, MLSTM_WIDTH), lambda b, c, cb=cb: (b, c, cb))
    y = pl.pallas_call(
        _mlstm_kernel,
        grid=(batch // nb, nsteps),
        in_specs=[
            blk(OFF_MQ // MLSTM_WIDTH), blk(OFF_MK // MLSTM_WIDTH),
            blk(OFF_MV // MLSTM_WIDTH), blk(OFF_MO // MLSTM_WIDTH),
            pl.BlockSpec((nb, MLSTM_ROWS, LANE), lambda b, c: (b, c, OFF_SMALL // LANE)),
            pl.BlockSpec((1, LANE), lambda b, c: (0, 0)),
            pl.BlockSpec((1, MLSTM_WIDTH), lambda b, c: (0, 0)),
        ],
        out_specs=pl.BlockSpec((nb, MLSTM_ROWS, MLSTM_WIDTH), lambda b, c: (b, c, 0)),
        out_shape=jax.ShapeDtypeStruct((batch, SEQ, MLSTM_WIDTH), BF16),
        scratch_shapes=[
            pltpu.VMEM((npair, HEAD_DIM, HEAD_DIM), F32),
            pltpu.VMEM((npair, 1, HEAD_DIM), F32),
            pltpu.VMEM((npair, 1, LANE), F32),
        ],
        compiler_params=_cparams("parallel", "arbitrary"),
        name="mlstm",
    )(proj3, proj3, proj3, proj3, proj3, gate_bias, norm_w)
    return y.reshape(m, MLSTM_WIDTH)


def _conv_kernel(a_ref, g_ref, ah_ref, gh_ref, w_ref, b_ref, lw_ref, lb_ref, y_ref, u_scr, s_scr):
    first = pl.program_id(1) == 0
    uh = ah_ref[...] * jax.nn.sigmoid(gh_ref[...])
    u_scr[0:CONV_HALO, :] = jnp.where(first, 0.0, uh)
    u_scr[CONV_HALO:, :] = a_ref[...] * jax.nn.sigmoid(g_ref[...])
    for ph in range(1, SUBLANE):
        s_scr[ph] = u_scr[ph:ph + CONV_PHASE_ROWS, :]
    w = w_ref[...]
    bias = b_ref[...]
    lw = lw_ref[...]
    lb = lb_ref[...]
    base = CONV_HALO - (CONV_LEN - 1)
    for r in range(0, CONV_ROWS, CONV_CHUNK):
        acc = jnp.zeros((CONV_CHUNK, CONV_WIDTH), F32)
        for j in range(CONV_LEN):
            ph, al = (base + j) % SUBLANE, (base + j) // SUBLANE * SUBLANE
            src = u_scr[al + r:al + r + CONV_CHUNK, :] if ph == 0 else s_scr[ph, al + r:al + r + CONV_CHUNK, :]
            acc = acc + w[j:j + 1, :] * src
        u = acc + bias
        mu = jnp.mean(u, axis=-1, keepdims=True)
        uc = u - mu
        y = uc * lax.rsqrt(jnp.mean(uc * uc, axis=-1, keepdims=True) + EPS) * lw + lb
        y_ref[r:r + CONV_CHUNK, :] = (y * jax.nn.sigmoid(y)).astype(y_ref.dtype)


def _conformer(proj, dw_w, dw_b, ln_w, ln_b, batch):
    m = proj.shape[0]
    nt = SEQ // CONV_ROWS
    hb = CONV_ROWS // CONV_HALO
    main = lambda cb: pl.BlockSpec((CONV_ROWS, CONV_WIDTH), lambda b, i, cb=cb: (b * nt + i, cb))
    halo = lambda cb: pl.BlockSpec(
        (CONV_HALO, CONV_WIDTH), lambda b, i, cb=cb: (jnp.maximum((b * nt + i) * hb - 1, 0), cb))
    vec = pl.BlockSpec((1, CONV_WIDTH), lambda b, i: (0, 0))
    return pl.pallas_call(
        _conv_kernel,
        grid=(batch, nt),
        in_specs=[
            main(OFF_CA // CONV_WIDTH), main(OFF_CG // CONV_WIDTH),
            halo(OFF_CA // CONV_WIDTH), halo(OFF_CG // CONV_WIDTH),
            pl.BlockSpec((CONV_LEN, CONV_WIDTH), lambda b, i: (0, 0)),
            vec, vec, vec,
        ],
        out_specs=pl.BlockSpec((CONV_ROWS, CONV_WIDTH), lambda b, i: (b * nt + i, 0)),
        out_shape=jax.ShapeDtypeStruct((m, CONV_WIDTH), BF16),
        scratch_shapes=[pltpu.VMEM((CONV_HALO + CONV_ROWS, CONV_WIDTH), F32),
                        pltpu.VMEM((SUBLANE, CONV_PHASE_ROWS, CONV_WIDTH), F32)],
        compiler_params=_cparams("parallel", "arbitrary"),
        name="conformer_conv",
    )(proj, proj, proj, proj, dw_w, dw_b, ln_w, ln_b)


def _rope(x, cs, sn):
    return x * cs + pltpu.roll(x, HEAD_DIM // 2, axis=1) * sn


def _compress(x_ref, pe_ref, w_ref):
    nrow = SEQ // CMP_STRIDE
    lo = jnp.zeros((nrow, HEAD_DIM), F32)
    hi = jnp.zeros((nrow, HEAD_DIM), F32)
    for l in range(CMP_STRIDE):
        xl = x_ref[pl.ds(l, nrow, stride=CMP_STRIDE), :]
        lo = lo + jnp.dot((xl + pe_ref[l:l + 1, :]).astype(BF16), w_ref[l],
                          preferred_element_type=F32)
        hi = hi + jnp.dot((xl + pe_ref[CMP_STRIDE + l:CMP_STRIDE + l + 1, :]).astype(BF16),
                          w_ref[CMP_STRIDE + l], preferred_element_type=F32)
    return lo + pltpu.roll(hi, nrow - 1, axis=0)


def _nsa_prep_kernel(kc_ref, vc_ref, ks_ref, vs_ref, kw_ref, vw_ref, pek_ref, wk_ref, pev_ref, wv_ref,
                     cs_ref, sn_ref, ccs_ref, csn_ref,
                     kco_ref, vco_ref, kso_ref, vso_ref, kwo_ref, vwo_ref):
    kco_ref[...] = _rope(_compress(kc_ref, pek_ref, wk_ref), ccs_ref[...], csn_ref[...]).astype(BF16)
    vco_ref[...] = _compress(vc_ref, pev_ref, wv_ref).T.astype(BF16)
    cs = cs_ref[...]
    sn = sn_ref[...]
    kso_ref[:, 0:HEAD_DIM] = _rope(ks_ref[...], cs, sn).astype(BF16)
    key = lax.broadcasted_iota(jnp.int32, (SEQ, LANE), 0)
    blk = lax.broadcasted_iota(jnp.int32, (SEQ, LANE), 1)
    kso_ref[:, HEAD_DIM:] = jnp.where(key // SEL_BLOCK == blk, 1.0, 0.0).astype(BF16)
    kwo_ref[...] = _rope(kw_ref[...], cs, sn).astype(BF16)
    for kt in range(SEQ // TK):
        vso_ref[kt] = vs_ref[kt * TK:(kt + 1) * TK, :].T.astype(BF16)
        vwo_ref[kt] = vw_ref[kt * TK:(kt + 1) * TK, :].T.astype(BF16)


def _nsa_prep(proj, pe_k, w_k, pe_v, w_v, tabs, batch, layer):
    cs, sn, ccs, csn = tabs
    G = NSA_KV_HEADS
    ncr = SEQ // CMP_STRIDE
    col = lambda off: pl.BlockSpec((SEQ, HEAD_DIM), lambda b, g, off=off: (b, off // HEAD_DIM + g))
    full2 = lambda r: pl.BlockSpec((r, HEAD_DIM), lambda b, g: (0, 0))
    pespec = pl.BlockSpec((None, CMP_LEN, HEAD_DIM), lambda b, g: (layer, 0, 0))
    wspec = pl.BlockSpec((None, CMP_LEN, HEAD_DIM, HEAD_DIM), lambda b, g: (layer, 0, 0, 0))
    ocmp = pl.BlockSpec((None, None, ncr, HEAD_DIM), lambda b, g: (b, g, 0, 0))
    oseq = pl.BlockSpec((None, None, SEQ, HEAD_DIM), lambda b, g: (b, g, 0, 0))
    oext = pl.BlockSpec((None, None, SEQ, HEAD_DIM + LANE), lambda b, g: (b, g, 0, 0))
    cmp_shape = jax.ShapeDtypeStruct((batch, G, ncr, HEAD_DIM), BF16)
    seq_shape = jax.ShapeDtypeStruct((batch, G, SEQ, HEAD_DIM), BF16)
    ext_shape = jax.ShapeDtypeStruct((batch, G, SEQ, HEAD_DIM + LANE), BF16)
    oval = pl.BlockSpec((None, None, SEQ // TK, HEAD_DIM, TK), lambda b, g: (b, g, 0, 0, 0))
    val_shape = jax.ShapeDtypeStruct((batch, G, SEQ // TK, HEAD_DIM, TK), BF16)
    return pl.pallas_call(
        _nsa_prep_kernel,
        grid=(batch, G),
        in_specs=[
            col(OFF_NKC), col(OFF_NVC), col(OFF_NKS), col(OFF_NVS), col(OFF_NKW), col(OFF_NVW),
            pespec, wspec, pespec, wspec,
            full2(SEQ), full2(SEQ), full2(ncr), full2(ncr),
        ],
        out_specs=[ocmp, ocmp, oext, oval, oseq, oval],
        out_shape=[cmp_shape, cmp_shape, ext_shape, val_shape, seq_shape, val_shape],
        compiler_params=_cparams("parallel", "arbitrary"),
        name="nsa_prep",
    )(proj, proj, proj, proj, proj, proj, pe_k, w_k, pe_v, w_v, cs, sn, ccs, csn)


def _attend(k, v, q, bias, carry, acc_ref):
    s = jnp.dot(k, q, preferred_element_type=F32)
    if bias is not None:
        s = s + bias
    if carry is None:
        m_new = jnp.max(s, axis=0, keepdims=True)
        p = jnp.exp2(s - m_new)
        acc_ref[...] = jnp.dot(v, p.astype(BF16), preferred_element_type=F32)
        return m_new, jnp.sum(p, axis=0, keepdims=True)
    m, l = carry
    m_new = jnp.maximum(m, jnp.max(s, axis=0, keepdims=True))
    alpha = jnp.exp2(m - m_new)
    p = jnp.exp2(s - m_new)
    l = alpha * l + jnp.sum(p, axis=0, keepdims=True)
    acc_ref[...] = alpha * acc_ref[...] + jnp.dot(v, p.astype(BF16), preferred_element_type=F32)
    return m_new, l


def _nsa_kernel(q_ref, sm_ref, cs_ref, sn_ref, kc_ref, vc_ref, ks_ref, vs_ref, kw_ref, vw_ref, y_ref,
                as_scr, aw_scr):
    g = pl.program_id(1)
    qi = pl.program_id(2)
    q0 = qi * TQ
    R = NSA_HPG * TQ
    scale = HEAD_DIM ** -0.5 * LOG2E
    cs = cs_ref[...]
    sn = sn_ref[...]
    q = jnp.concatenate(
        [(_rope(q_ref[:, h * HEAD_DIM:(h + 1) * HEAD_DIM], cs, sn) * scale).T.astype(BF16)
         for h in range(NSA_HPG)], axis=1)
    tok = lax.broadcasted_iota(jnp.int32, (LANE, R), 1) & (TQ - 1)
    rowi = lax.broadcasted_iota(jnp.int32, (LANE, R), 0)

    ncr = SEQ // CMP_STRIDE
    valid_c = (rowi * CMP_STRIDE + (CMP_LEN - 1)) <= q0 + tok
    s_c = jnp.dot(kc_ref[...], q, preferred_element_type=F32) + jnp.where(valid_c, 0.0, NEG)
    m_c = jnp.max(s_c, axis=0, keepdims=True)
    e_c = jnp.where(valid_c, jnp.exp2(s_c - m_c), 0.0)
    p_c = e_c * (1.0 / jnp.maximum(jnp.sum(e_c, axis=0, keepdims=True), 1e-30))
    o_cmp = jnp.dot(vc_ref[...], p_c.astype(BF16), preferred_element_type=F32)
    p_sum = p_c[:, 0:TQ] + p_c[:, TQ:2 * TQ] + p_c[:, 2 * TQ:3 * TQ] + p_c[:, 3 * TQ:4 * TQ]

    jj = lax.broadcasted_iota(jnp.int32, (LANE, ncr), 0)
    nn = lax.broadcasted_iota(jnp.int32, (LANE, ncr), 1)
    ov_t = ((nn * CMP_STRIDE <= jj * SEL_BLOCK + (SEL_BLOCK - 1))
            & (nn * CMP_STRIDE + (CMP_LEN - 1) >= jj * SEL_BLOCK)
            & (jj < N_SEL) & (nn < N_CMP)).astype(F32)
    hi = p_sum.astype(BF16)
    r1 = p_sum - hi.astype(F32)
    mid = r1.astype(BF16)
    lo = (r1 - mid.astype(F32)).astype(BF16)
    imp_t = jnp.dot(jnp.concatenate([ov_t.astype(BF16)] * 3, axis=1), jnp.concatenate([hi, mid, lo], axis=0),
                    preferred_element_type=F32)[0:N_SEL]
    j_i = lax.broadcasted_iota(jnp.int32, (N_SEL, TQ), 0)
    t_i = q0 + lax.broadcasted_iota(jnp.int32, (N_SEL, TQ), 1)
    qblk = t_i // SEL_BLOCK
    forced = (j_i == 0) | (j_i == qblk) | (j_i == qblk - 1)
    score = jnp.where(forced, FORCE, jnp.where(j_i * SEL_BLOCK <= t_i, imp_t, -FORCE))
    rank = jnp.zeros((N_SEL, TQ), jnp.int32)
    for jp in range(N_SEL):
        row = score[jp:jp + 1, :]
        beats = (row > score) | ((row == score) & (j_i > jp))
        rank = rank + beats.astype(jnp.int32)
    sel_bias_t = jnp.concatenate(
        [jnp.where(rank < SEL_COUNT, 0.0, NEG), jnp.zeros((LANE - N_SEL, TQ), F32)], axis=0)
    qx = jnp.concatenate([q, jnp.concatenate([sel_bias_t.astype(BF16)] * NSA_HPG, axis=1)], axis=0)

    keyi = lax.broadcasted_iota(jnp.int32, (TK, R), 0)
    toki = lax.broadcasted_iota(jnp.int32, (TK, R), 1) & (TQ - 1)
    causal = jnp.where(keyi <= toki, 0.0, NEG)

    diag = pl.multiple_of(q0, TK)

    def sel_step(kt, carry):
        k0 = pl.multiple_of(kt * TK, TK)
        return _attend(ks_ref[pl.ds(k0, TK), :], vs_ref[kt], qx, None, carry, as_scr)

    carry = _attend(ks_ref[pl.ds(diag, TK), :], vs_ref[qi], qx, causal, None, as_scr)
    _, l_s = lax.fori_loop(0, qi, sel_step, carry)

    carry = _attend(kw_ref[pl.ds(diag, TK), :], vw_ref[qi], q, causal, None, aw_scr)
    t1 = jnp.maximum(qi - 1, 0)
    carry = _attend(kw_ref[pl.ds(pl.multiple_of(t1 * TK, TK), TK), :], vw_ref[t1], q,
                    jnp.where(qi >= 1, 0.0, NEG), carry, aw_scr)
    t2 = jnp.maximum(qi - 2, 0)
    _, l_w = _attend(kw_ref[pl.ds(pl.multiple_of(t2 * TK, TK), TK), :], vw_ref[t2], q,
                     jnp.where(qi >= 2, jnp.where(keyi > toki, 0.0, NEG), NEG), carry, aw_scr)

    gl = jax.nn.sigmoid(sm_ref[...]).T
    o_slc = as_scr[...] * (1.0 / l_s)
    o_win = aw_scr[...] * (1.0 / l_w)
    for h in range(NSA_HPG):
        cs_ = slice(h * TQ, (h + 1) * TQ)
        gate = []
        for c in range(3):
            r0 = GATE_LANE0 + h * 3 + c
            r1 = r0 + NSA_HPG * 3
            gate.append(jnp.where(g == 0, gl[r0:r0 + 1, :], gl[r1:r1 + 1, :]))
        y = gate[0] * o_cmp[:, cs_] + gate[1] * o_slc[:, cs_] + gate[2] * o_win[:, cs_]
        y_ref[:, h * HEAD_DIM:(h + 1) * HEAD_DIM] = y.T.astype(y_ref.dtype)


def _nsa(proj, prep, tabs, batch):
    kc, vc, ks, vs, kw, vw = prep
    cs, sn = tabs[0], tabs[1]
    m = proj.shape[0]
    G = NSA_KV_HEADS
    nq = SEQ // TQ
    ncr = SEQ // CMP_STRIDE
    gw = NSA_HPG * HEAD_DIM
    cmp_spec = pl.BlockSpec((None, None, ncr, HEAD_DIM), lambda b, g, i: (b, g, 0, 0))
    seq_spec = pl.BlockSpec((None, None, SEQ, HEAD_DIM), lambda b, g, i: (b, g, 0, 0))
    ext_spec = pl.BlockSpec((None, None, SEQ, HEAD_DIM + LANE), lambda b, g, i: (b, g, 0, 0))
    val_spec = pl.BlockSpec((None, None, SEQ // TK, HEAD_DIM, TK), lambda b, g, i: (b, g, 0, 0, 0))
    tab_spec = pl.BlockSpec((TQ, HEAD_DIM), lambda b, g, i: (i, 0))
    return pl.pallas_call(
        _nsa_kernel,
        grid=(batch, G, nq),
        in_specs=[
            pl.BlockSpec((TQ, gw), lambda b, g, i: (b * nq + i, OFF_NQ // gw + g)),
            pl.BlockSpec((TQ, LANE), lambda b, g, i: (b * nq + i, OFF_SMALL // LANE)),
            tab_spec, tab_spec, cmp_spec, cmp_spec, ext_spec, val_spec, seq_spec, val_spec,
        ],
        out_specs=pl.BlockSpec((TQ, gw), lambda b, g, i: (b * nq + i, g)),
        out_shape=jax.ShapeDtypeStruct((m, NSA_WIDTH), BF16),
        scratch_shapes=[pltpu.VMEM((HEAD_DIM, NSA_HPG * TQ), F32), pltpu.VMEM((HEAD_DIM, NSA_HPG * TQ), F32)],
        compiler_params=_cparams("parallel", "parallel", "arbitrary"),
        name="nsa_attn",
    )(proj, proj, cs, sn, kc, vc, ks, vs, kw, vw)


def _out_proj_kernel(ya_ref, yb_ref, yc_ref, wa_ref, wb_ref, wc_ref, x_ref, o_ref):
    acc = jnp.dot(ya_ref[...], wa_ref[...], preferred_element_type=F32)
    acc = acc + jnp.dot(yb_ref[...], wb_ref[...], preferred_element_type=F32)
    acc = acc + jnp.dot(yc_ref[...], wc_ref[...], preferred_element_type=F32)
    o_ref[...] = x_ref[...] + acc


def _out_proj(ya, yb, yc, w, x, layer):
    m, d = x.shape
    wa, wb, wc = ya.shape[1], yb.shape[1], yc.shape[1]
    return pl.pallas_call(
        _out_proj_kernel,
        grid=(m // TM_OUT, d // TN_OUT),
        in_specs=[
            pl.BlockSpec((TM_OUT, wa), lambda i, j: (i, 0)),
            pl.BlockSpec((TM_OUT, wb), lambda i, j: (i, 0)),
            pl.BlockSpec((TM_OUT, wc), lambda i, j: (i, 0)),
            pl.BlockSpec((None, wa, TN_OUT), lambda i, j: (layer, 0, j)),
            pl.BlockSpec((None, wb, TN_OUT), lambda i, j: (layer, wa // wb, j)),
            pl.BlockSpec((None, wc, TN_OUT), lambda i, j: (layer, (wa + wb) // wc, j)),
            pl.BlockSpec((TM_OUT, TN_OUT), lambda i, j: (i, j)),
        ],
        out_specs=pl.BlockSpec((TM_OUT, TN_OUT), lambda i, j: (i, j)),
        out_shape=jax.ShapeDtypeStruct((m, d), F32),
        compiler_params=_cparams("parallel", "arbitrary"),
        name="out_proj",
    )(ya, yb, yc, w, w, w, x)


def _ffn_up_kernel(x_ref, xh_ref, nw_ref, wg_ref, wv_ref, cg_ref, cv_ref, o_ref,
                   h_scr, hh_scr, g_scr, v_scr):
    i = pl.program_id(0)

    @pl.when(pl.program_id(1) == 0)
    def _():
        h_scr[...] = _rms(x_ref[...], nw_ref[...]).astype(BF16)
        hh_scr[...] = _rms(xh_ref[...], nw_ref[...]).astype(BF16)

    keep = jnp.where((i * TM_UP) % SEQ == 0, 0.0, 1.0)
    cg = 0.5 * cg_ref[...]
    cv = cv_ref[...]

    for w_ref, scr in ((wg_ref, g_scr), (wv_ref, v_scr)):
        w = w_ref[...]
        scr[0:FFN_HALO, :] = jnp.dot(hh_scr[...], w, preferred_element_type=F32) * keep
        scr[FFN_HALO:, :] = jnp.dot(h_scr[...], w, preferred_element_type=F32)

    def conv(scr, c, r):
        lo = FFN_HALO + r - SUBLANE
        win = scr[lo:lo + SUBLANE + FFN_EPI_ROWS, :]
        out = c[FFN_CONV_LEN - 1:FFN_CONV_LEN, :] * win[SUBLANE:, :]
        for t in range(FFN_CONV_LEN - 1):
            back = FFN_CONV_LEN - 1 - t
            out = out + c[t:t + 1, :] * pltpu.roll(win, back, axis=0)[SUBLANE:, :]
        return out

    for r in range(0, TM_UP, FFN_EPI_ROWS):
        half = conv(g_scr, cg, r)
        val = conv(v_scr, cv, r)
        o_ref[r:r + FFN_EPI_ROWS, :] = ((half + half * jnp.tanh(half)) * val).astype(o_ref.dtype)


def _ffn_up(x, nw, w_up, dw, layer):
    m, d = x.shape
    nj = D_FF // TN_UP
    hb = TM_UP // FFN_HALO
    raw = pltpu.VMEM((FFN_HALO + TM_UP, TN_UP), F32)
    return pl.pallas_call(
        _ffn_up_kernel,
        grid=(m // TM_UP, nj),
        in_specs=[
            pl.BlockSpec((TM_UP, d), lambda i, j: (i, 0)),
            pl.BlockSpec((FFN_HALO, d), lambda i, j: (jnp.maximum(i * hb - 1, 0), 0)),
            pl.BlockSpec((1, d), lambda i, j: (0, 0)),
            pl.BlockSpec((None, d, TN_UP), lambda i, j: (layer, 0, j)),
            pl.BlockSpec((None, d, TN_UP), lambda i, j: (layer, 0, j + nj)),
            pl.BlockSpec((None, FFN_CONV_LEN, TN_UP), lambda i, j: (layer, 0, j)),
            pl.BlockSpec((None, FFN_CONV_LEN, TN_UP), lambda i, j: (layer, 0, j + nj)),
        ],
        out_specs=pl.BlockSpec((TM_UP, TN_UP), lambda i, j: (i, j)),
        out_shape=jax.ShapeDtypeStruct((m, D_FF), BF16),
        scratch_shapes=[pltpu.VMEM((TM_UP, d), BF16), pltpu.VMEM((FFN_HALO, d), BF16), raw, raw],
        compiler_params=_cparams("parallel", "arbitrary"),
        name="ffn_up",
    )(x, x, nw, w_up, w_up, dw, dw)


def _norm_kernel(x_ref, w_ref, o_ref):
    o_ref[...] = _rms(x_ref[...], w_ref[...]).astype(o_ref.dtype)


def _norm(x, w, dtype):
    m, d = x.shape
    return pl.pallas_call(
        _norm_kernel,
        grid=(m // TM_NORM,),
        in_specs=[pl.BlockSpec((TM_NORM, d), lambda i: (i, 0)), pl.BlockSpec((1, d), lambda i: (0, 0))],
        out_specs=pl.BlockSpec((TM_NORM, d), lambda i: (i, 0)),
        out_shape=jax.ShapeDtypeStruct((m, d), dtype),
        compiler_params=_cparams("parallel"),
        name="rms_norm",
    )(x, w)


def _ffn_down_kernel(a_ref, w_ref, x_ref, o_ref):
    o_ref[...] = x_ref[...] + jnp.dot(a_ref[...], w_ref[...], preferred_element_type=F32)


def _ffn_down(a, w, x, layer):
    m, d = x.shape
    f = a.shape[1]
    return pl.pallas_call(
        _ffn_down_kernel,
        grid=(m // TM_DN, d // TN_DN),
        in_specs=[
            pl.BlockSpec((TM_DN, f), lambda i, j: (i, 0)),
            pl.BlockSpec((None, f, TN_DN), lambda i, j: (layer, 0, j)),
            pl.BlockSpec((TM_DN, TN_DN), lambda i, j: (i, j)),
        ],
        out_specs=pl.BlockSpec((TM_DN, TN_DN), lambda i, j: (i, j)),
        out_shape=jax.ShapeDtypeStruct((m, d), F32),
        compiler_params=_cparams("parallel", "arbitrary"),
        name="ffn_down",
    )(a, w, x)


def _rope_tables():
    half = HEAD_DIM // 2
    inv = ROPE_THETA ** (-np.arange(half, dtype=np.float64) / half)

    def tab(pos):
        ang = np.asarray(pos, np.float64)[:, None] * inv[None, :]
        cos, sin = np.cos(ang), np.sin(ang)
        return (jnp.asarray(np.concatenate([cos, cos], -1), F32),
                jnp.asarray(np.concatenate([-sin, sin], -1), F32))

    cs, sn = tab(np.arange(SEQ))
    ccs, csn = tab(np.arange(SEQ // CMP_STRIDE) * CMP_STRIDE + CMP_LEN - 1)
    return cs, sn, ccs, csn


def _reorder_kernel(w_ref, o_ref):
    a0 = 4 * MLSTM_WIDTH
    a1 = a0 + 2 * MLSTM_HEADS
    a2 = a1 + (OFF_SMALL - OFF_CA)
    a3 = a2 + 3 * NSA_HEADS
    cols = o_ref.shape[1]
    o_ref[0:a0, :] = w_ref[0:a0, :].astype(BF16)
    o_ref[a0:OFF_SMALL, :] = w_ref[a1:a2, :].astype(BF16)
    small = jnp.concatenate(
        [w_ref[a0:a1, :], w_ref[a2:a3, :], jnp.zeros((N_PROJ - OFF_SMALL - (a1 - a0) - (a3 - a2), cols), F32)],
        axis=0)
    o_ref[OFF_SMALL:, :] = small.astype(BF16)


def _reorder_w_in(w):
    wt = jnp.swapaxes(w, 1, 2)
    depth, n, d = wt.shape
    return pl.pallas_call(
        _reorder_kernel,
        grid=(depth, d // TK_REORDER),
        in_specs=[pl.BlockSpec((None, n, TK_REORDER), lambda l, k: (l, 0, k))],
        out_specs=pl.BlockSpec((None, N_PROJ, TK_REORDER), lambda l, k: (l, 0, k)),
        out_shape=jax.ShapeDtypeStruct((depth, N_PROJ, d), BF16),
        compiler_params=_cparams("parallel", "parallel"),
        name="w_in_layout",
    )(wt)


def kernel(x, attn_norm_w, w_in, mlstm_i_bias, mlstm_f_bias, mlstm_norm_w, conv_dw_w, conv_dw_b, conv_ln_w, conv_ln_b, nsa_cmp_pe_k, nsa_cmp_w_k, nsa_cmp_pe_v, nsa_cmp_w_v, w_out, ffn_norm_w, w_up, ffn_dw_w, w_down, final_norm_w):
    batch, seq, d = x.shape
    assert seq == SEQ and d == D_MODEL
    depth = w_in.shape[0]
    tabs = _rope_tables()
    w_in_b = _reorder_w_in(w_in)
    w_out_b = w_out.astype(BF16)
    w_up_b = w_up.astype(BF16)
    w_down_b = w_down.astype(BF16)
    w_ck = nsa_cmp_w_k.reshape(depth, CMP_LEN, HEAD_DIM, HEAD_DIM).astype(BF16)
    w_cv = nsa_cmp_w_v.reshape(depth, CMP_LEN, HEAD_DIM, HEAD_DIM).astype(BF16)
    xf = x.reshape(batch * seq, d)
    for l in range(depth):
        proj = _norm_matmul(xf, attn_norm_w[l][None, :], w_in_b, l)
        gate_bias = jnp.concatenate(
            [mlstm_i_bias[l], mlstm_f_bias[l], jnp.zeros((LANE - 2 * MLSTM_HEADS,), F32)])[None, :]
        y_a = _mlstm(proj, gate_bias, mlstm_norm_w[l][None, :], batch)
        y_b = _conformer(proj, conv_dw_w[l], conv_dw_b[l][None, :], conv_ln_w[l][None, :],
                         conv_ln_b[l][None, :], batch)
        prep = _nsa_prep(proj, nsa_cmp_pe_k, w_ck, nsa_cmp_pe_v, w_cv, tabs, batch, l)
        y_c = _nsa(proj, prep, tabs, batch)
        xf = _out_proj(y_a, y_b, y_c, w_out_b, xf, l)
        act = _ffn_up(xf, ffn_norm_w[l][None, :], w_up_b, ffn_dw_w, l)
        xf = _ffn_down(act, w_down_b, xf, l)
    return _norm(xf, final_norm_w[None, :], F32).reshape(batch, seq, d)
```

```python
import numpy as np
import jax
import jax.numpy as jnp
from jax import lax
from jax.experimental import pallas as pl
from jax.experimental.pallas import tpu as pltpu

F32 = jnp.float32
BF16 = jnp.bfloat16

D_MODEL = 2048
SEQ = 2048
HEAD_DIM = 128
MLSTM_HEADS = 4
MLSTM_WIDTH = MLSTM_HEADS * HEAD_DIM
MLSTM_CHUNK = 64
CONV_WIDTH = 512
CONV_LEN = 31
NSA_HEADS = 8
NSA_KV_HEADS = 2
NSA_HPG = NSA_HEADS // NSA_KV_HEADS
NSA_WIDTH = NSA_HEADS * HEAD_DIM
CMP_LEN = 32
CMP_STRIDE = 16
N_CMP = (SEQ - CMP_LEN) // CMP_STRIDE + 1
SEL_BLOCK = 64
SEL_COUNT = 16
N_SEL = SEQ // SEL_BLOCK
WINDOW = 512
ROPE_THETA = 10000.0
D_FF = 5632
FFN_CONV_LEN = 3
EPS = 1e-6
NEG = -1e30
FORCE = 1e4

OFF_MQ, OFF_MK, OFF_MV, OFF_MO = 0, 512, 1024, 1536
OFF_CA, OFF_CG = 2048, 2560
OFF_NQ = 3072
OFF_NKC, OFF_NVC, OFF_NKS, OFF_NVS, OFF_NKW, OFF_NVW = 4096, 4352, 4608, 4864, 5120, 5376
OFF_SMALL = 5632
N_PROJ = 5760
LANE = 128
GATE_LANE0 = 8

VMEM_LIMIT = 52 * 1024 * 1024

TM_IN, TN_IN = 1024, 1152
TM_OUT, TN_OUT = 2048, 512
TM_UP, TN_UP = 1024, 512
TM_DN, TN_DN = 1024, 512
TM_NORM = 512
TK_REORDER = 256
MLSTM_ROWS = 2 * MLSTM_CHUNK
MLSTM_BATCH = 4
CONV_ROWS = 512
CONV_HALO = 32
SUBLANE = 8
CONV_PHASE_ROWS = CONV_HALO + CONV_ROWS - SUBLANE
CONV_CHUNK = 32
FFN_HALO = 16
FFN_EPI_ROWS = 32
TQ = 256
TK = 256
assert TQ == TK and WINDOW == 2 * TK
LOG2E = 1.4426950408889634


def _cparams(*sem):
    return pltpu.CompilerParams(dimension_semantics=sem, vmem_limit_bytes=VMEM_LIMIT)


def _rms(x, w):
    return x * lax.rsqrt(jnp.mean(x * x, axis=-1, keepdims=True) + EPS) * w


def _norm_matmul_kernel(x_ref, nw_ref, w_ref, o_ref, h_scr):
    @pl.when(pl.program_id(1) == 0)
    def _():
        h_scr[...] = _rms(x_ref[...], nw_ref[...]).astype(BF16)

    o_ref[...] = lax.dot_general(h_scr[...], w_ref[...], (((1,), (1,)), ((), ())),
                                 preferred_element_type=F32)


def _norm_matmul(x, nw, w, layer):
    m, d = x.shape
    n = w.shape[1]
    return pl.pallas_call(
        _norm_matmul_kernel,
        grid=(m // TM_IN, n // TN_IN),
        in_specs=[
            pl.BlockSpec((TM_IN, d), lambda i, j: (i, 0)),
            pl.BlockSpec((1, d), lambda i, j: (0, 0)),
            pl.BlockSpec((None, TN_IN, d), lambda i, j: (layer, j, 0)),
        ],
        out_specs=pl.BlockSpec((TM_IN, TN_IN), lambda i, j: (i, j)),
        out_shape=jax.ShapeDtypeStruct((m, n), F32),
        scratch_shapes=[pltpu.VMEM((TM_IN, d), BF16)],
        compiler_params=_cparams("parallel", "arbitrary"),
        name="in_proj",
    )(x, nw, w)


def _log_sigmoid(x):
    return jnp.minimum(x, 0.0) - jnp.log1p(jnp.exp(-jnp.abs(x)))


def _mlstm_kernel(q_ref, k_ref, v_ref, o_ref, g_ref, gb_ref, nw_ref, y_ref, c_scr, n_scr, m_scr):
    L = MLSTM_CHUNK

    @pl.when(pl.program_id(1) == 0)
    def _():
        c_scr[...] = jnp.zeros_like(c_scr)
        n_scr[...] = jnp.zeros_like(n_scr)
        m_scr[...] = jnp.full_like(m_scr, -jnp.inf)

    row = lax.broadcasted_iota(jnp.int32, (L, L), 0)
    col = lax.broadcasted_iota(jnp.int32, (L, L), 1)
    causal = col <= row
    seqs = range(q_ref.shape[0])
    gates = [g_ref[b] + gb_ref[...] for b in seqs]
    in_chunk = lax.broadcasted_iota(jnp.int32, (MLSTM_ROWS, LANE), 0) & (L - 1)

    def chunk_cumsum(x):
        d = 1
        while d < L:
            x = x + jnp.where(in_chunk >= d, pltpu.roll(x, d, axis=0), 0.0)
            d *= 2
        return x

    b_all = [chunk_cumsum(_log_sigmoid(gates[b])) for b in seqs]
    b_t = [x.T for x in b_all]
    g_t = [x.T for x in gates]
    scale = HEAD_DIM ** -0.5

    pairs = [(b, h) for b in seqs for h in range(MLSTM_HEADS)]
    hsl = [slice(h * HEAD_DIM, (h + 1) * HEAD_DIM) for h in range(MLSTM_HEADS)]
    per_head = lambda f: jnp.stack([f(b, h) for b, h in pairs], axis=0)
    k_all = {(b, h): k_ref[b, :, hsl[h]] * scale for b, h in pairs}
    k_allt = {p: k.T for p, k in k_all.items()}
    nw = per_head(lambda b, h: nw_ref[:, hsl[h]])
    c_st = c_scr[...]
    n_st = n_scr[...]
    m_st = m_scr[...][:, :, 0:1]
    bdot = lambda a, b, ca, cb: lax.dot_general(
        a.astype(BF16), b.astype(BF16), (((ca,), (cb,)), ((0,), (0,))), preferred_element_type=F32)
    for c in range(2):
        rs = slice(c * L, (c + 1) * L)
        qc = per_head(lambda b, h: q_ref[b, rs, hsl[h]])
        kc = per_head(lambda b, h: k_all[b, h][rs])
        kct = per_head(lambda b, h: k_allt[b, h][:, rs])
        vc = per_head(lambda b, h: v_ref[b, rs, hsl[h]])
        b_col = per_head(lambda b, h: b_all[b][rs, 4 + h:5 + h])
        i_col = per_head(lambda b, h: gates[b][rs, h:h + 1])
        b_row = per_head(lambda b, h: b_t[b][4 + h:5 + h, rs])
        i_row = per_head(lambda b, h: g_t[b][h:h + 1, rs])
        dmat = jnp.where(causal[None], b_col - b_row + i_row, -jnp.inf)
        inter = b_col + m_st
        m_t = jnp.maximum(inter, jnp.max(dmat, axis=-1, keepdims=True))
        s = bdot(qc, kc, 2, 2) * jnp.exp(dmat - m_t)
        w_inter = jnp.exp(inter - m_t)
        num = bdot(s, vc, 2, 1) + w_inter * bdot(qc, c_st, 2, 1)
        den = (jnp.sum(s, axis=-1, keepdims=True)
               + w_inter * jnp.sum(qc * n_st, axis=-1, keepdims=True))
        hh = num / jnp.maximum(jnp.abs(den), jnp.exp(-m_t))
        b_last = b_col[:, L - 1:L, :]
        g = b_last - b_col + i_col
        m_new = jnp.maximum(b_last + m_st, jnp.max(g, axis=1, keepdims=True))
        w_k = jnp.exp(g - m_new)
        decay = jnp.exp(b_last + m_st - m_new)
        c_st = decay * c_st + bdot(kct, w_k * vc, 2, 1)
        n_st = decay * n_st + jnp.sum(w_k * kc, axis=1, keepdims=True)
        m_st = m_new
        mu = jnp.mean(hh, axis=-1, keepdims=True)
        hc = hh - mu
        hn = hc * lax.rsqrt(jnp.mean(hc * hc, axis=-1, keepdims=True) + EPS) * nw
        for p, (b, h) in enumerate(pairs):
            y_ref[b, rs, hsl[h]] = (jax.nn.sigmoid(o_ref[b, rs, hsl[h]]) * hn[p]).astype(y_ref.dtype)
    c_scr[...] = c_st
    n_scr[...] = n_st
    m_scr[...] = jnp.broadcast_to(m_st, m_scr.shape)


def _mlstm(proj, gate_bias, norm_w, batch):
    m, n = proj.shape
    proj3 = proj.reshape(batch, SEQ, n)
    nsteps = SEQ // MLSTM_ROWS
    nb = MLSTM_BATCH
    npair = nb * MLSTM_HEADS
    blk = lambda cb: pl.BlockSpec((nb, MLSTM_ROWS, MLSTM_WIDTH), lambda b, c, cb=cb: (b, c, cb))
    y = pl.pallas_call(
        _mlstm_kernel,
        grid=(batch // nb, nsteps),
        in_specs=[
            blk(OFF_MQ // MLSTM_WIDTH), blk(OFF_MK // MLSTM_WIDTH),
            blk(OFF_MV // MLSTM_WIDTH), blk(OFF_MO // MLSTM_WIDTH),
            pl.BlockSpec((nb, MLSTM_ROWS, LANE), lambda b, c: (b, c, OFF_SMALL // LANE)),
            pl.BlockSpec((1, LANE), lambda b, c: (0, 0)),
            pl.BlockSpec((1, MLSTM_WIDTH), lambda b, c: (0, 0)),
        ],
        out_specs=pl.BlockSpec((nb, MLSTM_ROWS, MLSTM_WIDTH), lambda b, c: (b, c, 0)),
        out_shape=jax.ShapeDtypeStruct((batch, SEQ, MLSTM_WIDTH), BF16),
        scratch_shapes=[
            pltpu.VMEM((npair, HEAD_DIM, HEAD_DIM), F32),
            pltpu.VMEM((npair, 1, HEAD_DIM), F32),
            pltpu.VMEM((npair, 1, LANE), F32),
        ],
        compiler_params=_cparams("parallel", "arbitrary"),
        name="mlstm",
    )(proj3, proj3, proj3, proj3, proj3, gate_bias, norm_w)
    return y.reshape(m, MLSTM_WIDTH)


def _conv_kernel(a_ref, g_ref, ah_ref, gh_ref, w_ref, b_ref, lw_ref, lb_ref, y_ref, u_scr, s_scr):
    first = pl.program_id(1) == 0
    uh = ah_ref[...] * jax.nn.sigmoid(gh_ref[...])
    u_scr[0:CONV_HALO, :] = jnp.where(first, 0.0, uh)
    u_scr[CONV_HALO:, :] = a_ref[...] * jax.nn.sigmoid(g_ref[...])
    for ph in range(1, SUBLANE):
        s_scr[ph] = u_scr[ph:ph + CONV_PHASE_ROWS, :]
    w = w_ref[...]
    bias = b_ref[...]
    lw = lw_ref[...]
    lb = lb_ref[...]
    base = CONV_HALO - (CONV_LEN - 1)
    for r in range(0, CONV_ROWS, CONV_CHUNK):
        acc = jnp.zeros((CONV_CHUNK, CONV_WIDTH), F32)
        for j in range(CONV_LEN):
            ph, al = (base + j) % SUBLANE, (base + j) // SUBLANE * SUBLANE
            src = u_scr[al + r:al + r + CONV_CHUNK, :] if ph == 0 else s_scr[ph, al + r:al + r + CONV_CHUNK, :]
            acc = acc + w[j:j + 1, :] * src
        u = acc + bias
        mu = jnp.mean(u, axis=-1, keepdims=True)
        uc = u - mu
        y = uc * lax.rsqrt(jnp.mean(uc * uc, axis=-1, keepdims=True) + EPS) * lw + lb
        y_ref[r:r + CONV_CHUNK, :] = (y * jax.nn.sigmoid(y)).astype(y_ref.dtype)


def _conformer(proj, dw_w, dw_b, ln_w, ln_b, batch):
    m = proj.shape[0]
    nt = SEQ // CONV_ROWS
    hb = CONV_ROWS // CONV_HALO
    main = lambda cb: pl.BlockSpec((CONV_ROWS, CONV_WIDTH), lambda b, i, cb=cb: (b * nt + i, cb))
    halo = lambda cb: pl.BlockSpec(
        (CONV_HALO, CONV_WIDTH), lambda b, i, cb=cb: (jnp.maximum((b * nt + i) * hb - 1, 0), cb))
    vec = pl.BlockSpec((1, CONV_WIDTH), lambda b, i: (0, 0))
    return pl.pallas_call(
        _conv_kernel,
        grid=(batch, nt),
        in_specs=[
            main(OFF_CA // CONV_WIDTH), main(OFF_CG // CONV_WIDTH),
            halo(OFF_CA // CONV_WIDTH), halo(OFF_CG // CONV_WIDTH),
            pl.BlockSpec((CONV_LEN, CONV_WIDTH), lambda b, i: (0, 0)),
            vec, vec, vec,
        ],
        out_specs=pl.BlockSpec((CONV_ROWS, CONV_WIDTH), lambda b, i: (b * nt + i, 0)),
        out_shape=jax.ShapeDtypeStruct((m, CONV_WIDTH), BF16),
        scratch_shapes=[pltpu.VMEM((CONV_HALO + CONV_ROWS, CONV_WIDTH), F32),
                        pltpu.VMEM((SUBLANE, CONV_PHASE_ROWS, CONV_WIDTH), F32)],
        compiler_params=_cparams("parallel", "arbitrary"),
        name="conformer_conv",
    )(proj, proj, proj, proj, dw_w, dw_b, ln_w, ln_b)


def _rope(x, cs, sn):
    return x * cs + pltpu.roll(x, HEAD_DIM // 2, axis=1) * sn


def _compress(x_ref, pe_ref, w_ref):
    nrow = SEQ // CMP_STRIDE
    lo = jnp.zeros((nrow, HEAD_DIM), F32)
    hi = jnp.zeros((nrow, HEAD_DIM), F32)
    for l in range(CMP_STRIDE):
        xl = x_ref[pl.ds(l, nrow, stride=CMP_STRIDE), :]
        lo = lo + jnp.dot((xl + pe_ref[l:l + 1, :]).astype(BF16), w_ref[l],
                          preferred_element_type=F32)
        hi = hi + jnp.dot((xl + pe_ref[CMP_STRIDE + l:CMP_STRIDE + l + 1, :]).astype(BF16),
                          w_ref[CMP_STRIDE + l], preferred_element_type=F32)
    return lo + pltpu.roll(hi, nrow - 1, axis=0)


def _nsa_prep_kernel(kc_ref, vc_ref, ks_ref, vs_ref, kw_ref, vw_ref, pek_ref, wk_ref, pev_ref, wv_ref,
                     cs_ref, sn_ref, ccs_ref, csn_ref,
                     kco_ref, vco_ref, kso_ref, vso_ref, kwo_ref, vwo_ref):
    kco_ref[...] = _rope(_compress(kc_ref, pek_ref, wk_ref), ccs_ref[...], csn_ref[...]).astype(BF16)
    vco_ref[...] = _compress(vc_ref, pev_ref, wv_ref).T.astype(BF16)
    cs = cs_ref[...]
    sn = sn_ref[...]
    kso_ref[:, 0:HEAD_DIM] = _rope(ks_ref[...], cs, sn).astype(BF16)
    key = lax.broadcasted_iota(jnp.int32, (SEQ, LANE), 0)
    blk = lax.broadcasted_iota(jnp.int32, (SEQ, LANE), 1)
    kso_ref[:, HEAD_DIM:] = jnp.where(key // SEL_BLOCK == blk, 1.0, 0.0).astype(BF16)
    kwo_ref[...] = _rope(kw_ref[...], cs, sn).astype(BF16)
    for kt in range(SEQ // TK):
        vso_ref[kt] = vs_ref[kt * TK:(kt + 1) * TK, :].T.astype(BF16)
        vwo_ref[kt] = vw_ref[kt * TK:(kt + 1) * TK, :].T.astype(BF16)


def _nsa_prep(proj, pe_k, w_k, pe_v, w_v, tabs, batch, layer):
    cs, sn, ccs, csn = tabs
    G = NSA_KV_HEADS
    ncr = SEQ // CMP_STRIDE
    col = lambda off: pl.BlockSpec((SEQ, HEAD_DIM), lambda b, g, off=off: (b, off // HEAD_DIM + g))
    full2 = lambda r: pl.BlockSpec((r, HEAD_DIM), lambda b, g: (0, 0))
    pespec = pl.BlockSpec((None, CMP_LEN, HEAD_DIM), lambda b, g: (layer, 0, 0))
    wspec = pl.BlockSpec((None, CMP_LEN, HEAD_DIM, HEAD_DIM), lambda b, g: (layer, 0, 0, 0))
    ocmp = pl.BlockSpec((None, None, ncr, HEAD_DIM), lambda b, g: (b, g, 0, 0))
    oseq = pl.BlockSpec((None, None, SEQ, HEAD_DIM), lambda b, g: (b, g, 0, 0))
    oext = pl.BlockSpec((None, None, SEQ, HEAD_DIM + LANE), lambda b, g: (b, g, 0, 0))
    cmp_shape = jax.ShapeDtypeStruct((batch, G, ncr, HEAD_DIM), BF16)
    seq_shape = jax.ShapeDtypeStruct((batch, G, SEQ, HEAD_DIM), BF16)
    ext_shape = jax.ShapeDtypeStruct((batch, G, SEQ, HEAD_DIM + LANE), BF16)
    oval = pl.BlockSpec((None, None, SEQ // TK, HEAD_DIM, TK), lambda b, g: (b, g, 0, 0, 0))
    val_shape = jax.ShapeDtypeStruct((batch, G, SEQ // TK, HEAD_DIM, TK), BF16)
    return pl.pallas_call(
        _nsa_prep_kernel,
        grid=(batch, G),
        in_specs=[
            col(OFF_NKC), col(OFF_NVC), col(OFF_NKS), col(OFF_NVS), col(OFF_NKW), col(OFF_NVW),
            pespec, wspec, pespec, wspec,
            full2(SEQ), full2(SEQ), full2(ncr), full2(ncr),
        ],
        out_specs=[ocmp, ocmp, oext, oval, oseq, oval],
        out_shape=[cmp_shape, cmp_shape, ext_shape, val_shape, seq_shape, val_shape],
        compiler_params=_cparams("parallel", "arbitrary"),
        name="nsa_prep",
    )(proj, proj, proj, proj, proj, proj, pe_k, w_k, pe_v, w_v, cs, sn, ccs, csn)


def _attend(k, v, q, bias, carry, acc_ref):
    s = jnp.dot(k, q, preferred_element_type=F32)
    if bias is not None:
        s = s + bias
    if carry is None:
        m_new = jnp.max(s, axis=0, keepdims=True)
        p = jnp.exp2(s - m_new)
        acc_ref[...] = jnp.dot(v, p.astype(BF16), preferred_element_type=F32)
        return m_new, jnp.sum(p, axis=0, keepdims=True)
    m, l = carry
    m_new = jnp.maximum(m, jnp.max(s, axis=0, keepdims=True))
    alpha = jnp.exp2(m - m_new)
    p = jnp.exp2(s - m_new)
    l = alpha * l + jnp.sum(p, axis=0, keepdims=True)
    acc_ref[...] = alpha * acc_ref[...] + jnp.dot(v, p.astype(BF16), preferred_element_type=F32)
    return m_new, l


def _nsa_kernel(q_ref, sm_ref, cs_ref, sn_ref, kc_ref, vc_ref, ks_ref, vs_ref, kw_ref, vw_ref, y_ref,
                as_scr, aw_scr):
    g = pl.program_id(1)
    qi = pl.program_id(2)
    q0 = qi * TQ
    R = NSA_HPG * TQ
    scale = HEAD_DIM ** -0.5 * LOG2E
    cs = cs_ref[...]
    sn = sn_ref[...]
    q = jnp.concatenate(
        [(_rope(q_ref[:, h * HEAD_DIM:(h + 1) * HEAD_DIM], cs, sn) * scale).T.astype(BF16)
         for h in range(NSA_HPG)], axis=1)
    tok = lax.broadcasted_iota(jnp.int32, (LANE, R), 1) & (TQ - 1)
    rowi = lax.broadcasted_iota(jnp.int32, (LANE, R), 0)

    ncr = SEQ // CMP_STRIDE
    valid_c = (rowi * CMP_STRIDE + (CMP_LEN - 1)) <= q0 + tok
    s_c = jnp.dot(kc_ref[...], q, preferred_element_type=F32) + jnp.where(valid_c, 0.0, NEG)
    m_c = jnp.max(s_c, axis=0, keepdims=True)
    e_c = jnp.where(valid_c, jnp.exp2(s_c - m_c), 0.0)
    p_c = e_c * (1.0 / jnp.maximum(jnp.sum(e_c, axis=0, keepdims=True), 1e-30))
    o_cmp = jnp.dot(vc_ref[...], p_c.astype(BF16), preferred_element_type=F32)
    p_sum = p_c[:, 0:TQ] + p_c[:, TQ:2 * TQ] + p_c[:, 2 * TQ:3 * TQ] + p_c[:, 3 * TQ:4 * TQ]

    jj = lax.broadcasted_iota(jnp.int32, (LANE, ncr), 0)
    nn = lax.broadcasted_iota(jnp.int32, (LANE, ncr), 1)
    ov_t = ((nn * CMP_STRIDE <= jj * SEL_BLOCK + (SEL_BLOCK - 1))
            & (nn * CMP_STRIDE + (CMP_LEN - 1) >= jj * SEL_BLOCK)
            & (jj < N_SEL) & (nn < N_CMP)).astype(F32)
    hi = p_sum.astype(BF16)
    r1 = p_sum - hi.astype(F32)
    mid = r1.astype(BF16)
    lo = (r1 - mid.astype(F32)).astype(BF16)
    imp_t = jnp.dot(jnp.concatenate([ov_t.astype(BF16)] * 3, axis=1), jnp.concatenate([hi, mid, lo], axis=0),
                    preferred_element_type=F32)[0:N_SEL]
    j_i = lax.broadcasted_iota(jnp.int32, (N_SEL, TQ), 0)
    t_i = q0 + lax.broadcasted_iota(jnp.int32, (N_SEL, TQ), 1)
    qblk = t_i // SEL_BLOCK
    forced = (j_i == 0) | (j_i == qblk) | (j_i == qblk - 1)
    score = jnp.where(forced, FORCE, jnp.where(j_i * SEL_BLOCK <= t_i, imp_t, -FORCE))
    rank = jnp.zeros((N_SEL, TQ), jnp.int32)
    for jp in range(N_SEL):
        row = score[jp:jp + 1, :]
        beats = (row > score) | ((row == score) & (j_i > jp))
        rank = rank + beats.astype(jnp.int32)
    sel_bias_t = jnp.concatenate(
        [jnp.where(rank < SEL_COUNT, 0.0, NEG), jnp.zeros((LANE - N_SEL, TQ), F32)], axis=0)
    qx = jnp.concatenate([q, jnp.concatenate([sel_bias_t.astype(BF16)] * NSA_HPG, axis=1)], axis=0)

    keyi = lax.broadcasted_iota(jnp.int32, (TK, R), 0)
    toki = lax.broadcasted_iota(jnp.int32, (TK, R), 1) & (TQ - 1)
    causal = jnp.where(keyi <= toki, 0.0, NEG)

    diag = pl.multiple_of(q0, TK)

    def sel_step(kt, carry):
        k0 = pl.multiple_of(kt * TK, TK)
        return _attend(ks_ref[pl.ds(k0, TK), :], vs_ref[kt], qx, None, carry, as_scr)

    carry = _attend(ks_ref[pl.ds(diag, TK), :], vs_ref[qi], qx, causal, None, as_scr)
    _, l_s = lax.fori_loop(0, qi, sel_step, carry)

    carry = _attend(kw_ref[pl.ds(diag, TK), :], vw_ref[qi], q, causal, None, aw_scr)
    t1 = jnp.maximum(qi - 1, 0)
    carry = _attend(kw_ref[pl.ds(pl.multiple_of(t1 * TK, TK), TK), :], vw_ref[t1], q,
                    jnp.where(qi >= 1, jnp.zeros((TK, R), F32), NEG), carry, aw_scr)
    t2 = jnp.maximum(qi - 2, 0)
    _, l_w = _attend(kw_ref[pl.ds(pl.multiple_of(t2 * TK, TK), TK), :], vw_ref[t2], q,
                     jnp.where(qi >= 2, jnp.where(keyi > toki, 0.0, NEG), NEG), carry, aw_scr)

    gl = jax.nn.sigmoid(sm_ref[...]).T
    o_slc = as_scr[...] * (1.0 / l_s)
    o_win = aw_scr[...] * (1.0 / l_w)
    for h in range(NSA_HPG):
        cs_ = slice(h * TQ, (h + 1) * TQ)
        gate = []
        for c in range(3):
            r0 = GATE_LANE0 + h * 3 + c
            r1 = r0 + NSA_HPG * 3
            gate.append(jnp.where(g == 0, gl[r0:r0 + 1, :], gl[r1:r1 + 1, :]))
        y = gate[0] * o_cmp[:, cs_] + gate[1] * o_slc[:, cs_] + gate[2] * o_win[:, cs_]
        y_ref[:, h * HEAD_DIM:(h + 1) * HEAD_DIM] = y.T.astype(y_ref.dtype)


def _nsa(proj, prep, tabs, batch):
    kc, vc, ks, vs, kw, vw = prep
    cs, sn = tabs[0], tabs[1]
    m = proj.shape[0]
    G = NSA_KV_HEADS
    nq = SEQ // TQ
    ncr = SEQ // CMP_STRIDE
    gw = NSA_HPG * HEAD_DIM
    cmp_spec = pl.BlockSpec((None, None, ncr, HEAD_DIM), lambda b, g, i: (b, g, 0, 0))
    seq_spec = pl.BlockSpec((None, None, SEQ, HEAD_DIM), lambda b, g, i: (b, g, 0, 0))
    ext_spec = pl.BlockSpec((None, None, SEQ, HEAD_DIM + LANE), lambda b, g, i: (b, g, 0, 0))
    val_spec = pl.BlockSpec((None, None, SEQ // TK, HEAD_DIM, TK), lambda b, g, i: (b, g, 0, 0, 0))
    tab_spec = pl.BlockSpec((TQ, HEAD_DIM), lambda b, g, i: (i, 0))
    return pl.pallas_call(
        _nsa_kernel,
        grid=(batch, G, nq),
        in_specs=[
            pl.BlockSpec((TQ, gw), lambda b, g, i: (b * nq + i, OFF_NQ // gw + g)),
            pl.BlockSpec((TQ, LANE), lambda b, g, i: (b * nq + i, OFF_SMALL // LANE)),
            tab_spec, tab_spec, cmp_spec, cmp_spec, ext_spec, val_spec, seq_spec, val_spec,
        ],
        out_specs=pl.BlockSpec((TQ, gw), lambda b, g, i: (b * nq + i, g)),
        out_shape=jax.ShapeDtypeStruct((m, NSA_WIDTH), BF16),
        scratch_shapes=[pltpu.VMEM((HEAD_DIM, NSA_HPG * TQ), F32), pltpu.VMEM((HEAD_DIM, NSA_HPG * TQ), F32)],
        compiler_params=_cparams("parallel", "parallel", "arbitrary"),
        name="nsa_attn",
    )(proj, proj, cs, sn, kc, vc, ks, vs, kw, vw)


def _out_proj_kernel(ya_ref, yb_ref, yc_ref, wa_ref, wb_ref, wc_ref, x_ref, o_ref):
    acc = jnp.dot(ya_ref[...], wa_ref[...], preferred_element_type=F32)
    acc = acc + jnp.dot(yb_ref[...], wb_ref[...], preferred_element_type=F32)
    acc = acc + jnp.dot(yc_ref[...], wc_ref[...], preferred_element_type=F32)
    o_ref[...] = x_ref[...] + acc


def _out_proj(ya, yb, yc, w, x, layer):
    m, d = x.shape
    wa, wb, wc = ya.shape[1], yb.shape[1], yc.shape[1]
    return pl.pallas_call(
        _out_proj_kernel,
        grid=(m // TM_OUT, d // TN_OUT),
        in_specs=[
            pl.BlockSpec((TM_OUT, wa), lambda i, j: (i, 0)),
            pl.BlockSpec((TM_OUT, wb), lambda i, j: (i, 0)),
            pl.BlockSpec((TM_OUT, wc), lambda i, j: (i, 0)),
            pl.BlockSpec((None, wa, TN_OUT), lambda i, j: (layer, 0, j)),
            pl.BlockSpec((None, wb, TN_OUT), lambda i, j: (layer, wa // wb, j)),
            pl.BlockSpec((None, wc, TN_OUT), lambda i, j: (layer, (wa + wb) // wc, j)),
            pl.BlockSpec((TM_OUT, TN_OUT), lambda i, j: (i, j)),
        ],
        out_specs=pl.BlockSpec((TM_OUT, TN_OUT), lambda i, j: (i, j)),
        out_shape=jax.ShapeDtypeStruct((m, d), F32),
        compiler_params=_cparams("parallel", "arbitrary"),
        name="out_proj",
    )(ya, yb, yc, w, w, w, x)


def _ffn_up_kernel(x_ref, xh_ref, nw_ref, wg_ref, wv_ref, cg_ref, cv_ref, o_ref,
                   h_scr, hh_scr, g_scr, v_scr):
    i = pl.program_id(0)

    @pl.when(pl.program_id(1) == 0)
    def _():
        h_scr[...] = _rms(x_ref[...], nw_ref[...]).astype(BF16)
        hh_scr[...] = _rms(xh_ref[...], nw_ref[...]).astype(BF16)

    keep = jnp.where((i * TM_UP) % SEQ == 0, 0.0, 1.0)
    cg = 0.5 * cg_ref[...]
    cv = cv_ref[...]

    for w_ref, scr in ((wg_ref, g_scr), (wv_ref, v_scr)):
        w = w_ref[...]
        scr[0:FFN_HALO, :] = jnp.dot(hh_scr[...], w, preferred_element_type=F32) * keep
        scr[FFN_HALO:, :] = jnp.dot(h_scr[...], w, preferred_element_type=F32)

    def conv(scr, c, r):
        lo = FFN_HALO + r - SUBLANE
        win = scr[lo:lo + SUBLANE + FFN_EPI_ROWS, :]
        out = c[FFN_CONV_LEN - 1:FFN_CONV_LEN, :] * win[SUBLANE:, :]
        for t in range(FFN_CONV_LEN - 1):
            back = FFN_CONV_LEN - 1 - t
            out = out + c[t:t + 1, :] * pltpu.roll(win, back, axis=0)[SUBLANE:, :]
        return out

    for r in range(0, TM_UP, FFN_EPI_ROWS):
        half = conv(g_scr, cg, r)
        val = conv(v_scr, cv, r)
        o_ref[r:r + FFN_EPI_ROWS, :] = ((half + half * jnp.tanh(half)) * val).astype(o_ref.dtype)


def _ffn_up(x, nw, w_up, dw, layer):
    m, d = x.shape
    nj = D_FF // TN_UP
    hb = TM_UP // FFN_HALO
    raw = pltpu.VMEM((FFN_HALO + TM_UP, TN_UP), F32)
    return pl.pallas_call(
        _ffn_up_kernel,
        grid=(m // TM_UP, nj),
        in_specs=[
            pl.BlockSpec((TM_UP, d), lambda i, j: (i, 0)),
            pl.BlockSpec((FFN_HALO, d), lambda i, j: (jnp.maximum(i * hb - 1, 0), 0)),
            pl.BlockSpec((1, d), lambda i, j: (0, 0)),
            pl.BlockSpec((None, d, TN_UP), lambda i, j: (layer, 0, j)),
            pl.BlockSpec((None, d, TN_UP), lambda i, j: (layer, 0, j + nj)),
            pl.BlockSpec((None, FFN_CONV_LEN, TN_UP), lambda i, j: (layer, 0, j)),
            pl.BlockSpec((None, FFN_CONV_LEN, TN_UP), lambda i, j: (layer, 0, j + nj)),
        ],
        out_specs=pl.BlockSpec((TM_UP, TN_UP), lambda i, j: (i, j)),
        out_shape=jax.ShapeDtypeStruct((m, D_FF), BF16),
        scratch_shapes=[pltpu.VMEM((TM_UP, d), BF16), pltpu.VMEM((FFN_HALO, d), BF16), raw, raw],
        compiler_params=_cparams("parallel", "arbitrary"),
        name="ffn_up",
    )(x, x, nw, w_up, w_up, dw, dw)


def _norm_kernel(x_ref, w_ref, o_ref):
    o_ref[...] = _rms(x_ref[...], w_ref[...]).astype(o_ref.dtype)


def _norm(x, w, dtype):
    m, d = x.shape
    return pl.pallas_call(
        _norm_kernel,
        grid=(m // TM_NORM,),
        in_specs=[pl.BlockSpec((TM_NORM, d), lambda i: (i, 0)), pl.BlockSpec((1, d), lambda i: (0, 0))],
        out_specs=pl.BlockSpec((TM_NORM, d), lambda i: (i, 0)),
        out_shape=jax.ShapeDtypeStruct((m, d), dtype),
        compiler_params=_cparams("parallel"),
        name="rms_norm",
    )(x, w)


def _ffn_down_kernel(a_ref, w_ref, x_ref, o_ref):
    o_ref[...] = x_ref[...] + jnp.dot(a_ref[...], w_ref[...], preferred_element_type=F32)


def _ffn_down(a, w, x, layer):
    m, d = x.shape
    f = a.shape[1]
    return pl.pallas_call(
        _ffn_down_kernel,
        grid=(m // TM_DN, d // TN_DN),
        in_specs=[
            pl.BlockSpec((TM_DN, f), lambda i, j: (i, 0)),
            pl.BlockSpec((None, f, TN_DN), lambda i, j: (layer, 0, j)),
            pl.BlockSpec((TM_DN, TN_DN), lambda i, j: (i, j)),
        ],
        out_specs=pl.BlockSpec((TM_DN, TN_DN), lambda i, j: (i, j)),
        out_shape=jax.ShapeDtypeStruct((m, d), F32),
        compiler_params=_cparams("parallel", "arbitrary"),
        name="ffn_down",
    )(a, w, x)


def _rope_tables():
    half = HEAD_DIM // 2
    inv = ROPE_THETA ** (-np.arange(half, dtype=np.float64) / half)

    def tab(pos):
        ang = np.asarray(pos, np.float64)[:, None] * inv[None, :]
        cos, sin = np.cos(ang), np.sin(ang)
        return (jnp.asarray(np.concatenate([cos, cos], -1), F32),
                jnp.asarray(np.concatenate([-sin, sin], -1), F32))

    cs, sn = tab(np.arange(SEQ))
    ccs, csn = tab(np.arange(SEQ // CMP_STRIDE) * CMP_STRIDE + CMP_LEN - 1)
    return cs, sn, ccs, csn


def _reorder_kernel(w_ref, o_ref):
    a0 = 4 * MLSTM_WIDTH
    a1 = a0 + 2 * MLSTM_HEADS
    a2 = a1 + (OFF_SMALL - OFF_CA)
    a3 = a2 + 3 * NSA_HEADS
    cols = o_ref.shape[1]
    o_ref[0:a0, :] = w_ref[0:a0, :].astype(BF16)
    o_ref[a0:OFF_SMALL, :] = w_ref[a1:a2, :].astype(BF16)
    small = jnp.concatenate(
        [w_ref[a0:a1, :], w_ref[a2:a3, :], jnp.zeros((N_PROJ - OFF_SMALL - (a1 - a0) - (a3 - a2), cols), F32)],
        axis=0)
    o_ref[OFF_SMALL:, :] = small.astype(BF16)


def _reorder_w_in(w):
    wt = jnp.swapaxes(w, 1, 2)
    depth, n, d = wt.shape
    return pl.pallas_call(
        _reorder_kernel,
        grid=(depth, d // TK_REORDER),
        in_specs=[pl.BlockSpec((None, n, TK_REORDER), lambda l, k: (l, 0, k))],
        out_specs=pl.BlockSpec((None, N_PROJ, TK_REORDER), lambda l, k: (l, 0, k)),
        out_shape=jax.ShapeDtypeStruct((depth, N_PROJ, d), BF16),
        compiler_params=_cparams("parallel", "parallel"),
        name="w_in_layout",
    )(wt)


def kernel(x, attn_norm_w, w_in, mlstm_i_bias, mlstm_f_bias, mlstm_norm_w, conv_dw_w, conv_dw_b, conv_ln_w, conv_ln_b, nsa_cmp_pe_k, nsa_cmp_w_k, nsa_cmp_pe_v, nsa_cmp_w_v, w_out, ffn_norm_w, w_up, ffn_dw_w, w_down, final_norm_w):
    batch, seq, d = x.shape
    assert seq == SEQ and d == D_MODEL
    depth = w_in.shape[0]
    tabs = _rope_tables()
    w_in_b = _reorder_w_in(w_in)
    w_out_b = w_out.astype(BF16)
    w_up_b = w_up.astype(BF16)
    w_down_b = w_down.astype(BF16)
    w_ck = nsa_cmp_w_k.reshape(depth, CMP_LEN, HEAD_DIM, HEAD_DIM).astype(BF16)
    w_cv = nsa_cmp_w_v.reshape(depth, CMP_LEN, HEAD_DIM, HEAD_DIM).astype(BF16)
    xf = x.reshape(batch * seq, d)
    for l in range(depth):
        proj = _norm_matmul(xf, attn_norm_w[l][None, :], w_in_b, l)
        gate_bias = jnp.concatenate(
            [mlstm_i_bias[l], mlstm_f_bias[l], jnp.zeros((LANE - 2 * MLSTM_HEADS,), F32)])[None, :]
        y_a = _mlstm(proj, gate_bias, mlstm_norm_w[l][None, :], batch)
        y_b = _conformer(proj, conv_dw_w[l], conv_dw_b[l][None, :], conv_ln_w[l][None, :],
                         conv_ln_b[l][None, :], batch)
        prep = _nsa_prep(proj, nsa_cmp_pe_k, w_ck, nsa_cmp_pe_v, w_cv, tabs, batch, l)
        y_c = _nsa(proj, prep, tabs, batch)
        xf = _out_proj(y_a, y_b, y_c, w_out_b, xf, l)
        act = _ffn_up(xf, ffn_norm_w[l][None, :], w_up_b, ffn_dw_w, l)
        xf = _ffn_down(act, w_down_b, xf, l)
    return _norm(xf, final_norm_w[None, :], F32).reshape(batch, seq, d)
```
